```python
import math
import jax, jax.numpy as jnp
from jax import lax
import numpy as np

D_MODEL = 2048
BATCH = 4
SEQ = 4096
DEPTH = 4

GRID_W = 64
CTX_LEN = 256
S5_WIDTH = D_MODEL // 2
S5_GROUP = 16
S5_GROUPS = S5_WIDTH // S5_GROUP
S5_STATE = 64
RET_WIDTH = D_MODEL - S5_WIDTH
RET_HEAD_DIM = 256
RET_HEADS = RET_WIDTH // RET_HEAD_DIM
RET_CHUNK = 128
ROPE_BASE = 10000.0
IN_WIDTH = S5_WIDTH + 4 * RET_WIDTH
IN_SPLITS = (S5_WIDTH, S5_WIDTH + RET_WIDTH, S5_WIDTH + 2 * RET_WIDTH, S5_WIDTH + 3 * RET_WIDTH)
N_EXPERTS = 16
N_EXPERT_GROUPS = 4
EXPERTS_PER_GROUP = N_EXPERTS // N_EXPERT_GROUPS
TOP_K = 2
EXPERT_FF = 1024
MOE_BLOCK = 128
NORM_EPS = 1e-6
F32 = jnp.float32

kernel_name = "hybrid_s5_retention_grouped_moe_dit"


def rms_norm(t, gain):
    tf = t.astype(F32)
    tf = tf * lax.rsqrt(jnp.mean(tf * tf, axis=-1, keepdims=True) + NORM_EPS)
    return (tf * gain.astype(F32)).astype(t.dtype)


def modulate(h, shift, scale):
    return h * (1.0 + scale) + shift


def s5_discretize(a_re, a_im, log_dt, b_re, b_im):
    a = lax.complex(a_re.astype(F32), a_im.astype(F32))
    dt_a = jnp.exp(log_dt.astype(F32))[:, None] * a
    a_bar = jnp.exp(dt_a)
    b_bar = ((a_bar - 1.0) / a)[..., None] * lax.complex(b_re.astype(F32), b_im.astype(F32))
    return dt_a, a_bar, b_bar


def _linear_recurrence_combine(left, right):
    a_l, b_l = left
    a_r, b_r = right
    return a_r * a_l, a_r * b_l + b_r


def s5_scan(u, dt_a, a_bar, b_bar, c_mat, h0=None):
    length = u.shape[0]
    bu = jnp.einsum('lbgc,gnc->lbgn', u.astype(jnp.complex64), b_bar)
    a_seq = jnp.broadcast_to(a_bar[None, None], (length, 1) + a_bar.shape)
    _, h = lax.associative_scan(_linear_recurrence_combine, (a_seq, bu), axis=0)
    if h0 is not None:
        steps = jnp.arange(1, length + 1, dtype=F32)[:, None, None, None]
        h = h + jnp.exp(steps * dt_a[None, None]) * h0[None]
    y = jnp.einsum('lbgn,gcn->lbgc', h, c_mat).real
    return y, h[-1]


def s5_mixer(u_c, u_l, a_re, a_im, log_dt, b_re, b_im, c_re, c_im, d_skip, glu_w, glu_b):
    def to_scan(u):
        b, l, _ = u.shape
        return jnp.transpose(u.astype(F32).reshape(b, l, S5_GROUPS, S5_GROUP), (1, 0, 2, 3))

    sc, sl = to_scan(u_c), to_scan(u_l)
    ys_c, ys_l = [], []
    for direction in range(2):
        flip = (lambda t: jnp.flip(t, 0)) if direction == 1 else (lambda t: t)
        dt_a, a_bar, b_bar = s5_discretize(a_re[direction], a_im[direction], log_dt[direction],
                                           b_re[direction], b_im[direction])
        c_mat = lax.complex(c_re[direction].astype(F32), c_im[direction].astype(F32))
        yc, h_ctx = s5_scan(flip(sc), dt_a, a_bar, b_bar, c_mat)
        yl, _ = s5_scan(flip(sl), dt_a, a_bar, b_bar, c_mat, h_ctx)
        ys_c.append(flip(yc))
        ys_l.append(flip(yl))

    def finish(y, u):
        b, l, w = u.shape
        y = jnp.transpose(y, (1, 0, 2, 3)).reshape(b, l, w) + d_skip.astype(F32) * u.astype(F32)
        y = jax.nn.gelu(y)
        y = y * jax.nn.sigmoid(y @ glu_w.astype(F32) + glu_b.astype(F32))
        return y.astype(u.dtype)

    return finish(ys_c[0] + ys_c[1], u_c), finish(ys_l[0] + ys_l[1], u_l)


def axial_rope_tables(n_rows):
    row = jnp.broadcast_to(jnp.arange(n_rows, dtype=F32)[:, None], (n_rows, GRID_W)).reshape(-1)
    col = jnp.broadcast_to(jnp.arange(GRID_W, dtype=F32)[None, :], (n_rows, GRID_W)).reshape(-1)
    quarter = RET_HEAD_DIM // 4
    freqs = ROPE_BASE ** (-jnp.arange(quarter, dtype=F32) / quarter)
    ang = jnp.concatenate([row[:, None] * freqs, col[:, None] * freqs], axis=-1)
    return jnp.cos(ang), jnp.sin(ang)


def apply_rope(t, cos, sin):
    half = t.shape[-1] // 2
    t1, t2 = t[..., :half], t[..., half:]
    return jnp.concatenate([t1 * cos - t2 * sin, t2 * cos + t1 * sin], axis=-1)


def retention_chunkwise(q, k, v, log_gamma, state0):
    bsz, n_heads, length, dk = q.shape
    dv = v.shape[-1]
    n_chunks = length // RET_CHUNK
    qc = q.reshape(bsz, n_heads, n_chunks, RET_CHUNK, dk)
    kc = k.reshape(bsz, n_heads, n_chunks, RET_CHUNK, dk)
    vc = v.reshape(bsz, n_heads, n_chunks, RET_CHUNK, dv)
    pos = jnp.arange(RET_CHUNK, dtype=F32)
    rel = pos[:, None] - pos[None, :]
    decay = jnp.where(rel >= 0, jnp.exp(log_gamma[:, None, None] * jnp.maximum(rel, 0.0)), 0.0)
    scores = jnp.einsum('bhnid,bhnjd->bhnij', qc, kc) * decay[None, :, None]
    intra = jnp.einsum('bhnij,bhnje->bhnie', scores, vc)
    k_decay = jnp.exp(log_gamma[:, None] * (RET_CHUNK - 1 - pos))
    chunk_kv = jnp.einsum('bhnjd,bhnje->nbhde', kc * k_decay[None, :, None, :, None], vc)
    chunk_gamma = jnp.exp(log_gamma * RET_CHUNK)[None, :, None, None]

    def step(state, kv):
        return state * chunk_gamma + kv, state

    final, prev = lax.scan(step, state0, chunk_kv)
    q_decay = jnp.exp(log_gamma[:, None] * (pos + 1.0))
    inter = jnp.einsum('bhnid,nbhde->bhnie', qc, prev) * q_decay[None, :, None, :, None]
    return (intra + inter).reshape(bsz, n_heads, length, dv), final


def split_heads(t):
    b, l, _ = t.shape
    return jnp.transpose(t.reshape(b, l, RET_HEADS, RET_HEAD_DIM), (0, 2, 1, 3)).astype(F32)


def retention_mixer(q_c, k_c, v_c, g_c, q_l, k_l, v_l, g_l, decay_raw, rope_cos, rope_sin):
    log_gamma = jnp.log1p(-jnp.exp2(-decay_raw.astype(F32)))
    scale = RET_HEAD_DIM ** -0.5
    qc, kc, vc = split_heads(q_c) * scale, split_heads(k_c), split_heads(v_c)
    ql = apply_rope(split_heads(q_l), rope_cos, rope_sin) * scale
    kl = apply_rope(split_heads(k_l), rope_cos, rope_sin)
    vl = split_heads(v_l)
    bsz = q_l.shape[0]
    zero_state = jnp.zeros((bsz, RET_HEADS, RET_HEAD_DIM, RET_HEAD_DIM), F32)
    outs_c, outs_l = [], []
    for direction in range(2):
        flip = (lambda t: jnp.flip(t, 2)) if direction == 1 else (lambda t: t)
        oc, s_ctx = retention_chunkwise(flip(qc), flip(kc), flip(vc), log_gamma[direction], zero_state)
        ol, _ = retention_chunkwise(flip(ql), flip(kl), flip(vl), log_gamma[direction], s_ctx)
        outs_c.append(flip(oc))
        outs_l.append(flip(ol))

    def finish(o, g):
        o = o * lax.rsqrt(jnp.mean(o * o, axis=-1, keepdims=True) + NORM_EPS)
        o = jnp.transpose(o, (0, 2, 1, 3)).reshape(g.shape[0], g.shape[1], RET_WIDTH)
        return (o * jax.nn.silu(g.astype(F32))).astype(g.dtype)

    return finish(outs_c[0] + outs_c[1], g_c), finish(outs_l[0] + outs_l[1], g_l)


def route_grouped(tokens, router_w, router_b):
    n_tok = tokens.shape[0]
    scores = jax.nn.sigmoid(tokens.astype(F32) @ router_w.astype(F32))
    biased = (scores + router_b.astype(F32)).reshape(n_tok, N_EXPERT_GROUPS, EXPERTS_PER_GROUP)
    group_score = lax.top_k(biased, TOP_K)[0].sum(-1)
    best_group = jnp.argmax(group_score, axis=-1)
    in_group = jnp.take_along_axis(biased, best_group[:, None, None], axis=1)[:, 0]
    _, local = lax.top_k(in_group, TOP_K)
    expert_idx = best_group[:, None] * EXPERTS_PER_GROUP + local
    sel = jnp.take_along_axis(scores, expert_idx, axis=-1)
    return expert_idx, sel / jnp.sum(sel, axis=-1, keepdims=True)


def moe_ffn(tokens, expert_idx, gate_w, w_gate, w_up, w_down):
    n_tok, d = tokens.shape
    n_assign = n_tok * TOP_K
    flat_e = expert_idx.reshape(-1)
    flat_tok = jnp.repeat(jnp.arange(n_tok, dtype=jnp.int32), TOP_K)
    order = jnp.argsort(flat_e)
    se, st, sw = flat_e[order], flat_tok[order], gate_w.reshape(-1)[order]
    counts = jnp.bincount(flat_e, length=N_EXPERTS)
    padded = (counts + MOE_BLOCK - 1) // MOE_BLOCK * MOE_BLOCK
    start = jnp.cumsum(counts) - counts
    padded_end = jnp.cumsum(padded)
    pstart = padded_end - padded
    dest = pstart[se] + jnp.arange(n_assign, dtype=jnp.int32) - start[se]
    n_blocks = -(-n_assign // MOE_BLOCK) + N_EXPERTS
    buf = jnp.zeros((n_blocks * MOE_BLOCK, d), tokens.dtype).at[dest].set(tokens[st])
    block_start = jnp.arange(n_blocks, dtype=jnp.int32) * MOE_BLOCK
    block_expert = jnp.minimum(jnp.searchsorted(padded_end, block_start, side='right'), N_EXPERTS - 1)

    def expert_block(args):
        xb, e = args
        return (jax.nn.silu(xb @ w_gate[e]) * (xb @ w_up[e])) @ w_down[e]

    out_blocks = lax.map(expert_block, (buf.reshape(n_blocks, MOE_BLOCK, d), block_expert))
    y = out_blocks.reshape(-1, d)[dest].astype(F32) * sw[:, None]
    return jnp.zeros((n_tok, d), F32).at[st].add(y).astype(tokens.dtype)


def setup_inputs(seed: int = 0) -> dict:
    key = jax.random.key(seed)
    ks = jax.random.split(key, 32)
    nrm = jax.random.normal
    d, g, n = D_MODEL, S5_GROUPS, S5_STATE
    n_idx = jnp.arange(n, dtype=F32)
    return {
        'x': nrm(ks[0], (BATCH, SEQ, d), F32),
        'c': nrm(ks[1], (BATCH, d), F32),
        'ctx': nrm(ks[2], (BATCH, CTX_LEN, d), F32),
        'c_ctx': nrm(ks[3], (d,), F32),
        'mod_w': nrm(ks[4], (DEPTH, d, 6 * d), F32) * (0.5 * d ** -0.5),
        'mod_b': nrm(ks[5], (DEPTH, 6 * d), F32) * 0.01,
        'norm1_g': 1.0 + 0.02 * nrm(ks[6], (DEPTH, d), F32),
        'norm2_g': 1.0 + 0.02 * nrm(ks[7], (DEPTH, d), F32),
        'w_in': nrm(ks[8], (DEPTH, d, IN_WIDTH), F32) * d ** -0.5,
        's5_a_re': -0.5 + 0.01 * nrm(ks[9], (DEPTH, 2, g, n), F32),
        's5_a_im': math.pi * n_idx + 0.01 * nrm(ks[10], (DEPTH, 2, g, n), F32),
        's5_log_dt': jax.random.uniform(ks[11], (DEPTH, 2, g), F32, math.log(1e-3), math.log(1e-1)),
        's5_b_re': nrm(ks[12], (DEPTH, 2, g, n, S5_GROUP), F32) * (2 * S5_GROUP) ** -0.5,
        's5_b_im': nrm(ks[13], (DEPTH, 2, g, n, S5_GROUP), F32) * (2 * S5_GROUP) ** -0.5,
        's5_c_re': nrm(ks[14], (DEPTH, 2, g, S5_GROUP, n), F32) * (2 * n) ** -0.5,
        's5_c_im': nrm(ks[15], (DEPTH, 2, g, S5_GROUP, n), F32) * (2 * n) ** -0.5,
        's5_d': nrm(ks[16], (DEPTH, S5_WIDTH), F32),
        'glu_w': nrm(ks[17], (DEPTH, S5_WIDTH, S5_WIDTH), F32) * S5_WIDTH ** -0.5,
        'glu_b': nrm(ks[18], (DEPTH, S5_WIDTH), F32) * 0.01,
        'ret_decay_raw': (5.0 + jnp.arange(RET_HEADS, dtype=F32)[None, None, :]
                          + jnp.array([0.0, 0.5], F32)[None, :, None]
                          + 0.05 * nrm(ks[19], (DEPTH, 2, RET_HEADS), F32)),
        'w_out': nrm(ks[20], (DEPTH, d, d), F32) * d ** -0.5,
        'router_w': nrm(ks[21], (d, N_EXPERTS), F32) * d ** -0.5,
        'router_b': nrm(ks[22], (N_EXPERTS,), F32) * 0.01,
        'exp_w_gate': nrm(ks[23], (DEPTH, N_EXPERTS, d, EXPERT_FF), F32) * d ** -0.5,
        'exp_w_up': nrm(ks[24], (DEPTH, N_EXPERTS, d, EXPERT_FF), F32) * d ** -0.5,
        'exp_w_down': nrm(ks[25], (DEPTH, N_EXPERTS, EXPERT_FF, d), F32) * EXPERT_FF ** -0.5,
        'final_norm_g': 1.0 + 0.02 * nrm(ks[26], (d,), F32),
    }


def reference(x, c, ctx, c_ctx, mod_w, mod_b, norm1_g, norm2_g, w_in,
              s5_a_re, s5_a_im, s5_log_dt, s5_b_re, s5_b_im, s5_c_re, s5_c_im, s5_d,
              glu_w, glu_b, ret_decay_raw, w_out, router_w, router_b,
              exp_w_gate, exp_w_up, exp_w_down, final_norm_g):
    bsz, n_lat, d = x.shape
    n_ctx = ctx.shape[1]
    n_rows = n_lat // GRID_W
    rope_cos, rope_sin = axial_rope_tables(n_rows)
    c_act = jax.nn.silu(c)[:, None, :]
    c_ctx_act = jax.nn.silu(c_ctx)[None, None, :]
    for layer in range(DEPTH):
        last = layer == DEPTH - 1
        mod_l = jnp.split(c_act @ mod_w[layer] + mod_b[layer], 6, axis=-1)
        mod_c = jnp.split(c_ctx_act @ mod_w[layer] + mod_b[layer], 6, axis=-1)

        h_l = modulate(rms_norm(x, norm1_g[layer]), mod_l[0], mod_l[1])
        h_c = modulate(rms_norm(ctx, norm1_g[layer]), mod_c[0], mod_c[1])
        p_l = jnp.split(h_l @ w_in[layer], IN_SPLITS, axis=-1)
        p_c = jnp.split(h_c @ w_in[layer], IN_SPLITS, axis=-1)
        s5_c, s5_l = s5_mixer(p_c[0], p_l[0], s5_a_re[layer], s5_a_im[layer], s5_log_dt[layer],
                              s5_b_re[layer], s5_b_im[layer], s5_c_re[layer], s5_c_im[layer],
                              s5_d[layer], glu_w[layer], glu_b[layer])
        ret_c, ret_l = retention_mixer(p_c[1], p_c[2], p_c[3], p_c[4], p_l[1], p_l[2], p_l[3], p_l[4],
                                       ret_decay_raw[layer], rope_cos, rope_sin)
        x = x + mod_l[2] * (jnp.concatenate([s5_l, ret_l], axis=-1) @ w_out[layer])

        f_l = modulate(rms_norm(x, norm2_g[layer]), mod_l[3], mod_l[4]).reshape(-1, d)
        if last:
            tokens = f_l
        else:
            ctx = ctx + mod_c[2] * (jnp.concatenate([s5_c, ret_c], axis=-1) @ w_out[layer])
            f_c = modulate(rms_norm(ctx, norm2_g[layer]), mod_c[3], mod_c[4]).reshape(-1, d)
            tokens = jnp.concatenate([f_c, f_l], axis=0)
        expert_idx, gate_w = route_grouped(tokens, router_w, router_b)
        ff = moe_ffn(tokens, expert_idx, gate_w, exp_w_gate[layer], exp_w_up[layer], exp_w_down[layer])
        if last:
            x = x + mod_l[5] * ff.reshape(bsz, n_lat, d)
        else:
            n_c = bsz * n_ctx
            ctx = ctx + mod_c[5] * ff[:n_c].reshape(bsz, n_ctx, d)
            x = x + mod_l[5] * ff[n_c:].reshape(bsz, n_lat, d)
    return rms_norm(x, final_norm_g)
```

```python
import functools
import math

import jax
import jax.numpy as jnp
from jax import lax
from jax.experimental import pallas as pl
from jax.experimental.pallas import tpu as pltpu

F32 = jnp.float32
BF16 = jnp.bfloat16
HIGHEST = lax.Precision.HIGHEST

GRID_W = 64
S5_GROUP = 16
S5_STATE = 64
S5_STEP = 16
RET_HEAD_DIM = 256
RET_CHUNK = 128
ROPE_BASE = 10000.0
N_EXPERTS = 16
N_EXPERT_GROUPS = 4
EXPERTS_PER_GROUP = 4
TOP_K = 2
NORM_EPS = 1e-6

ROW_BLOCK = 256
MM_ROW_BLOCK = 1024
MM_COL_BLOCK = 1024
MOE_ROWS = 256
S5_GROUPS_PER_STEP = 4
MIB = 1024 * 1024


def _cparams(sem, vmem_mib):
    return pltpu.CompilerParams(dimension_semantics=sem, vmem_limit_bytes=int(vmem_mib * MIB))


def _mods_kernel(c_ref, w_ref, b_ref, o_ref):
    cv = c_ref[...]
    act = cv * jax.nn.sigmoid(cv)
    o_ref[0] = jnp.dot(act, w_ref[0], precision=HIGHEST, preferred_element_type=F32) + b_ref[0]


def _mods(c8, mod_w, mod_b):
    depth, d, n = mod_w.shape
    tn = 1024
    return pl.pallas_call(
        _mods_kernel,
        out_shape=jax.ShapeDtypeStruct((depth, 8, n), F32),
        grid=(depth, n // tn),
        in_specs=[pl.BlockSpec((8, d), lambda l, j: (0, 0)),
                  pl.BlockSpec((1, d, tn), lambda l, j: (l, 0, j)),
                  pl.BlockSpec((1, 1, tn), lambda l, j: (l, 0, j))],
        out_specs=pl.BlockSpec((1, 8, tn), lambda l, j: (l, 0, j)),
        compiler_params=_cparams(("arbitrary", "arbitrary"), 40),
        name="adaln_mods",
    )(c8, mod_w, mod_b.reshape(depth, 1, n))


def _prenorm_kernel(*refs, combine, final, d):
    if combine:
        x_ref, ya_ref, yb_ref, mprev_ref, mcur_ref, g_ref = refs[:6]
        outs = refs[6:]
    else:
        x_ref, mcur_ref, g_ref = refs[:3]
        outs = refs[3:]
    x = x_ref[...]
    if combine:
        gate2 = mprev_ref[:, 5 * d:6 * d]
        x = x + gate2 * (ya_ref[...] + yb_ref[...])
    ms = jnp.mean(x * x, axis=-1, keepdims=True)
    xn = x * lax.rsqrt(ms + NORM_EPS) * g_ref[...]
    if final:
        outs[0][...] = xn
    else:
        outs[0][...] = x
        shift = mcur_ref[:, 0:d]
        scale = mcur_ref[:, d:2 * d]
        outs[1][...] = (xn * (1.0 + scale) + shift).astype(BF16)


def _prenorm(x, ys2, mods_prev, mods_cur, gain, *, n_lat, rows_per_batch, final):
    t, d = x.shape
    tm = ROW_BLOCK
    combine = ys2 is not None
    nblk = (n_lat if final else t) // tm
    lat_blocks = n_lat // tm
    per_b = rows_per_batch // tm
    ctx_row = n_lat // rows_per_batch

    def mod_idx(i):
        return (jnp.where(i < lat_blocks, i // per_b, ctx_row), 0, 0)

    row_spec = pl.BlockSpec((tm, d), lambda i: (i, 0))
    mod_spec = pl.BlockSpec((None, 1, 6 * d), mod_idx)
    g_spec = pl.BlockSpec((1, d), lambda i: (0, 0))
    ins, specs = [x], [row_spec]
    if combine:
        second = t // tm
        ins += [ys2, ys2, mods_prev]
        specs += [row_spec, pl.BlockSpec((tm, d), lambda i: (i + second, 0)), mod_spec]
    ins += [mods_cur, gain.reshape(1, d)]
    specs += [mod_spec, g_spec]
    if final:
        out_shape = jax.ShapeDtypeStruct((n_lat, d), F32)
        out_specs = row_spec
    else:
        out_shape = (jax.ShapeDtypeStruct((t, d), F32), jax.ShapeDtypeStruct((t, d), BF16))
        out_specs = (row_spec, row_spec)
    return pl.pallas_call(
        functools.partial(_prenorm_kernel, combine=combine, final=final, d=d),
        out_shape=out_shape, grid=(nblk,), in_specs=specs, out_specs=out_specs,
        compiler_params=_cparams(("arbitrary",), 40),
        name="prenorm_final" if final else ("prenorm_combine" if combine else "prenorm"),
    )(*ins)


def _mm_kernel(h_ref, w_ref, o_ref):
    o_ref[...] = jnp.dot(h_ref[...], w_ref[...], preferred_element_type=F32).astype(o_ref.dtype)


def _matmul_bf16(h, w):
    t, k = h.shape
    n = w.shape[1]
    tm, tn = MM_ROW_BLOCK, MM_COL_BLOCK
    return pl.pallas_call(
        _mm_kernel,
        out_shape=jax.ShapeDtypeStruct((t, n), BF16),
        grid=(t // tm, n // tn),
        in_specs=[pl.BlockSpec((tm, k), lambda i, j: (i, 0)),
                  pl.BlockSpec((k, tn), lambda i, j: (0, j))],
        out_specs=pl.BlockSpec((tm, tn), lambda i, j: (i, j)),
        compiler_params=_cparams(("arbitrary", "arbitrary"), 40),
        name="in_proj",
    )(h, w)


def _s5_weights(a_re, a_im, log_dt, b_re, b_im, c_re, c_im):
    ns = S5_STEP
    a_re, a_im = a_re.astype(F32), a_im.astype(F32)
    dt = jnp.exp(log_dt.astype(F32))[..., None]
    lam_re, lam_im = dt * a_re, dt * a_im
    tau = jnp.arange(ns + 1, dtype=F32)[:, None, None, None]
    mag = jnp.exp(tau * lam_re[None])
    p_re, p_im = mag * jnp.cos(tau * lam_im[None]), mag * jnp.sin(tau * lam_im[None])
    x, y = p_re[1] - 1.0, p_im[1]
    den = a_re * a_re + a_im * a_im
    k_re, k_im = (x * a_re + y * a_im) / den, (y * a_re - x * a_im) / den
    b_re, b_im = b_re.astype(F32), b_im.astype(F32)
    bb_re = k_re[..., None] * b_re - k_im[..., None] * b_im
    bb_im = k_re[..., None] * b_im + k_im[..., None] * b_re
    c_re, c_im = c_re.astype(F32), c_im.astype(F32)
    cp_re = c_re[None] * p_re[:, :, :, None, :] - c_im[None] * p_im[:, :, :, None, :]
    cp_im = c_re[None] * p_im[:, :, :, None, :] + c_im[None] * p_re[:, :, :, None, :]
    kmat = (jnp.einsum('tdgon,dgni->tdgoi', cp_re, bb_re, precision=HIGHEST)
            - jnp.einsum('tdgon,dgni->tdgoi', cp_im, bb_im, precision=HIGHEST))
    s_idx = jnp.arange(ns)[:, None]
    t_idx = jnp.arange(ns)[None, :]
    lag = t_idx - s_idx
    kf = jnp.where((lag >= 0)[:, :, None, None, None], kmat[jnp.clip(lag, 0, ns - 1), 0], 0.0)
    kb = jnp.where((lag <= 0)[:, :, None, None, None], kmat[jnp.clip(-lag, 0, ns - 1), 1], 0.0)
    g = a_re.shape[1]
    m = jnp.transpose(kf + kb, (2, 0, 4, 1, 3)).reshape(g, ns * S5_GROUP, ns * S5_GROUP)

    def inject(p_r, p_i, d):
        r = p_r[:, :, :, None] * bb_re[d][None] - p_i[:, :, :, None] * bb_im[d][None]
        i = p_r[:, :, :, None] * bb_im[d][None] + p_i[:, :, :, None] * bb_re[d][None]
        tr = lambda z: jnp.transpose(z, (1, 0, 3, 2)).reshape(g, ns * S5_GROUP, S5_STATE)
        return tr(r), tr(i)

    rev = jnp.arange(ns - 1, -1, -1)
    pf_re, pf_im = inject(p_re[rev, 0], p_im[rev, 0], 0)
    pb_re, pb_im = inject(p_re[:ns, 1], p_im[:ns, 1], 1)
    w1 = jnp.concatenate([m, pf_re, pb_re, pf_im, pb_im], axis=-1)

    def carry(cp, taus, d):
        return jnp.transpose(cp[taus, d], (1, 3, 0, 2)).reshape(g, S5_STATE, ns * S5_GROUP)

    up = jnp.arange(1, ns + 1)
    down = jnp.arange(ns, 0, -1)
    q = jnp.concatenate([carry(cp_re, up, 0), carry(cp_re, down, 1),
                         -carry(cp_im, up, 0), -carry(cp_im, down, 1)], axis=1)
    a16 = jnp.stack([jnp.concatenate([p_re[ns, 0], p_re[ns, 1]], axis=-1),
                     jnp.concatenate([p_im[ns, 0], p_im[ns, 1]], axis=-1)], axis=1)
    return w1.astype(BF16), q.astype(BF16), a16


def _s5_kernel(u_ref, w1_ref, q_ref, a_ref, o_ref, inj_ref, hf_ref, hb_ref, *, gb, lat_tiles, ctx_tiles):
    width = S5_STEP * S5_GROUP
    for g in range(gb):
        z = jnp.dot(u_ref[g], w1_ref[g], preferred_element_type=F32)
        o_ref[g] = z[:, :width]
        inj_ref[g] = z[:, width:]

    lane = lax.broadcasted_iota(jnp.int32, (8, 128), 1)
    row = lax.broadcasted_iota(jnp.int32, (8, 128), 0)
    fwd_lane = lane < S5_STATE
    first = fwd_lane == (row < 4)
    a_re = [jnp.broadcast_to(a_ref[g, 0:1, :], (8, 128)) for g in range(gb)]
    a_im = [jnp.broadcast_to(a_ref[g, 1:2, :], (8, 128)) for g in range(gb)]

    def make_body(base, ntiles):
        def body(it, carry):
            rf = pl.multiple_of(base + it * 8, 8)
            rb = pl.multiple_of(base + (ntiles - 1 - it) * 8, 8)
            new = []
            for g in range(gb):
                h_re, h_im = carry[2 * g], carry[2 * g + 1]
                t_re = jnp.where(fwd_lane, inj_ref[g, pl.ds(rf, 8), 0:128], inj_ref[g, pl.ds(rb, 8), 0:128])
                t_im = jnp.where(fwd_lane, inj_ref[g, pl.ds(rf, 8), 128:256], inj_ref[g, pl.ds(rb, 8), 128:256])
                s_re = pltpu.roll(t_re, 4, 0)
                s_im = pltpu.roll(t_im, 4, 0)
                i1_re, i2_re = jnp.where(first, t_re, s_re), jnp.where(first, s_re, t_re)
                i1_im, i2_im = jnp.where(first, t_im, s_im), jnp.where(first, s_im, t_im)
                h1_re = a_re[g] * h_re - a_im[g] * h_im + i1_re
                h1_im = a_re[g] * h_im + a_im[g] * h_re + i1_im
                h2_re = a_re[g] * h1_re - a_im[g] * h1_im + i2_re
                h2_im = a_re[g] * h1_im + a_im[g] * h1_re + i2_im
                e_re = jnp.where(first, h_re, h1_re)
                e_im = jnp.where(first, h_im, h1_im)
                hf_ref[g, pl.ds(rf, 8), 0:128] = e_re
                hf_ref[g, pl.ds(rf, 8), 128:256] = e_im
                hb_ref[g, pl.ds(rb, 8), 0:128] = e_re
                hb_ref[g, pl.ds(rb, 8), 128:256] = e_im
                new += [h2_re, h2_im]
            return tuple(new)
        return body

    carry = tuple(jnp.zeros((8, 128), F32) for _ in range(2 * gb))
    carry = lax.fori_loop(0, ctx_tiles, make_body(lat_tiles * 8, ctx_tiles), carry)
    carry = lax.fori_loop(0, lat_tiles, make_body(0, lat_tiles), carry)

    rows = u_ref.shape[1]
    lane2 = lax.broadcasted_iota(jnp.int32, (rows, width), 1)
    fwd2 = (lane2 % 128) < S5_STATE
    for g in range(gb):
        h = jnp.where(fwd2, hf_ref[g], hb_ref[g]).astype(BF16)
        o_ref[g] += jnp.dot(h, q_ref[g], preferred_element_type=F32)


def _s5_scan(u, w1, q, a16, *, lat_rows, ctx_rows):
    g, rows, width = u.shape
    gb = S5_GROUPS_PER_STEP
    kern = functools.partial(_s5_kernel, gb=gb, lat_tiles=lat_rows // 8, ctx_tiles=ctx_rows // 8)
    return pl.pallas_call(
        kern,
        out_shape=jax.ShapeDtypeStruct((g, rows, width), F32),
        grid=(g // gb,),
        in_specs=[pl.BlockSpec((gb, rows, width), lambda i: (i, 0, 0)),
                  pl.BlockSpec((gb, width, 2 * width), lambda i: (i, 0, 0)),
                  pl.BlockSpec((gb, width, width), lambda i: (i, 0, 0)),
                  pl.BlockSpec((gb, 2, 128), lambda i: (i, 0, 0))],
        out_specs=pl.BlockSpec((gb, rows, width), lambda i: (i, 0, 0)),
        scratch_shapes=[pltpu.VMEM((gb, rows, width), F32),
                        pltpu.VMEM((gb, rows, width), F32),
                        pltpu.VMEM((gb, rows, width), F32)],
        compiler_params=_cparams(("arbitrary",), 48),
        name="s5_scan",
    )(u, w1, q, a16)


def _ret_kernel(lg_ref, ql_ref, kl_ref, vl_ref, gl_ref, qc_ref, kc_ref, vc_ref, gc_ref, cos_ref, sin_ref,
                ol_ref, oc_ref, qs, ks, vs, acc, sf, sb, *, n_ctx, n_lat):
    c = RET_CHUNK
    half = RET_HEAD_DIM // 2
    head = pl.program_id(1)
    lgf = lg_ref[0, head]
    lgb = lg_ref[1, head]
    qscale = RET_HEAD_DIM ** -0.5
    ctx_rows = n_ctx * c

    ii = lax.broadcasted_iota(jnp.int32, (c, c), 0)
    jj = lax.broadcasted_iota(jnp.int32, (c, c), 1)
    rel = (ii - jj).astype(F32)
    dsum = (jnp.where(rel >= 0, jnp.exp(lgf * jnp.maximum(rel, 0.0)), 0.0)
            + jnp.where(rel <= 0, jnp.exp(lgb * jnp.maximum(-rel, 0.0)), 0.0))
    pos = lax.broadcasted_iota(jnp.int32, (c, 1), 0).astype(F32)
    qdec_f = jnp.exp(lgf * (pos + 1.0))
    kdec_f = jnp.exp(lgf * (c - 1.0 - pos))
    qdec_b = jnp.exp(lgb * (c - pos))
    kdec_b = jnp.exp(lgb * pos)
    cg_f = jnp.exp(lgf * c)
    cg_b = jnp.exp(lgb * c)

    qs[0:ctx_rows, :] = (qc_ref[...].astype(F32) * qscale).astype(BF16)
    ks[0:ctx_rows, :] = kc_ref[...]
    vs[0:ctx_rows, :] = vc_ref[...]

    def rope_body(n, _):
        r = pl.multiple_of(n * c, c)
        cs = cos_ref[pl.ds(r, c), :]
        sn = sin_ref[pl.ds(r, c), :]

        def rot(t):
            t1, t2 = t[:, :half], t[:, half:]
            return jnp.concatenate([t1 * cs - t2 * sn, t2 * cs + t1 * sn], axis=1)

        dst = pl.multiple_of(ctx_rows + r, c)
        qs[pl.ds(dst, c), :] = (rot(ql_ref[pl.ds(r, c), :].astype(F32)) * qscale).astype(BF16)
        ks[pl.ds(dst, c), :] = rot(kl_ref[pl.ds(r, c), :].astype(F32)).astype(BF16)
        vs[pl.ds(dst, c), :] = vl_ref[pl.ds(r, c), :]
        return 0

    lax.fori_loop(0, n_lat, rope_body, 0)

    sf[...] = jnp.zeros_like(sf)
    sb[...] = jnp.zeros_like(sb)

    def fwd_body(n, _):
        r = pl.multiple_of(n * c, c)
        q = qs[pl.ds(r, c), :]
        k = ks[pl.ds(r, c), :]
        v = vs[pl.ds(r, c), :]
        s = lax.dot_general(q, k, (((1,), (1,)), ((), ())), preferred_element_type=F32)
        intra = jnp.dot((s * dsum).astype(BF16), v, preferred_element_type=F32)
        inter = jnp.dot(q, sf[...].astype(BF16), preferred_element_type=F32) * qdec_f
        acc[pl.ds(r, c), :] = intra + inter
        kd = (k.astype(F32) * kdec_f).astype(BF16)
        sf[...] = cg_f * sf[...] + lax.dot_general(kd, v, (((0,), (0,)), ((), ())), preferred_element_type=F32)
        return 0

    lax.fori_loop(0, n_ctx + n_lat, fwd_body, 0)

    def bwd_chunk(r, gate, out_ref, ro):
        q = qs[pl.ds(r, c), :]
        k = ks[pl.ds(r, c), :]
        v = vs[pl.ds(r, c), :]
        inter = jnp.dot(q, sb[...].astype(BF16), preferred_element_type=F32) * qdec_b
        o = acc[pl.ds(r, c), :] + inter
        o = o * lax.rsqrt(jnp.mean(o * o, axis=-1, keepdims=True) + NORM_EPS)
        out_ref[pl.ds(ro, c), :] = (o * (gate * jax.nn.sigmoid(gate))).astype(out_ref.dtype)
        kd = (k.astype(F32) * kdec_b).astype(BF16)
        sb[...] = cg_b * sb[...] + lax.dot_general(kd, v, (((0,), (0,)), ((), ())), preferred_element_type=F32)

    for n in range(n_ctx - 1, -1, -1):
        bwd_chunk(n * c, gc_ref[n * c:(n + 1) * c, :].astype(F32), oc_ref, n * c)

    def bwd_body(m, _):
        ro = pl.multiple_of((n_lat - 1 - m) * c, c)
        r = pl.multiple_of(ctx_rows + ro, c)
        bwd_chunk(r, gl_ref[pl.ds(ro, c), :].astype(F32), ol_ref, ro)
        return 0

    lax.fori_loop(0, n_lat, bwd_body, 0)


def _retention(p, log_gamma, cos, sin, *, bsz, n_lat, n_ctx, col0):
    t = p.shape[0]
    dh = RET_HEAD_DIM
    heads = (p.shape[1] - col0) // (4 * dh)
    qb, kb, vb, gb = (col0 // dh + i * heads for i in range(4))
    ctx_blk0 = bsz * n_lat // n_ctx

    def lat(cb):
        return pl.BlockSpec((n_lat, dh), lambda b, h: (b, cb + h))

    def ctx(cb):
        return pl.BlockSpec((n_ctx, dh), lambda b, h: (ctx_blk0 + b, cb + h))

    tab = pl.BlockSpec((n_lat, dh // 2), lambda b, h: (0, 0))
    kern = functools.partial(_ret_kernel, n_ctx=n_ctx // RET_CHUNK, n_lat=n_lat // RET_CHUNK)
    seq = n_ctx + n_lat
    return pl.pallas_call(
        kern,
        out_shape=(jax.ShapeDtypeStruct((bsz * n_lat, heads * dh), BF16),
                   jax.ShapeDtypeStruct((bsz * n_ctx, heads * dh), BF16)),
        grid=(bsz, heads),
        in_specs=[pl.BlockSpec(memory_space=pltpu.SMEM),
                  lat(qb), lat(kb), lat(vb), lat(gb), ctx(qb), ctx(kb), ctx(vb), ctx(gb), tab, tab],
        out_specs=(pl.BlockSpec((n_lat, dh), lambda b, h: (b, h)),
                   pl.BlockSpec((n_ctx, dh), lambda b, h: (b, h))),
        scratch_shapes=[pltpu.VMEM((seq, dh), BF16), pltpu.VMEM((seq, dh), BF16), pltpu.VMEM((seq, dh), BF16),
                        pltpu.VMEM((seq, dh), F32), pltpu.VMEM((dh, dh), F32), pltpu.VMEM((dh, dh), F32)],
        compiler_params=_cparams(("arbitrary", "arbitrary"), 52),
        name="retention",
    )(log_gamma, p, p, p, p, p, p, p, p, cos, sin)


def _top2_sum(a, b, c, d):
    hi1, lo1 = jnp.maximum(a, b), jnp.minimum(a, b)
    hi2, lo2 = jnp.maximum(c, d), jnp.minimum(c, d)
    return jnp.maximum(hi1, hi2) + jnp.maximum(jnp.minimum(hi1, hi2), jnp.maximum(lo1, lo2))


def _route(logits_t, rb):
    sc = jax.nn.sigmoid(logits_t)
    biased = sc + rb
    per = EXPERTS_PER_GROUP
    brow = [biased[e:e + 1, :] for e in range(N_EXPERTS)]
    srow = [sc[e:e + 1, :] for e in range(N_EXPERTS)]
    gscore = [_top2_sum(*brow[per * g:per * g + per]) for g in range(N_EXPERT_GROUPS)]
    best_v = gscore[0]
    best_g = jnp.zeros_like(best_v, dtype=jnp.int32)
    for g in range(1, N_EXPERT_GROUPS):
        upd = gscore[g] > best_v
        best_v = jnp.where(upd, gscore[g], best_v)
        best_g = jnp.where(upd, g, best_g)
    vals, sels = [], []
    for j in range(per):
        v = brow[j]
        s = srow[j]
        for g in range(1, N_EXPERT_GROUPS):
            v = jnp.where(best_g == g, brow[per * g + j], v)
            s = jnp.where(best_g == g, srow[per * g + j], s)
        vals.append(v)
        sels.append(s)
    v1, i1, s1 = vals[0], jnp.zeros_like(best_g), sels[0]
    for j in range(1, per):
        upd = vals[j] > v1
        v1 = jnp.where(upd, vals[j], v1)
        s1 = jnp.where(upd, sels[j], s1)
        i1 = jnp.where(upd, j, i1)
    v2 = jnp.full_like(v1, -jnp.inf)
    i2, s2 = jnp.zeros_like(best_g), jnp.zeros_like(s1)
    for j in range(per):
        upd = (i1 != j) & (vals[j] > v2)
        v2 = jnp.where(upd, vals[j], v2)
        s2 = jnp.where(upd, sels[j], s2)
        i2 = jnp.where(upd, j, i2)
    tot = s1 + s2
    idx = jnp.concatenate([best_g * per + i1, best_g * per + i2], axis=0)
    wts = jnp.concatenate([s1 / tot, s2 / tot], axis=0)
    return idx, wts


def _mix_kernel(ys_ref, u_ref, rl_ref, rc_ref, x_ref, m_ref, dsk_ref, gw_ref, gb_ref, ws_ref, wr_ref, g2_ref,
                rw_ref, rb_ref, xo_ref, f_ref, ei_ref, ew_ref, *, lat_blocks, d):
    i = pl.program_id(0)
    y = ys_ref[...] + dsk_ref[...] * u_ref[...].astype(F32)
    y = jax.nn.gelu(y, approximate=True)
    z = jnp.dot(y.astype(BF16), gw_ref[...], preferred_element_type=F32) + gb_ref[...]
    s5o = (y * jax.nn.sigmoid(z)).astype(BF16)
    r = jnp.where(i < lat_blocks, rl_ref[...], rc_ref[...])
    mixed = (jnp.dot(s5o, ws_ref[...], preferred_element_type=F32)
             + jnp.dot(r, wr_ref[...], preferred_element_type=F32))
    gate1 = m_ref[:, 2 * d:3 * d]
    x = x_ref[...] + gate1 * mixed
    xo_ref[...] = x
    ms = jnp.mean(x * x, axis=-1, keepdims=True)
    f = x * lax.rsqrt(ms + NORM_EPS) * g2_ref[...]
    f = f * (1.0 + m_ref[:, 4 * d:5 * d]) + m_ref[:, 3 * d:4 * d]
    f_ref[...] = f
    logits_t = lax.dot_general(rw_ref[...], f, (((1,), (1,)), ((), ())), precision=HIGHEST,
                               preferred_element_type=F32)
    idx, wts = _route(logits_t, rb_ref[...])
    ei_ref[...] = idx
    ew_ref[...] = wts


def _mix(ys, p, rl, rc, x, mods_cur, d_skip, glu_w, glu_b, w_s, w_r, g2, rw_t, rb, *, n_lat, rows_per_batch):
    t, d = x.shape
    sw = ys.shape[1]
    tm = ROW_BLOCK
    lat_blocks = n_lat // tm
    per_b = rows_per_batch // tm
    ctx_row = n_lat // rows_per_batch
    const = lambda i: (0, 0)
    row = lambda i: (i, 0)
    kern = functools.partial(_mix_kernel, lat_blocks=lat_blocks, d=d)
    return pl.pallas_call(
        kern,
        out_shape=(jax.ShapeDtypeStruct((t, d), F32), jax.ShapeDtypeStruct((t, d), F32),
                   jax.ShapeDtypeStruct((TOP_K, t), jnp.int32), jax.ShapeDtypeStruct((TOP_K, t), F32)),
        grid=(t // tm,),
        in_specs=[pl.BlockSpec((tm, sw), row),
                  pl.BlockSpec((tm, sw), row),
                  pl.BlockSpec((tm, sw), lambda i: (jnp.minimum(i, lat_blocks - 1), 0)),
                  pl.BlockSpec((tm, sw), lambda i: (jnp.maximum(i - lat_blocks, 0), 0)),
                  pl.BlockSpec((tm, d), row),
                  pl.BlockSpec((None, 1, 6 * d), lambda i: (jnp.where(i < lat_blocks, i // per_b, ctx_row), 0, 0)),
                  pl.BlockSpec((1, sw), const),
                  pl.BlockSpec((sw, sw), const),
                  pl.BlockSpec((1, sw), const),
                  pl.BlockSpec((sw, d), const),
                  pl.BlockSpec((d - sw, d), const),
                  pl.BlockSpec((1, d), const),
                  pl.BlockSpec((N_EXPERTS, d), const),
                  pl.BlockSpec((N_EXPERTS, 1), const)],
        out_specs=(pl.BlockSpec((tm, d), row), pl.BlockSpec((tm, d), row),
                   pl.BlockSpec((TOP_K, tm), lambda i: (0, i)), pl.BlockSpec((TOP_K, tm), lambda i: (0, i))),
        compiler_params=_cparams(("arbitrary",), 52),
        name="mix",
    )(ys, p, rl, rc, x, mods_cur, d_skip, glu_w, glu_b, w_s, w_r, g2, rw_t, rb)


def _expert_kernel(bexp_ref, nused_ref, src0_ref, srcn_ref, dst_ref, f_hbm, sw_ref, wg_ref, wu_ref, wd_ref,
                   y_hbm, xbuf, ybuf, sem_in, sem_out, pending, *, bm, nblk, n_real):
    j = pl.program_id(0)
    slot = j % 2
    nused = nused_ref[0]
    unroll = 8

    def gather(idx_ref, s):
        def body(r8, _):
            for k in range(unroll):
                r = r8 * unroll + k
                pltpu.make_async_copy(f_hbm.at[pl.ds(idx_ref[0, r], 1)], xbuf.at[s, pl.ds(r, 1)],
                                      sem_in.at[s]).start()
            return 0
        lax.fori_loop(0, bm // unroll, body, 0)

    def wait_in(s):
        pltpu.make_async_copy(f_hbm.at[pl.ds(0, bm)], xbuf.at[s], sem_in.at[s]).wait()

    def wait_out(s):
        pltpu.make_async_copy(ybuf.at[s], y_hbm.at[pl.ds(0, bm)], sem_out.at[s]).wait()

    @pl.when(j == 0)
    def _():
        pending[0] = 0
        pending[1] = 0
        ybuf[1] = jnp.zeros(ybuf.shape[1:], ybuf.dtype)
        for half in range(2):
            pltpu.make_async_copy(ybuf.at[1], y_hbm.at[pl.ds(n_real + half * bm, bm)], sem_out.at[1]).start()
        for half in range(2):
            wait_out(1)
        gather(src0_ref, 0)

    @pl.when(j + 1 < nused)
    def _():
        gather(srcn_ref, 1 - slot)

    @pl.when(j < nused)
    def _():
        wait_in(slot)
        xb = xbuf[slot].astype(BF16)
        gt = jnp.dot(xb, wg_ref[...], preferred_element_type=F32)
        up = jnp.dot(xb, wu_ref[...], preferred_element_type=F32)
        hmid = (gt * jax.nn.sigmoid(gt) * up).astype(BF16)
        y = jnp.dot(hmid, wd_ref[...], preferred_element_type=F32) * sw_ref[...]

        @pl.when(pending[slot] == 1)
        def _():
            wait_out(slot)

        ybuf[slot] = y

        def sbody(r8, _):
            for k in range(unroll):
                r = r8 * unroll + k
                pltpu.make_async_copy(ybuf.at[slot, pl.ds(r, 1)], y_hbm.at[pl.ds(dst_ref[0, r], 1)],
                                      sem_out.at[slot]).start()
            return 0
        lax.fori_loop(0, bm // unroll, sbody, 0)
        pending[slot] = 1

    @pl.when(j == nblk - 1)
    def _():
        for s in range(2):
            @pl.when(pending[s] == 1)
            def _():
                wait_out(s)
            pending[s] = 0


def _experts(f, bexp, nused, src, dst, sw, wg, wu, wd, *, out_rows):
    t, d = f.shape
    bm = MOE_ROWS
    nblk = bexp.shape[0]
    ff = wg.shape[2]
    kern = functools.partial(_expert_kernel, bm=bm, nblk=nblk, n_real=out_rows - 2 * bm)
    smem_blk = lambda fn: pl.BlockSpec((None, 1, bm), fn, memory_space=pltpu.SMEM)
    grid_spec = pltpu.PrefetchScalarGridSpec(
        num_scalar_prefetch=2,
        grid=(nblk,),
        in_specs=[smem_blk(lambda j, be, nu: (0, 0, 0)),
                  smem_blk(lambda j, be, nu: (jnp.minimum(j + 1, nblk - 1), 0, 0)),
                  smem_blk(lambda j, be, nu: (j, 0, 0)),
                  pl.BlockSpec(memory_space=pl.ANY),
                  pl.BlockSpec((bm, 1), lambda j, be, nu: (j, 0)),
                  pl.BlockSpec((None, d, ff), lambda j, be, nu: (be[j], 0, 0)),
                  pl.BlockSpec((None, d, ff), lambda j, be, nu: (be[j], 0, 0)),
                  pl.BlockSpec((None, ff, d), lambda j, be, nu: (be[j], 0, 0))],
        out_specs=pl.BlockSpec(memory_space=pl.ANY),
        scratch_shapes=[pltpu.VMEM((2, bm, d), F32), pltpu.VMEM((2, bm, d), F32),
                        pltpu.SemaphoreType.DMA((2,)), pltpu.SemaphoreType.DMA((2,)),
                        pltpu.SMEM((2,), jnp.int32)],
    )
    src3 = src.reshape(nblk, 1, bm)
    return pl.pallas_call(
        kern,
        out_shape=jax.ShapeDtypeStruct((out_rows, d), F32),
        grid_spec=grid_spec,
        compiler_params=_cparams(("arbitrary",), 52),
        name="experts",
    )(bexp, nused, src3, src3, dst.reshape(nblk, 1, bm), f, sw, wg, wu, wd)


def _dispatch_plan(eidx, ew, n_tok):
    bm = MOE_ROWS
    n_assign = TOP_K * n_tok
    nblk = -(-n_assign // bm) + N_EXPERTS
    flat_e = eidx.reshape(-1)
    onehot = (flat_e[:, None] == jnp.arange(N_EXPERTS, dtype=jnp.int32)[None, :]).astype(jnp.int32)
    csum = jnp.cumsum(onehot, axis=0)
    counts = csum[-1]
    rank = jnp.take_along_axis(csum, flat_e[:, None], axis=1)[:, 0] - 1
    padded = (counts + bm - 1) // bm * bm
    pend = jnp.cumsum(padded)
    pstart = pend - padded
    dest = pstart[flat_e] + rank
    a_ids = jnp.arange(n_assign, dtype=jnp.int32)
    slots = nblk * bm
    src = jnp.zeros((slots,), jnp.int32).at[dest].set(a_ids % n_tok)
    trash = n_assign + ((jnp.arange(slots, dtype=jnp.int32) // bm) % 2) * bm + jnp.arange(slots, dtype=jnp.int32) % bm
    dst = trash.at[dest].set(a_ids)
    sw = jnp.zeros((slots,), F32).at[dest].set(ew.reshape(-1)).reshape(slots, 1)
    blk_start = jnp.arange(nblk, dtype=jnp.int32) * bm
    bexp = jnp.minimum(jnp.sum((pend[None, :] <= blk_start[:, None]).astype(jnp.int32), axis=1), N_EXPERTS - 1)
    nused = (pend[-1] // bm).astype(jnp.int32).reshape(1)
    return bexp, nused, src, dst, sw, n_assign + 2 * bm


def _to_s5_layout(pu, bsz, n_lat, n_ctx):
    g = pu.shape[1] // S5_GROUP

    def one(z, length):
        z = z.reshape(bsz, length // S5_STEP, S5_STEP, g, S5_GROUP)
        return jnp.transpose(z, (3, 1, 0, 2, 4)).reshape(g, (length // S5_STEP) * bsz, S5_STEP * S5_GROUP)

    return jnp.concatenate([one(pu[:bsz * n_lat], n_lat), one(pu[bsz * n_lat:], n_ctx)], axis=1)


def _from_s5_layout(y, bsz, n_lat, n_ctx):
    g = y.shape[0]

    def one(z, length):
        z = z.reshape(g, length // S5_STEP, bsz, S5_STEP, S5_GROUP)
        return jnp.transpose(z, (2, 1, 3, 0, 4)).reshape(bsz * length, g * S5_GROUP)

    lat_rows = (n_lat // S5_STEP) * bsz
    return jnp.concatenate([one(y[:, :lat_rows], n_lat), one(y[:, lat_rows:], n_ctx)], axis=0)


def _rope_tables(n_lat):
    n_rows = n_lat // GRID_W
    row = jnp.broadcast_to(jnp.arange(n_rows, dtype=F32)[:, None], (n_rows, GRID_W)).reshape(-1)
    col = jnp.broadcast_to(jnp.arange(GRID_W, dtype=F32)[None, :], (n_rows, GRID_W)).reshape(-1)
    quarter = RET_HEAD_DIM // 4
    freqs = ROPE_BASE ** (-jnp.arange(quarter, dtype=F32) / quarter)
    ang = jnp.concatenate([row[:, None] * freqs, col[:, None] * freqs], axis=-1)
    return jnp.cos(ang), jnp.sin(ang)


def kernel(x, c, ctx, c_ctx, mod_w, mod_b, norm1_g, norm2_g, w_in, s5_a_re, s5_a_im, s5_log_dt, s5_b_re, s5_b_im,
           s5_c_re, s5_c_im, s5_d, glu_w, glu_b, ret_decay_raw, w_out, router_w, router_b, exp_w_gate, exp_w_up,
           exp_w_down, final_norm_g):
    bsz, n_lat, d = x.shape
    n_ctx = ctx.shape[1]
    depth = mod_w.shape[0]
    s5w = s5_d.shape[1]
    t = bsz * (n_lat + n_ctx)
    lat_total = bsz * n_lat

    xs = jnp.concatenate([x.reshape(lat_total, d), ctx.reshape(bsz * n_ctx, d)], axis=0)
    c8 = jnp.zeros((8, d), F32).at[:bsz].set(c).at[bsz].set(c_ctx)
    mods = _mods(c8, mod_w, mod_b).reshape(depth, 8, 1, 6 * d)
    cos, sin = _rope_tables(n_lat)
    rw_t = router_w.T
    rb = router_b.reshape(N_EXPERTS, 1)

    ys2 = None
    for layer in range(depth):
        xs, h = _prenorm(xs, ys2, mods[layer - 1] if layer else None, mods[layer], norm1_g[layer],
                         n_lat=lat_total, rows_per_batch=n_lat, final=False)
        p = _matmul_bf16(h, w_in[layer].astype(BF16))

        w1, q, a16 = _s5_weights(s5_a_re[layer], s5_a_im[layer], s5_log_dt[layer], s5_b_re[layer],
                                 s5_b_im[layer], s5_c_re[layer], s5_c_im[layer])
        u = _to_s5_layout(p[:, :s5w], bsz, n_lat, n_ctx)
        yc = _s5_scan(u, w1, q, a16, lat_rows=(n_lat // S5_STEP) * bsz, ctx_rows=(n_ctx // S5_STEP) * bsz)
        ys = _from_s5_layout(yc, bsz, n_lat, n_ctx)

        log_gamma = jnp.log1p(-jnp.exp2(-ret_decay_raw[layer].astype(F32)))
        rl, rc = _retention(p, log_gamma, cos, sin, bsz=bsz, n_lat=n_lat, n_ctx=n_ctx, col0=s5w)

        w_o = w_out[layer].astype(BF16)
        xs, f, eidx, ew = _mix(ys, p, rl, rc, xs, mods[layer], s5_d[layer].reshape(1, s5w),
                               glu_w[layer].astype(BF16), glu_b[layer].reshape(1, s5w), w_o[:s5w], w_o[s5w:],
                               norm2_g[layer].reshape(1, d), rw_t, rb, n_lat=lat_total, rows_per_batch=n_lat)

        bexp, nused, src, dst, sw, out_rows = _dispatch_plan(eidx, ew, t)
        ys2 = _experts(f, bexp, nused, src, dst, sw, exp_w_gate[layer].astype(BF16),
                       exp_w_up[layer].astype(BF16), exp_w_down[layer].astype(BF16), out_rows=out_rows)

    out = _prenorm(xs, ys2, mods[depth - 1], mods[depth - 1], final_norm_g,
                   n_lat=lat_total, rows_per_batch=n_lat, final=True)
    return out.reshape(bsz, n_lat, d)
```

```python
import functools
import math

import jax
import jax.numpy as jnp
from jax import lax
from jax.experimental import pallas as pl
from jax.experimental.pallas import tpu as pltpu

F32 = jnp.float32
BF16 = jnp.bfloat16
HIGHEST = lax.Precision.HIGHEST

GRID_W = 64
S5_GROUP = 16
S5_STATE = 64
S5_STEP = 16
RET_HEAD_DIM = 256
RET_BLOCK = 256
ROPE_BASE = 10000.0
N_EXPERTS = 16
N_EXPERT_GROUPS = 4
EXPERTS_PER_GROUP = 4
TOP_K = 2
NORM_EPS = 1e-6

ROW_BLOCK = 256
MIX_ROW_BLOCK = 512
MM_ROW_BLOCK = 1024
MM_COL_BLOCK = 1024
MOE_ROWS = 256
S5_TILE_GROUPS = 128 // S5_GROUP
MIB = 1024 * 1024


def _cparams(sem, vmem_mib):
    return pltpu.CompilerParams(dimension_semantics=sem, vmem_limit_bytes=int(vmem_mib * MIB))


def _mods_kernel(c_ref, w_ref, b_ref, o_ref):
    cv = c_ref[...]
    act = cv * jax.nn.sigmoid(cv)
    o_ref[0] = jnp.dot(act, w_ref[0], precision=HIGHEST, preferred_element_type=F32) + b_ref[0]


def _mods(c8, mod_w, mod_b):
    depth, d, n = mod_w.shape
    tn = 1024
    return pl.pallas_call(
        _mods_kernel,
        out_shape=jax.ShapeDtypeStruct((depth, 8, n), F32),
        grid=(depth, n // tn),
        in_specs=[pl.BlockSpec((8, d), lambda l, j: (0, 0)),
                  pl.BlockSpec((1, d, tn), lambda l, j: (l, 0, j)),
                  pl.BlockSpec((1, 1, tn), lambda l, j: (l, 0, j))],
        out_specs=pl.BlockSpec((1, 8, tn), lambda l, j: (l, 0, j)),
        compiler_params=_cparams(("arbitrary", "arbitrary"), 40),
        name="adaln_mods",
    )(c8, mod_w, mod_b.reshape(depth, 1, n))


def _prenorm_kernel(*refs, combine, final, d):
    if combine:
        x_ref, ya_ref, yb_ref, mprev_ref, mcur_ref, g_ref = refs[:6]
        outs = refs[6:]
    else:
        x_ref, mcur_ref, g_ref = refs[:3]
        outs = refs[3:]
    x = x_ref[...]
    if combine:
        gate2 = mprev_ref[:, 5 * d:6 * d]
        x = x + gate2 * (ya_ref[...] + yb_ref[...])
    ms = jnp.mean(x * x, axis=-1, keepdims=True)
    xn = x * lax.rsqrt(ms + NORM_EPS) * g_ref[...]
    if final:
        outs[0][...] = xn
    else:
        outs[0][...] = x
        shift = mcur_ref[:, 0:d]
        scale = mcur_ref[:, d:2 * d]
        outs[1][...] = (xn * (1.0 + scale) + shift).astype(BF16)


def _prenorm(x, ys2, mods_prev, mods_cur, gain, *, n_lat, rows_per_batch, final):
    t, d = x.shape
    tm = ROW_BLOCK
    combine = ys2 is not None
    nblk = (n_lat if final else t) // tm
    lat_blocks = n_lat // tm
    per_b = rows_per_batch // tm
    ctx_row = n_lat // rows_per_batch

    def mod_idx(i):
        return (jnp.where(i < lat_blocks, i // per_b, ctx_row), 0, 0)

    row_spec = pl.BlockSpec((tm, d), lambda i: (i, 0))
    mod_spec = pl.BlockSpec((None, 1, 6 * d), mod_idx)
    g_spec = pl.BlockSpec((1, d), lambda i: (0, 0))
    ins, specs = [x], [row_spec]
    if combine:
        second = t // tm
        ins += [ys2, ys2, mods_prev]
        specs += [row_spec, pl.BlockSpec((tm, d), lambda i: (i + second, 0)), mod_spec]
    ins += [mods_cur, gain.reshape(1, d)]
    specs += [mod_spec, g_spec]
    if final:
        out_shape = jax.ShapeDtypeStruct((n_lat, d), F32)
        out_specs = row_spec
    else:
        out_shape = (jax.ShapeDtypeStruct((t, d), F32), jax.ShapeDtypeStruct((t, d), BF16))
        out_specs = (row_spec, row_spec)
    return pl.pallas_call(
        functools.partial(_prenorm_kernel, combine=combine, final=final, d=d),
        out_shape=out_shape, grid=(nblk,), in_specs=specs, out_specs=out_specs,
        compiler_params=_cparams(("arbitrary",), 40),
        name="prenorm_final" if final else ("prenorm_combine" if combine else "prenorm"),
    )(*ins)


def _inproj_kernel(h_ref, w_ref, pu_ref, pr_ref):
    acc = jnp.dot(h_ref[...], w_ref[...], preferred_element_type=F32)

    @pl.when(pl.program_id(1) == 0)
    def _():
        pu_ref[...] = acc

    @pl.when(pl.program_id(1) > 0)
    def _():
        pr_ref[...] = acc.astype(pr_ref.dtype)


def _in_proj(h, w, s5w):
    t, k = h.shape
    n = w.shape[1]
    tm, tn = MM_ROW_BLOCK, MM_COL_BLOCK
    assert s5w == tn
    return pl.pallas_call(
        _inproj_kernel,
        out_shape=(jax.ShapeDtypeStruct((t, s5w), F32), jax.ShapeDtypeStruct((t, n - s5w), BF16)),
        grid=(t // tm, n // tn),
        in_specs=[pl.BlockSpec((tm, k), lambda i, j: (i, 0)),
                  pl.BlockSpec((k, tn), lambda i, j: (0, j))],
        out_specs=(pl.BlockSpec((tm, tn), lambda i, j: (i, 0)),
                   pl.BlockSpec((tm, tn), lambda i, j: (i, jnp.maximum(j - 1, 0)))),
        compiler_params=_cparams(("arbitrary", "arbitrary"), 40),
        name="in_proj",
    )(h, w)


def _s5_weights(a_re, a_im, log_dt, b_re, b_im, c_re, c_im):
    ns = S5_STEP
    a_re, a_im = a_re.astype(F32), a_im.astype(F32)
    dt = jnp.exp(log_dt.astype(F32))[..., None]
    lam_re, lam_im = dt * a_re, dt * a_im
    tau = jnp.arange(ns + 1, dtype=F32)[:, None, None, None]
    mag = jnp.exp(tau * lam_re[None])
    p_re, p_im = mag * jnp.cos(tau * lam_im[None]), mag * jnp.sin(tau * lam_im[None])
    x, y = p_re[1] - 1.0, p_im[1]
    den = a_re * a_re + a_im * a_im
    k_re, k_im = (x * a_re + y * a_im) / den, (y * a_re - x * a_im) / den
    b_re, b_im = b_re.astype(F32), b_im.astype(F32)
    bb_re = k_re[..., None] * b_re - k_im[..., None] * b_im
    bb_im = k_re[..., None] * b_im + k_im[..., None] * b_re
    c_re, c_im = c_re.astype(F32), c_im.astype(F32)
    cp_re = c_re[None] * p_re[:, :, :, None, :] - c_im[None] * p_im[:, :, :, None, :]
    cp_im = c_re[None] * p_im[:, :, :, None, :] + c_im[None] * p_re[:, :, :, None, :]
    kmat = (jnp.einsum('tdgon,dgni->tdgoi', cp_re, bb_re, precision=HIGHEST)
            - jnp.einsum('tdgon,dgni->tdgoi', cp_im, bb_im, precision=HIGHEST))
    s_idx = jnp.arange(ns)[:, None]
    t_idx = jnp.arange(ns)[None, :]
    lag = t_idx - s_idx
    kf = jnp.where((lag >= 0)[:, :, None, None, None], kmat[jnp.clip(lag, 0, ns - 1), 0], 0.0)
    kb = jnp.where((lag <= 0)[:, :, None, None, None], kmat[jnp.clip(-lag, 0, ns - 1), 1], 0.0)
    g = a_re.shape[1]
    m = jnp.transpose(kf + kb, (2, 0, 4, 1, 3)).reshape(g, ns * S5_GROUP, ns * S5_GROUP)

    def inject(p_r, p_i, d):
        r = p_r[:, :, :, None] * bb_re[d][None] - p_i[:, :, :, None] * bb_im[d][None]
        i = p_r[:, :, :, None] * bb_im[d][None] + p_i[:, :, :, None] * bb_re[d][None]
        tr = lambda z: jnp.transpose(z, (1, 0, 3, 2)).reshape(g, ns * S5_GROUP, S5_STATE)
        return tr(r), tr(i)

    rev = jnp.arange(ns - 1, -1, -1)
    pf_re, pf_im = inject(p_re[rev, 0], p_im[rev, 0], 0)
    pb_re, pb_im = inject(p_re[:ns, 1], p_im[:ns, 1], 1)
    w1 = jnp.concatenate([m, pf_re, pb_re, pf_im, pb_im], axis=-1)

    def carry(cp, taus, d):
        return jnp.transpose(cp[taus, d], (1, 3, 0, 2)).reshape(g, S5_STATE, ns * S5_GROUP)

    up = jnp.arange(1, ns + 1)
    down = jnp.arange(ns, 0, -1)
    q = jnp.concatenate([carry(cp_re, up, 0), carry(cp_re, down, 1),
                         -carry(cp_im, up, 0), -carry(cp_im, down, 1)], axis=1)
    a16 = jnp.stack([jnp.concatenate([p_re[ns, 0], p_re[ns, 1]], axis=-1),
                     jnp.concatenate([p_im[ns, 0], p_im[ns, 1]], axis=-1)], axis=1)
    return w1.astype(BF16), q.astype(BF16), a16


def _group_transpose(xs):
    ng = len(xs)
    grp = lax.broadcasted_iota(jnp.int32, xs[0].shape, 1) // S5_GROUP
    rolled = []
    for j in range(ng):
        y = xs[j % ng]
        for k in range(1, ng):
            y = jnp.where(grp == k, xs[(k + j) % ng], y)
        rolled.append(y if j == 0 else pltpu.roll(y, S5_GROUP * j, 1))
    outs = []
    for a in range(ng):
        o = rolled[(-a) % ng]
        for b in range(1, ng):
            o = jnp.where(grp == b, rolled[(b - a) % ng], o)
        outs.append(o)
    return outs


def _s5_kernel(ul_ref, uc_ref, w1_ref, q_ref, a_ref, yl_ref, yc_ref, uflat, st_re, st_im, yfl, *,
               lat_chunks, ctx_chunks):
    ng = S5_TILE_GROUPS
    width = S5_STEP * S5_GROUP
    n_chunks = lat_chunks + ctx_chunks
    sub = 32

    def flatten(src_ref, c0, row0, n):
        for hh in range(2):
            xs = [src_ref[pl.ds(row0 + hh * 8 + t, n, stride=S5_STEP), :] for t in range(8)]
            outs = _group_transpose(xs)
            for k in range(ng):
                uflat[k, pl.ds(c0, n), hh * 128:(hh + 1) * 128] = outs[k].astype(BF16)

    def flat_body(i, _):
        c0 = pl.multiple_of(i * sub, sub)
        flatten(ul_ref, c0, c0 * S5_STEP, sub)
        return 0

    lax.fori_loop(0, lat_chunks // sub, flat_body, 0)
    flatten(uc_ref, lat_chunks, 0, ctx_chunks)

    for k in range(ng):
        z = jnp.dot(uflat[k], w1_ref[k], preferred_element_type=F32)
        yfl[k] = z[:, :width]
        st_re[pl.ds(k, n_chunks, stride=ng), :] = z[:, width:width + 128]
        st_im[pl.ds(k, n_chunks, stride=ng), :] = z[:, width + 128:]

    lane = lax.broadcasted_iota(jnp.int32, (ng, 128), 1)
    fwd_lane = lane < S5_STATE
    a_re = a_ref[0]
    a_im = a_ref[1]

    def make_body(base, n):
        def body(i, carry):
            h_re, h_im = carry
            rf = pl.multiple_of((base + i) * ng, ng)
            rb = pl.multiple_of((base + n - 1 - i) * ng, ng)
            tf_re, tf_im = st_re[pl.ds(rf, ng), :], st_im[pl.ds(rf, ng), :]
            tb_re, tb_im = st_re[pl.ds(rb, ng), :], st_im[pl.ds(rb, ng), :]
            t_re = jnp.where(fwd_lane, tf_re, tb_re)
            t_im = jnp.where(fwd_lane, tf_im, tb_im)
            n_re = a_re * h_re - a_im * h_im + t_re
            n_im = a_re * h_im + a_im * h_re + t_im
            st_re[pl.ds(rf, ng), :] = jnp.where(fwd_lane, h_re, tf_re)
            st_im[pl.ds(rf, ng), :] = jnp.where(fwd_lane, h_im, tf_im)
            st_re[pl.ds(rb, ng), :] = jnp.where(fwd_lane, tb_re, h_re)
            st_im[pl.ds(rb, ng), :] = jnp.where(fwd_lane, tb_im, h_im)
            return n_re, n_im
        return body

    carry = (jnp.zeros((ng, 128), F32), jnp.zeros((ng, 128), F32))
    carry = lax.fori_loop(0, ctx_chunks, make_body(lat_chunks, ctx_chunks), carry)
    carry = lax.fori_loop(0, lat_chunks, make_body(0, lat_chunks), carry)

    for k in range(ng):
        hk = jnp.concatenate([st_re[pl.ds(k, n_chunks, stride=ng), :], st_im[pl.ds(k, n_chunks, stride=ng), :]],
                             axis=1).astype(BF16)
        yfl[k] = yfl[k] + jnp.dot(hk, q_ref[k], preferred_element_type=F32)

    def unflatten(dst_ref, c0, row0, n):
        for hh in range(2):
            vs = [yfl[k, pl.ds(c0, n), hh * 128:(hh + 1) * 128] for k in range(ng)]
            ts = _group_transpose(vs)
            for t in range(8):
                dst_ref[pl.ds(row0 + hh * 8 + t, n, stride=S5_STEP), :] = ts[t]

    def unflat_body(i, _):
        c0 = pl.multiple_of(i * sub, sub)
        unflatten(yl_ref, c0, c0 * S5_STEP, sub)
        return 0

    lax.fori_loop(0, lat_chunks // sub, unflat_body, 0)
    unflatten(yc_ref, lat_chunks, 0, ctx_chunks)


def _s5_scan(pu, w1, q, a16, *, bsz, n_lat, n_ctx):
    g = w1.shape[0]
    ng = S5_TILE_GROUPS
    width = S5_STEP * S5_GROUP
    lat_chunks, ctx_chunks = n_lat // S5_STEP, n_ctx // S5_STEP
    assert lat_chunks % 32 == 0 and ctx_chunks % 2 == 0 and ctx_chunks % 16 == 0
    n_chunks = lat_chunks + ctx_chunks
    ctx_blk0 = bsz * n_lat // n_ctx
    a_t = jnp.transpose(a16.reshape(g // ng, ng, 2, 128), (0, 2, 1, 3))
    kern = functools.partial(_s5_kernel, lat_chunks=lat_chunks, ctx_chunks=ctx_chunks)
    return pl.pallas_call(
        kern,
        out_shape=(jax.ShapeDtypeStruct((bsz * n_lat, g * S5_GROUP), F32),
                   jax.ShapeDtypeStruct((bsz * n_ctx, g * S5_GROUP), F32)),
        grid=(g // ng, bsz),
        in_specs=[pl.BlockSpec((n_lat, 128), lambda i, b: (b, i)),
                  pl.BlockSpec((n_ctx, 128), lambda i, b: (ctx_blk0 + b, i)),
                  pl.BlockSpec((ng, width, 2 * width), lambda i, b: (i, 0, 0)),
                  pl.BlockSpec((ng, width, width), lambda i, b: (i, 0, 0)),
                  pl.BlockSpec((None, 2, ng, 128), lambda i, b: (i, 0, 0, 0))],
        out_specs=(pl.BlockSpec((n_lat, 128), lambda i, b: (b, i)),
                   pl.BlockSpec((n_ctx, 128), lambda i, b: (b, i))),
        scratch_shapes=[pltpu.VMEM((ng, n_chunks, width), BF16),
                        pltpu.VMEM((n_chunks * ng, 128), F32),
                        pltpu.VMEM((n_chunks * ng, 128), F32),
                        pltpu.VMEM((ng, n_chunks, width), F32)],
        compiler_params=_cparams(("arbitrary", "arbitrary"), 40),
        name="s5_scan",
    )(pu, pu, w1, q, a_t)


def _ret_kernel(lg_ref, ql_ref, kl_ref, vl_ref, gl_ref, qc_ref, kc_ref, vc_ref, gc_ref, cos_ref, sin_ref,
                ol_ref, oc_ref, qs, ks, vs, acc, sf, sb, dm, *, n_ctx, n_lat):
    c = RET_BLOCK
    half = RET_HEAD_DIM // 2
    head = pl.program_id(1)
    lgf = lg_ref[0, head]
    lgb = lg_ref[1, head]
    qscale = RET_HEAD_DIM ** -0.5
    ctx_rows = n_ctx * c

    ii = lax.broadcasted_iota(jnp.int32, (c, c), 0)
    jj = lax.broadcasted_iota(jnp.int32, (c, c), 1)
    rel = (ii - jj).astype(F32)
    dsum = (jnp.where(rel >= 0, jnp.exp(lgf * jnp.maximum(rel, 0.0)), 0.0)
            + jnp.where(rel <= 0, jnp.exp(lgb * jnp.maximum(-rel, 0.0)), 0.0))
    pos = lax.broadcasted_iota(jnp.int32, (c, 1), 0).astype(F32)
    qdec_f = jnp.exp(lgf * (pos + 1.0))
    kdec_f = jnp.exp(lgf * (c - 1.0 - pos))
    qdec_b = jnp.exp(lgb * (c - pos))
    kdec_b = jnp.exp(lgb * pos)
    cg_f = jnp.exp(lgf * c)
    cg_b = jnp.exp(lgb * c)

    qs[0:ctx_rows, :] = (qc_ref[...].astype(F32) * qscale).astype(BF16)
    ks[0:ctx_rows, :] = kc_ref[...]
    vs[0:ctx_rows, :] = vc_ref[...]

    def rope_body(n, _):
        r = pl.multiple_of(n * c, c)
        cs = cos_ref[pl.ds(r, c), :]
        sn = sin_ref[pl.ds(r, c), :]

        def rot(t):
            t1, t2 = t[:, :half], t[:, half:]
            return jnp.concatenate([t1 * cs - t2 * sn, t2 * cs + t1 * sn], axis=1)

        dst = pl.multiple_of(ctx_rows + r, c)
        qs[pl.ds(dst, c), :] = (rot(ql_ref[pl.ds(r, c), :].astype(F32)) * qscale).astype(BF16)
        ks[pl.ds(dst, c), :] = rot(kl_ref[pl.ds(r, c), :].astype(F32)).astype(BF16)
        vs[pl.ds(dst, c), :] = vl_ref[pl.ds(r, c), :]
        return 0

    lax.fori_loop(0, n_lat, rope_body, 0)

    dm[...] = dsum
    sf[...] = jnp.zeros_like(sf)
    sb[...] = jnp.zeros_like(sb)

    def load(r):
        return qs[pl.ds(r, c), :], ks[pl.ds(r, c), :], vs[pl.ds(r, c), :]

    def intra_of(q, k, v):
        s = lax.dot_general(q, k, (((1,), (1,)), ((), ())), preferred_element_type=F32)
        return jnp.dot((s * dm[...]).astype(BF16), v, preferred_element_type=F32)

    def step(s_ref, q, k, v, qdec, kdec, cg):
        inter = jnp.dot(q, s_ref[...].astype(BF16), preferred_element_type=F32) * qdec
        kd = (k.astype(F32) * kdec).astype(BF16)
        s_ref[...] = cg * s_ref[...] + lax.dot_general(kd, v, (((0,), (0,)), ((), ())),
                                                       preferred_element_type=F32)
        return inter

    def finish(o, gate, out_ref, ro):
        o = o * lax.rsqrt(jnp.mean(o * o, axis=-1, keepdims=True) + NORM_EPS)
        out_ref[pl.ds(ro, c), :] = (o * (gate * jax.nn.sigmoid(gate))).astype(out_ref.dtype)

    fwd = (sf, qdec_f, kdec_f, cg_f)
    bwd = (sb, qdec_b, kdec_b, cg_b)

    for n in range(n_ctx):
        q, k, v = load(n * c)
        acc[n * c:(n + 1) * c, :] = intra_of(q, k, v) + step(fwd[0], q, k, v, *fwd[1:])
    for n in range(n_ctx - 1, -1, -1):
        q, k, v = load(n * c)
        o = acc[n * c:(n + 1) * c, :] + step(bwd[0], q, k, v, *bwd[1:])
        finish(o, gc_ref[n * c:(n + 1) * c, :].astype(F32), oc_ref, n * c)

    def first_visits(i, _):
        rf = pl.multiple_of(i * c, c)
        rb = pl.multiple_of((n_lat - 1 - i) * c, c)
        q, k, v = load(ctx_rows + rf)
        acc[pl.ds(ctx_rows + rf, c), :] = intra_of(q, k, v) + step(fwd[0], q, k, v, *fwd[1:])
        q, k, v = load(ctx_rows + rb)
        acc[pl.ds(ctx_rows + rb, c), :] = step(bwd[0], q, k, v, *bwd[1:])
        return 0

    def second_visits(i, _):
        rf = pl.multiple_of(i * c, c)
        rb = pl.multiple_of((n_lat - 1 - i) * c, c)
        q, k, v = load(ctx_rows + rf)
        o = acc[pl.ds(ctx_rows + rf, c), :] + intra_of(q, k, v) + step(fwd[0], q, k, v, *fwd[1:])
        finish(o, gl_ref[pl.ds(rf, c), :].astype(F32), ol_ref, rf)
        q, k, v = load(ctx_rows + rb)
        o = acc[pl.ds(ctx_rows + rb, c), :] + step(bwd[0], q, k, v, *bwd[1:])
        finish(o, gl_ref[pl.ds(rb, c), :].astype(F32), ol_ref, rb)
        return 0

    lax.fori_loop(0, n_lat // 2, first_visits, 0)
    lax.fori_loop(n_lat // 2, n_lat, second_visits, 0)


def _retention(p, log_gamma, cos, sin, *, bsz, n_lat, n_ctx, col0):
    t = p.shape[0]
    dh = RET_HEAD_DIM
    heads = (p.shape[1] - col0) // (4 * dh)
    qb, kb, vb, gb = (col0 // dh + i * heads for i in range(4))
    ctx_blk0 = bsz * n_lat // n_ctx

    def lat(cb):
        return pl.BlockSpec((n_lat, dh), lambda b, h: (b, cb + h))

    def ctx(cb):
        return pl.BlockSpec((n_ctx, dh), lambda b, h: (ctx_blk0 + b, cb + h))

    tab = pl.BlockSpec((n_lat, dh // 2), lambda b, h: (0, 0))
    assert n_ctx % RET_BLOCK == 0 and (n_lat // RET_BLOCK) % 2 == 0
    kern = functools.partial(_ret_kernel, n_ctx=n_ctx // RET_BLOCK, n_lat=n_lat // RET_BLOCK)
    seq = n_ctx + n_lat
    return pl.pallas_call(
        kern,
        out_shape=(jax.ShapeDtypeStruct((bsz * n_lat, heads * dh), BF16),
                   jax.ShapeDtypeStruct((bsz * n_ctx, heads * dh), BF16)),
        grid=(bsz, heads),
        in_specs=[pl.BlockSpec(memory_space=pltpu.SMEM),
                  lat(qb), lat(kb), lat(vb), lat(gb), ctx(qb), ctx(kb), ctx(vb), ctx(gb), tab, tab],
        out_specs=(pl.BlockSpec((n_lat, dh), lambda b, h: (b, h)),
                   pl.BlockSpec((n_ctx, dh), lambda b, h: (b, h))),
        scratch_shapes=[pltpu.VMEM((seq, dh), BF16), pltpu.VMEM((seq, dh), BF16), pltpu.VMEM((seq, dh), BF16),
                        pltpu.VMEM((seq, dh), F32), pltpu.VMEM((dh, dh), F32), pltpu.VMEM((dh, dh), F32),
                        pltpu.VMEM((RET_BLOCK, RET_BLOCK), F32)],
        compiler_params=_cparams(("arbitrary", "arbitrary"), 52),
        name="retention",
    )(log_gamma, p, p, p, p, p, p, p, p, cos, sin)


def _top2_sum(a, b, c, d):
    hi1, lo1 = jnp.maximum(a, b), jnp.minimum(a, b)
    hi2, lo2 = jnp.maximum(c, d), jnp.minimum(c, d)
    return jnp.maximum(hi1, hi2) + jnp.maximum(jnp.minimum(hi1, hi2), jnp.maximum(lo1, lo2))


def _route(logits_t, rb):
    sc = jax.nn.sigmoid(logits_t)
    biased = sc + rb
    per = EXPERTS_PER_GROUP
    brow = [biased[e:e + 1, :] for e in range(N_EXPERTS)]
    srow = [sc[e:e + 1, :] for e in range(N_EXPERTS)]
    gscore = [_top2_sum(*brow[per * g:per * g + per]) for g in range(N_EXPERT_GROUPS)]
    best_v = gscore[0]
    best_g = jnp.zeros_like(best_v, dtype=jnp.int32)
    for g in range(1, N_EXPERT_GROUPS):
        upd = gscore[g] > best_v
        best_v = jnp.where(upd, gscore[g], best_v)
        best_g = jnp.where(upd, g, best_g)
    vals, sels = [], []
    for j in range(per):
        v = brow[j]
        s = srow[j]
        for g in range(1, N_EXPERT_GROUPS):
            v = jnp.where(best_g == g, brow[per * g + j], v)
            s = jnp.where(best_g == g, srow[per * g + j], s)
        vals.append(v)
        sels.append(s)
    v1, i1, s1 = vals[0], jnp.zeros_like(best_g), sels[0]
    for j in range(1, per):
        upd = vals[j] > v1
        v1 = jnp.where(upd, vals[j], v1)
        s1 = jnp.where(upd, sels[j], s1)
        i1 = jnp.where(upd, j, i1)
    v2 = jnp.full_like(v1, -jnp.inf)
    i2, s2 = jnp.zeros_like(best_g), jnp.zeros_like(s1)
    for j in range(per):
        upd = (i1 != j) & (vals[j] > v2)
        v2 = jnp.where(upd, vals[j], v2)
        s2 = jnp.where(upd, sels[j], s2)
        i2 = jnp.where(upd, j, i2)
    tot = s1 + s2
    idx = jnp.concatenate([best_g * per + i1, best_g * per + i2], axis=0)
    wts = jnp.concatenate([s1 / tot, s2 / tot], axis=0)
    return idx, wts


def _mix_kernel(ysl_ref, ysc_ref, u_ref, rl_ref, rc_ref, x_ref, m_ref, dsk_ref, gw_ref, gb_ref, ws_ref, wr_ref,
                g2_ref, rw_ref, rb_ref, xo_ref, f_ref, ei_ref, ew_ref, *, lat_blocks, d):
    is_lat = pl.program_id(0) < lat_blocks
    ys = jnp.where(is_lat, ysl_ref[...], ysc_ref[...])
    y = ys + dsk_ref[...] * u_ref[...]
    y = jax.nn.gelu(y, approximate=True)
    z = jnp.dot(y.astype(BF16), gw_ref[...], preferred_element_type=F32) + gb_ref[...]
    s5o = (y * jax.nn.sigmoid(z)).astype(BF16)
    r = jnp.where(is_lat, rl_ref[...], rc_ref[...])
    mixed = (jnp.dot(s5o, ws_ref[...], preferred_element_type=F32)
             + jnp.dot(r, wr_ref[...], preferred_element_type=F32))
    gate1 = m_ref[:, 2 * d:3 * d]
    x = x_ref[...] + gate1 * mixed
    xo_ref[...] = x
    ms = jnp.mean(x * x, axis=-1, keepdims=True)
    f = x * lax.rsqrt(ms + NORM_EPS) * g2_ref[...]
    f = f * (1.0 + m_ref[:, 4 * d:5 * d]) + m_ref[:, 3 * d:4 * d]
    f_ref[...] = f
    logits_t = lax.dot_general(rw_ref[...], f, (((1,), (1,)), ((), ())), precision=HIGHEST,
                               preferred_element_type=F32)
    idx, wts = _route(logits_t, rb_ref[...])
    ei_ref[...] = idx
    ew_ref[...] = wts


def _mix(ysl, ysc, pu, rl, rc, x, mods_cur, d_skip, glu_w, glu_b, w_s, w_r, g2, rw_t, rb, *, n_lat,
         rows_per_batch):
    t, d = x.shape
    sw = pu.shape[1]
    tm = MIX_ROW_BLOCK
    lat_blocks = n_lat // tm
    per_b = rows_per_batch // tm
    ctx_row = n_lat // rows_per_batch
    const = lambda i: (0, 0)
    row = lambda i: (i, 0)
    lat_row = lambda i: (jnp.minimum(i, lat_blocks - 1), 0)
    ctx_rowblk = lambda i: (jnp.maximum(i - lat_blocks, 0), 0)
    once = pl.Buffered(1)
    kern = functools.partial(_mix_kernel, lat_blocks=lat_blocks, d=d)
    return pl.pallas_call(
        kern,
        out_shape=(jax.ShapeDtypeStruct((t, d), F32), jax.ShapeDtypeStruct((t, d), F32),
                   jax.ShapeDtypeStruct((TOP_K, t), jnp.int32), jax.ShapeDtypeStruct((TOP_K, t), F32)),
        grid=(t // tm,),
        in_specs=[pl.BlockSpec((tm, sw), lat_row),
                  pl.BlockSpec((tm, sw), ctx_rowblk, pipeline_mode=once),
                  pl.BlockSpec((tm, sw), row),
                  pl.BlockSpec((tm, sw), lat_row),
                  pl.BlockSpec((tm, sw), ctx_rowblk, pipeline_mode=once),
                  pl.BlockSpec((tm, d), row),
                  pl.BlockSpec((None, 1, 6 * d), lambda i: (jnp.where(i < lat_blocks, i // per_b, ctx_row), 0, 0)),
                  pl.BlockSpec((1, sw), const),
                  pl.BlockSpec((sw, sw), const, pipeline_mode=once),
                  pl.BlockSpec((1, sw), const),
                  pl.BlockSpec((sw, d), const, pipeline_mode=once),
                  pl.BlockSpec((d - sw, d), const, pipeline_mode=once),
                  pl.BlockSpec((1, d), const),
                  pl.BlockSpec((N_EXPERTS, d), const),
                  pl.BlockSpec((N_EXPERTS, 1), const)],
        out_specs=(pl.BlockSpec((tm, d), row), pl.BlockSpec((tm, d), row),
                   pl.BlockSpec((TOP_K, tm), lambda i: (0, i)), pl.BlockSpec((TOP_K, tm), lambda i: (0, i))),
        compiler_params=_cparams(("arbitrary",), 58),
        name="mix",
    )(ysl, ysc, pu, rl, rc, x, mods_cur, d_skip, glu_w, glu_b, w_s, w_r, g2, rw_t, rb)


def _expert_kernel(bexp_ref, nused_ref, src0_ref, srcn_ref, dst_ref, f_hbm, sw_ref, wg_ref, wu_ref, wd_ref,
                   y_hbm, xbuf, ybuf, sem_in, sem_out, pending, *, bm, nblk, n_real):
    j = pl.program_id(0)
    slot = j % 2
    nused = nused_ref[0]
    unroll = 8

    def gather(idx_ref, s):
        def body(r8, _):
            for k in range(unroll):
                r = r8 * unroll + k
                pltpu.make_async_copy(f_hbm.at[pl.ds(idx_ref[0, r], 1)], xbuf.at[s, pl.ds(r, 1)],
                                      sem_in.at[s]).start()
            return 0
        lax.fori_loop(0, bm // unroll, body, 0)

    def wait_in(s):
        pltpu.make_async_copy(f_hbm.at[pl.ds(0, bm)], xbuf.at[s], sem_in.at[s]).wait()

    def wait_out(s):
        pltpu.make_async_copy(ybuf.at[s], y_hbm.at[pl.ds(0, bm)], sem_out.at[s]).wait()

    @pl.when(j == 0)
    def _():
        pending[0] = 0
        pending[1] = 0
        ybuf[1] = jnp.zeros(ybuf.shape[1:], ybuf.dtype)
        for half in range(2):
            pltpu.make_async_copy(ybuf.at[1], y_hbm.at[pl.ds(n_real + half * bm, bm)], sem_out.at[1]).start()
        for half in range(2):
            wait_out(1)
        gather(src0_ref, 0)

    @pl.when(j + 1 < nused)
    def _():
        gather(srcn_ref, 1 - slot)

    @pl.when(j < nused)
    def _():
        wait_in(slot)
        xb = xbuf[slot].astype(BF16)
        gt = jnp.dot(xb, wg_ref[...], preferred_element_type=F32)
        up = jnp.dot(xb, wu_ref[...], preferred_element_type=F32)
        hmid = (gt * jax.nn.sigmoid(gt) * up).astype(BF16)
        y = jnp.dot(hmid, wd_ref[...], preferred_element_type=F32) * sw_ref[...]

        @pl.when(pending[slot] == 1)
        def _():
            wait_out(slot)

        ybuf[slot] = y

        def sbody(r8, _):
            for k in range(unroll):
                r = r8 * unroll + k
                pltpu.make_async_copy(ybuf.at[slot, pl.ds(r, 1)], y_hbm.at[pl.ds(dst_ref[0, r], 1)],
                                      sem_out.at[slot]).start()
            return 0
        lax.fori_loop(0, bm // unroll, sbody, 0)
        pending[slot] = 1

    @pl.when(j == nblk - 1)
    def _():
        for s in range(2):
            @pl.when(pending[s] == 1)
            def _():
                wait_out(s)
            pending[s] = 0


def _experts(f, bexp, nused, src, dst, sw, wg, wu, wd, *, out_rows):
    t, d = f.shape
    bm = MOE_ROWS
    nblk = bexp.shape[0]
    ff = wg.shape[2]
    kern = functools.partial(_expert_kernel, bm=bm, nblk=nblk, n_real=out_rows - 2 * bm)
    smem_blk = lambda fn: pl.BlockSpec((None, 1, bm), fn, memory_space=pltpu.SMEM)
    grid_spec = pltpu.PrefetchScalarGridSpec(
        num_scalar_prefetch=2,
        grid=(nblk,),
        in_specs=[smem_blk(lambda j, be, nu: (0, 0, 0)),
                  smem_blk(lambda j, be, nu: (jnp.minimum(j + 1, nblk - 1), 0, 0)),
                  smem_blk(lambda j, be, nu: (j, 0, 0)),
                  pl.BlockSpec(memory_space=pl.ANY),
                  pl.BlockSpec((bm, 1), lambda j, be, nu: (j, 0)),
                  pl.BlockSpec((None, d, ff), lambda j, be, nu: (be[j], 0, 0)),
                  pl.BlockSpec((None, d, ff), lambda j, be, nu: (be[j], 0, 0)),
                  pl.BlockSpec((None, ff, d), lambda j, be, nu: (be[j], 0, 0))],
        out_specs=pl.BlockSpec(memory_space=pl.ANY),
        scratch_shapes=[pltpu.VMEM((2, bm, d), F32), pltpu.VMEM((2, bm, d), F32),
                        pltpu.SemaphoreType.DMA((2,)), pltpu.SemaphoreType.DMA((2,)),
                        pltpu.SMEM((2,), jnp.int32)],
    )
    src3 = src.reshape(nblk, 1, bm)
    return pl.pallas_call(
        kern,
        out_shape=jax.ShapeDtypeStruct((out_rows, d), F32),
        grid_spec=grid_spec,
        compiler_params=_cparams(("arbitrary",), 52),
        name="experts",
    )(bexp, nused, src3, src3, dst.reshape(nblk, 1, bm), f, sw, wg, wu, wd)


def _dispatch_plan(eidx, ew, n_tok):
    bm = MOE_ROWS
    n_assign = TOP_K * n_tok
    nblk = -(-n_assign // bm) + N_EXPERTS
    flat_e = eidx.reshape(-1)
    onehot = (flat_e[:, None] == jnp.arange(N_EXPERTS, dtype=jnp.int32)[None, :]).astype(jnp.int32)
    csum = jnp.cumsum(onehot, axis=0)
    counts = csum[-1]
    rank = jnp.take_along_axis(csum, flat_e[:, None], axis=1)[:, 0] - 1
    padded = (counts + bm - 1) // bm * bm
    pend = jnp.cumsum(padded)
    pstart = pend - padded
    dest = pstart[flat_e] + rank
    a_ids = jnp.arange(n_assign, dtype=jnp.int32)
    slots = nblk * bm
    owner = jnp.full((slots,), -1, jnp.int32).at[dest].set(a_ids)
    real = owner >= 0
    src = jnp.where(real, owner % n_tok, 0)
    slot_ids = jnp.arange(slots, dtype=jnp.int32)
    dst = jnp.where(real, owner, n_assign + ((slot_ids // bm) % 2) * bm + slot_ids % bm)
    sw = jnp.where(real, ew.reshape(-1)[jnp.maximum(owner, 0)], 0.0).reshape(slots, 1)
    blk_start = jnp.arange(nblk, dtype=jnp.int32) * bm
    bexp = jnp.minimum(jnp.sum((pend[None, :] <= blk_start[:, None]).astype(jnp.int32), axis=1), N_EXPERTS - 1)
    nused = (pend[-1] // bm).astype(jnp.int32).reshape(1)
    return bexp, nused, src, dst, sw, n_assign + 2 * bm


def _rope_tables(n_lat):
    n_rows = n_lat // GRID_W
    row = jnp.broadcast_to(jnp.arange(n_rows, dtype=F32)[:, None], (n_rows, GRID_W)).reshape(-1)
    col = jnp.broadcast_to(jnp.arange(GRID_W, dtype=F32)[None, :], (n_rows, GRID_W)).reshape(-1)
    quarter = RET_HEAD_DIM // 4
    freqs = ROPE_BASE ** (-jnp.arange(quarter, dtype=F32) / quarter)
    ang = jnp.concatenate([row[:, None] * freqs, col[:, None] * freqs], axis=-1)
    return jnp.cos(ang), jnp.sin(ang)


def kernel(x, c, ctx, c_ctx, mod_w, mod_b, norm1_g, norm2_g, w_in, s5_a_re, s5_a_im, s5_log_dt, s5_b_re, s5_b_im,
           s5_c_re, s5_c_im, s5_d, glu_w, glu_b, ret_decay_raw, w_out, router_w, router_b, exp_w_gate, exp_w_up,
           exp_w_down, final_norm_g):
    bsz, n_lat, d = x.shape
    n_ctx = ctx.shape[1]
    depth = mod_w.shape[0]
    s5w = s5_d.shape[1]
    t = bsz * (n_lat + n_ctx)
    lat_total = bsz * n_lat

    xs = jnp.concatenate([x.reshape(lat_total, d), ctx.reshape(bsz * n_ctx, d)], axis=0)
    c8 = jnp.zeros((8, d), F32).at[:bsz].set(c).at[bsz].set(c_ctx)
    mods = _mods(c8, mod_w, mod_b).reshape(depth, 8, 1, 6 * d)
    cos, sin = _rope_tables(n_lat)
    rw_t = router_w.T
    rb = router_b.reshape(N_EXPERTS, 1)

    ys2 = None
    for layer in range(depth):
        xs, h = _prenorm(xs, ys2, mods[layer - 1] if layer else None, mods[layer], norm1_g[layer],
                         n_lat=lat_total, rows_per_batch=n_lat, final=False)
        pu, pr = _in_proj(h, w_in[layer].astype(BF16), s5w)

        w1, q, a16 = _s5_weights(s5_a_re[layer], s5_a_im[layer], s5_log_dt[layer], s5_b_re[layer],
                                 s5_b_im[layer], s5_c_re[layer], s5_c_im[layer])
        ysl, ysc = _s5_scan(pu, w1, q, a16, bsz=bsz, n_lat=n_lat, n_ctx=n_ctx)

        log_gamma = jnp.log1p(-jnp.exp2(-ret_decay_raw[layer].astype(F32)))
        rl, rc = _retention(pr, log_gamma, cos, sin, bsz=bsz, n_lat=n_lat, n_ctx=n_ctx, col0=0)

        w_o = w_out[layer].astype(BF16)
        xs, f, eidx, ew = _mix(ysl, ysc, pu, rl, rc, xs, mods[layer], s5_d[layer].reshape(1, s5w),
                               glu_w[layer].astype(BF16), glu_b[layer].reshape(1, s5w), w_o[:s5w], w_o[s5w:],
                               norm2_g[layer].reshape(1, d), rw_t, rb, n_lat=lat_total, rows_per_batch=n_lat)

        bexp, nused, src, dst, sw, out_rows = _dispatch_plan(eidx, ew, t)
        ys2 = _experts(f, bexp, nused, src, dst, sw, exp_w_gate[layer].astype(BF16),
                       exp_w_up[layer].astype(BF16), exp_w_down[layer].astype(BF16), out_rows=out_rows)

    out = _prenorm(xs, ys2, mods[depth - 1], mods[depth - 1], final_norm_g,
                   n_lat=lat_total, rows_per_batch=n_lat, final=True)
    return out.reshape(bsz, n_lat, d)
```

```python
import functools
import math

import jax
import jax.numpy as jnp
from jax import lax
from jax.experimental import pallas as pl
from jax.experimental.pallas import tpu as pltpu

F32 = jnp.float32
BF16 = jnp.bfloat16
HIGHEST = lax.Precision.HIGHEST

GRID_W = 64
S5_GROUP = 16
S5_STATE = 64
S5_STEP = 16
RET_HEAD_DIM = 256
RET_BLOCK = 256
ROPE_BASE = 10000.0
N_EXPERTS = 16
N_EXPERT_GROUPS = 4
EXPERTS_PER_GROUP = 4
TOP_K = 2
NORM_EPS = 1e-6

ROW_BLOCK = 256
MIX_ROW_BLOCK = 512
MM_ROW_BLOCK = 1024
MM_COL_BLOCK = 1024
MOE_ROWS = 256
ROW_TILES = 16
DMA_UNROLL = 8
S5_TILE_GROUPS = 128 // S5_GROUP
MIB = 1024 * 1024


def _cparams(sem, vmem_mib):
    return pltpu.CompilerParams(dimension_semantics=sem, vmem_limit_bytes=int(vmem_mib * MIB))


def _mods_kernel(c_ref, w_ref, b_ref, o_ref):
    cv = c_ref[...]
    act = cv * jax.nn.sigmoid(cv)
    o_ref[0] = jnp.dot(act, w_ref[0], precision=HIGHEST, preferred_element_type=F32) + b_ref[0]


def _mods(c8, mod_w, mod_b):
    depth, d, n = mod_w.shape
    tn = 1024
    return pl.pallas_call(
        _mods_kernel,
        out_shape=jax.ShapeDtypeStruct((depth, 8, n), F32),
        grid=(depth, n // tn),
        in_specs=[pl.BlockSpec((8, d), lambda l, j: (0, 0)),
                  pl.BlockSpec((1, d, tn), lambda l, j: (l, 0, j)),
                  pl.BlockSpec((1, 1, tn), lambda l, j: (l, 0, j))],
        out_specs=pl.BlockSpec((1, 8, tn), lambda l, j: (l, 0, j)),
        compiler_params=_cparams(("arbitrary", "arbitrary"), 40),
        name="adaln_mods",
    )(c8, mod_w, mod_b.reshape(depth, 1, n))


def _prenorm_kernel(*refs, combine, final, d, tm, nblk):
    if combine:
        (dest0_ref, destn_ref, x_ref, ys_hbm, wt_ref, mprev_ref, mcur_ref, g_ref), rest = refs[:8], refs[8:]
        outs, (ybuf, sem) = rest[:-2], rest[-2:]
    else:
        (x_ref, mcur_ref, g_ref), outs = refs[:3], refs[3:]
    x = x_ref[...]
    if combine:
        i = pl.program_id(0)
        slot = i % 2
        rpt = ROW_TILES
        half = tm * rpt
        per_slot = TOP_K * half

        def gather(dref, s):
            def body(t8, _):
                for u in range(DMA_UNROLL):
                    tt = t8 * DMA_UNROLL + u
                    for k in range(TOP_K):
                        src = ys_hbm.at[pl.ds(pl.multiple_of(dref[k, tt] * rpt, rpt), rpt)]
                        dst = ybuf.at[pl.ds(pl.multiple_of(s * per_slot + k * half + tt * rpt, rpt), rpt)]
                        pltpu.make_async_copy(src, dst, sem.at[s]).start()
                return 0
            lax.fori_loop(0, tm // DMA_UNROLL, body, 0)

        @pl.when(i == 0)
        def _():
            gather(dest0_ref, 0)

        @pl.when(i + 1 < nblk)
        def _():
            gather(destn_ref, 1 - slot)

        base = pl.multiple_of(slot * per_slot, per_slot)
        pltpu.make_async_copy(ys_hbm.at[pl.ds(0, per_slot)], ybuf.at[pl.ds(base, per_slot)], sem.at[slot]).wait()

        def rows_of(k):
            return jnp.concatenate([ybuf[pl.ds(base + k * half + j, tm, stride=rpt), :] for j in range(rpt)], axis=1)

        gate2 = mprev_ref[:, 5 * d:6 * d]
        x = x + gate2 * (wt_ref[:, 0:1] * rows_of(0) + wt_ref[:, 1:2] * rows_of(1))
    ms = jnp.mean(x * x, axis=-1, keepdims=True)
    xn = x * lax.rsqrt(ms + NORM_EPS) * g_ref[...]
    if final:
        outs[0][...] = xn
    else:
        outs[0][...] = x
        shift = mcur_ref[:, 0:d]
        scale = mcur_ref[:, d:2 * d]
        outs[1][...] = (xn * (1.0 + scale) + shift).astype(BF16)


def _prenorm(x, moe, mods_prev, mods_cur, gain, *, n_lat, rows_per_batch, final):
    t, d = x.shape
    tm = ROW_BLOCK
    combine = moe is not None
    nblk = (n_lat if final else t) // tm
    lat_blocks = n_lat // tm
    per_b = rows_per_batch // tm
    ctx_row = n_lat // rows_per_batch

    def mod_idx(i):
        return (jnp.where(i < lat_blocks, i // per_b, ctx_row), 0, 0)

    row_spec = pl.BlockSpec((tm, d), lambda i: (i, 0))
    mod_spec = pl.BlockSpec((None, 1, 6 * d), mod_idx)
    g_spec = pl.BlockSpec((1, d), lambda i: (0, 0))
    ins, specs, scratch = [], [], []
    if combine:
        ys, dest, wt = moe
        ins += [dest, dest]
        specs += [pl.BlockSpec((TOP_K, tm), lambda i: (0, 0), memory_space=pltpu.SMEM),
                  pl.BlockSpec((TOP_K, tm), lambda i: (0, jnp.minimum(i + 1, nblk - 1)), memory_space=pltpu.SMEM)]
    ins.append(x)
    specs.append(row_spec)
    if combine:
        ins += [ys, wt, mods_prev]
        specs += [pl.BlockSpec(memory_space=pl.ANY), pl.BlockSpec((tm, TOP_K), lambda i: (i, 0)), mod_spec]
        scratch = [pltpu.VMEM((2 * TOP_K * tm * ROW_TILES, 128), F32), pltpu.SemaphoreType.DMA((2,))]
    ins += [mods_cur, gain.reshape(1, d)]
    specs += [mod_spec, g_spec]
    if final:
        out_shape = jax.ShapeDtypeStruct((n_lat, d), F32)
        out_specs = row_spec
    else:
        out_shape = (jax.ShapeDtypeStruct((t, d), F32), jax.ShapeDtypeStruct((t, d), BF16))
        out_specs = (row_spec, row_spec)
    return pl.pallas_call(
        functools.partial(_prenorm_kernel, combine=combine, final=final, d=d, tm=tm, nblk=nblk),
        out_shape=out_shape, grid=(nblk,), in_specs=specs, out_specs=out_specs, scratch_shapes=scratch,
        compiler_params=_cparams(("arbitrary",), 40),
        name="prenorm_final" if final else ("prenorm_combine" if combine else "prenorm"),
    )(*ins)


def _inproj_kernel(h_ref, w_ref, pu_ref, pr_ref):
    acc = jnp.dot(h_ref[...], w_ref[...], preferred_element_type=F32)

    @pl.when(pl.program_id(1) == 0)
    def _():
        pu_ref[...] = acc

    @pl.when(pl.program_id(1) > 0)
    def _():
        pr_ref[...] = acc.astype(pr_ref.dtype)


def _in_proj(h, w, s5w):
    t, k = h.shape
    n = w.shape[1]
    tm, tn = MM_ROW_BLOCK, MM_COL_BLOCK
    assert s5w == tn
    return pl.pallas_call(
        _inproj_kernel,
        out_shape=(jax.ShapeDtypeStruct((t, s5w), F32), jax.ShapeDtypeStruct((t, n - s5w), BF16)),
        grid=(t // tm, n // tn),
        in_specs=[pl.BlockSpec((tm, k), lambda i, j: (i, 0)),
                  pl.BlockSpec((k, tn), lambda i, j: (0, j))],
        out_specs=(pl.BlockSpec((tm, tn), lambda i, j: (i, 0)),
                   pl.BlockSpec((tm, tn), lambda i, j: (i, jnp.maximum(j - 1, 0)))),
        compiler_params=_cparams(("arbitrary", "arbitrary"), 40),
        name="in_proj",
    )(h, w)


def _s5_weights(a_re, a_im, log_dt, b_re, b_im, c_re, c_im):
    ns = S5_STEP
    a_re, a_im = a_re.astype(F32), a_im.astype(F32)
    dt = jnp.exp(log_dt.astype(F32))[..., None]
    lam_re, lam_im = dt * a_re, dt * a_im
    tau = jnp.arange(ns + 1, dtype=F32)[:, None, None, None]
    mag = jnp.exp(tau * lam_re[None])
    p_re, p_im = mag * jnp.cos(tau * lam_im[None]), mag * jnp.sin(tau * lam_im[None])
    x, y = p_re[1] - 1.0, p_im[1]
    den = a_re * a_re + a_im * a_im
    k_re, k_im = (x * a_re + y * a_im) / den, (y * a_re - x * a_im) / den
    b_re, b_im = b_re.astype(F32), b_im.astype(F32)
    bb_re = k_re[..., None] * b_re - k_im[..., None] * b_im
    bb_im = k_re[..., None] * b_im + k_im[..., None] * b_re
    c_re, c_im = c_re.astype(F32), c_im.astype(F32)
    cp_re = c_re[None] * p_re[:, :, :, None, :] - c_im[None] * p_im[:, :, :, None, :]
    cp_im = c_re[None] * p_im[:, :, :, None, :] + c_im[None] * p_re[:, :, :, None, :]
    kmat = (jnp.einsum('tdgon,dgni->tdgoi', cp_re, bb_re, precision=HIGHEST)
            - jnp.einsum('tdgon,dgni->tdgoi', cp_im, bb_im, precision=HIGHEST))
    s_idx = jnp.arange(ns)[:, None]
    t_idx = jnp.arange(ns)[None, :]
    lag = t_idx - s_idx
    kf = jnp.where((lag >= 0)[:, :, None, None, None], kmat[jnp.clip(lag, 0, ns - 1), 0], 0.0)
    kb = jnp.where((lag <= 0)[:, :, None, None, None], kmat[jnp.clip(-lag, 0, ns - 1), 1], 0.0)
    g = a_re.shape[1]
    m = jnp.transpose(kf + kb, (2, 0, 4, 1, 3)).reshape(g, ns * S5_GROUP, ns * S5_GROUP)

    def inject(p_r, p_i, d):
        r = p_r[:, :, :, None] * bb_re[d][None] - p_i[:, :, :, None] * bb_im[d][None]
        i = p_r[:, :, :, None] * bb_im[d][None] + p_i[:, :, :, None] * bb_re[d][None]
        tr = lambda z: jnp.transpose(z, (1, 0, 3, 2)).reshape(g, ns * S5_GROUP, S5_STATE)
        return tr(r), tr(i)

    rev = jnp.arange(ns - 1, -1, -1)
    pf_re, pf_im = inject(p_re[rev, 0], p_im[rev, 0], 0)
    pb_re, pb_im = inject(p_re[:ns, 1], p_im[:ns, 1], 1)
    w1 = jnp.concatenate([m, pf_re, pb_re, pf_im, pb_im], axis=-1)

    def carry(cp, taus, d):
        return jnp.transpose(cp[taus, d], (1, 3, 0, 2)).reshape(g, S5_STATE, ns * S5_GROUP)

    up = jnp.arange(1, ns + 1)
    down = jnp.arange(ns, 0, -1)
    q = jnp.concatenate([carry(cp_re, up, 0), carry(cp_re, down, 1),
                         -carry(cp_im, up, 0), -carry(cp_im, down, 1)], axis=1)
    a16 = jnp.stack([jnp.concatenate([p_re[ns, 0], p_re[ns, 1]], axis=-1),
                     jnp.concatenate([p_im[ns, 0], p_im[ns, 1]], axis=-1)], axis=1)
    return w1.astype(BF16), q.astype(BF16), a16


def _group_transpose(xs):
    ng = len(xs)
    grp = lax.broadcasted_iota(jnp.int32, xs[0].shape, 1) // S5_GROUP
    rolled = []
    for j in range(ng):
        y = xs[j % ng]
        for k in range(1, ng):
            y = jnp.where(grp == k, xs[(k + j) % ng], y)
        rolled.append(y if j == 0 else pltpu.roll(y, S5_GROUP * j, 1))
    outs = []
    for a in range(ng):
        o = rolled[(-a) % ng]
        for b in range(1, ng):
            o = jnp.where(grp == b, rolled[(b - a) % ng], o)
        outs.append(o)
    return outs


def _s5_kernel(ul_ref, uc_ref, w1_ref, q_ref, a_ref, yl_ref, yc_ref, uflat, st_re, st_im, yfl, *,
               lat_chunks, ctx_chunks):
    ng = S5_TILE_GROUPS
    width = S5_STEP * S5_GROUP
    n_chunks = lat_chunks + ctx_chunks
    sub = 32

    def flatten(src_ref, c0, row0, n):
        for hh in range(2):
            xs = [src_ref[pl.ds(row0 + hh * 8 + t, n, stride=S5_STEP), :] for t in range(8)]
            outs = _group_transpose(xs)
            for k in range(ng):
                uflat[k, pl.ds(c0, n), hh * 128:(hh + 1) * 128] = outs[k].astype(BF16)

    def flat_body(i, _):
        c0 = pl.multiple_of(i * sub, sub)
        flatten(ul_ref, c0, c0 * S5_STEP, sub)
        return 0

    lax.fori_loop(0, lat_chunks // sub, flat_body, 0)
    flatten(uc_ref, lat_chunks, 0, ctx_chunks)

    for k in range(ng):
        z = jnp.dot(uflat[k], w1_ref[k], preferred_element_type=F32)
        yfl[k] = z[:, :width]
        st_re[pl.ds(k, n_chunks, stride=ng), :] = z[:, width:width + 128]
        st_im[pl.ds(k, n_chunks, stride=ng), :] = z[:, width + 128:]

    lane = lax.broadcasted_iota(jnp.int32, (ng, 128), 1)
    fwd_lane = lane < S5_STATE
    a_re = a_ref[0]
    a_im = a_ref[1]

    def make_body(base, n):
        def body(i, carry):
            h_re, h_im = carry
            rf = pl.multiple_of((base + i) * ng, ng)
            rb = pl.multiple_of((base + n - 1 - i) * ng, ng)
            tf_re, tf_im = st_re[pl.ds(rf, ng), :], st_im[pl.ds(rf, ng), :]
            tb_re, tb_im = st_re[pl.ds(rb, ng), :], st_im[pl.ds(rb, ng), :]
            t_re = jnp.where(fwd_lane, tf_re, tb_re)
            t_im = jnp.where(fwd_lane, tf_im, tb_im)
            n_re = a_re * h_re - a_im * h_im + t_re
            n_im = a_re * h_im + a_im * h_re + t_im
            st_re[pl.ds(rf, ng), :] = jnp.where(fwd_lane, h_re, tf_re)
            st_im[pl.ds(rf, ng), :] = jnp.where(fwd_lane, h_im, tf_im)
            st_re[pl.ds(rb, ng), :] = jnp.where(fwd_lane, tb_re, h_re)
            st_im[pl.ds(rb, ng), :] = jnp.where(fwd_lane, tb_im, h_im)
            return n_re, n_im
        return body

    carry = (jnp.zeros((ng, 128), F32), jnp.zeros((ng, 128), F32))
    carry = lax.fori_loop(0, ctx_chunks, make_body(lat_chunks, ctx_chunks), carry)
    carry = lax.fori_loop(0, lat_chunks, make_body(0, lat_chunks), carry)

    for k in range(ng):
        hk = jnp.concatenate([st_re[pl.ds(k, n_chunks, stride=ng), :], st_im[pl.ds(k, n_chunks, stride=ng), :]],
                             axis=1).astype(BF16)
        yfl[k] = yfl[k] + jnp.dot(hk, q_ref[k], preferred_element_type=F32)

    def unflatten(dst_ref, c0, row0, n):
        for hh in range(2):
            vs = [yfl[k, pl.ds(c0, n), hh * 128:(hh + 1) * 128] for k in range(ng)]
            ts = _group_transpose(vs)
            for t in range(8):
                dst_ref[pl.ds(row0 + hh * 8 + t, n, stride=S5_STEP), :] = ts[t]

    def unflat_body(i, _):
        c0 = pl.multiple_of(i * sub, sub)
        unflatten(yl_ref, c0, c0 * S5_STEP, sub)
        return 0

    lax.fori_loop(0, lat_chunks // sub, unflat_body, 0)
    unflatten(yc_ref, lat_chunks, 0, ctx_chunks)


def _s5_scan(pu, w1, q, a16, *, bsz, n_lat, n_ctx):
    g = w1.shape[0]
    ng = S5_TILE_GROUPS
    width = S5_STEP * S5_GROUP
    lat_chunks, ctx_chunks = n_lat // S5_STEP, n_ctx // S5_STEP
    assert lat_chunks % 32 == 0 and ctx_chunks % 2 == 0 and ctx_chunks % 16 == 0
    n_chunks = lat_chunks + ctx_chunks
    ctx_blk0 = bsz * n_lat // n_ctx
    a_t = jnp.transpose(a16.reshape(g // ng, ng, 2, 128), (0, 2, 1, 3))
    kern = functools.partial(_s5_kernel, lat_chunks=lat_chunks, ctx_chunks=ctx_chunks)
    return pl.pallas_call(
        kern,
        out_shape=(jax.ShapeDtypeStruct((bsz * n_lat, g * S5_GROUP), F32),
                   jax.ShapeDtypeStruct((bsz * n_ctx, g * S5_GROUP), F32)),
        grid=(g // ng, bsz),
        in_specs=[pl.BlockSpec((n_lat, 128), lambda i, b: (b, i)),
                  pl.BlockSpec((n_ctx, 128), lambda i, b: (ctx_blk0 + b, i)),
                  pl.BlockSpec((ng, width, 2 * width), lambda i, b: (i, 0, 0)),
                  pl.BlockSpec((ng, width, width), lambda i, b: (i, 0, 0)),
                  pl.BlockSpec((None, 2, ng, 128), lambda i, b: (i, 0, 0, 0))],
        out_specs=(pl.BlockSpec((n_lat, 128), lambda i, b: (b, i)),
                   pl.BlockSpec((n_ctx, 128), lambda i, b: (b, i))),
        scratch_shapes=[pltpu.VMEM((ng, n_chunks, width), BF16),
                        pltpu.VMEM((n_chunks * ng, 128), F32),
                        pltpu.VMEM((n_chunks * ng, 128), F32),
                        pltpu.VMEM((ng, n_chunks, width), F32)],
        compiler_params=_cparams(("arbitrary", "arbitrary"), 40),
        name="s5_scan",
    )(pu, pu, w1, q, a_t)


def _ret_kernel(lg_ref, ql_ref, kl_ref, vl_ref, gl_ref, qc_ref, kc_ref, vc_ref, gc_ref, cos_ref, sin_ref,
                ol_ref, oc_ref, qs, ks, vs, acc, sf, sb, dm, *, n_ctx, n_lat):
    c = RET_BLOCK
    half = RET_HEAD_DIM // 2
    head = pl.program_id(1)
    lgf = lg_ref[0, head]
    lgb = lg_ref[1, head]
    qscale = RET_HEAD_DIM ** -0.5
    ctx_rows = n_ctx * c

    ii = lax.broadcasted_iota(jnp.int32, (c, c), 0)
    jj = lax.broadcasted_iota(jnp.int32, (c, c), 1)
    rel = (ii - jj).astype(F32)
    dsum = (jnp.where(rel >= 0, jnp.exp(lgf * jnp.maximum(rel, 0.0)), 0.0)
            + jnp.where(rel <= 0, jnp.exp(lgb * jnp.maximum(-rel, 0.0)), 0.0))
    pos = lax.broadcasted_iota(jnp.int32, (c, 1), 0).astype(F32)
    qdec_f = jnp.exp(lgf * (pos + 1.0))
    kdec_f = jnp.exp(lgf * (c - 1.0 - pos))
    qdec_b = jnp.exp(lgb * (c - pos))
    kdec_b = jnp.exp(lgb * pos)
    cg_f = jnp.exp(lgf * c)
    cg_b = jnp.exp(lgb * c)

    qs[0:ctx_rows, :] = (qc_ref[...].astype(F32) * qscale).astype(BF16)
    ks[0:ctx_rows, :] = kc_ref[...]
    vs[0:ctx_rows, :] = vc_ref[...]

    def rope_body(n, _):
        r = pl.multiple_of(n * c, c)
        cs = cos_ref[pl.ds(r, c), :]
        sn = sin_ref[pl.ds(r, c), :]

        def rot(t):
            t1, t2 = t[:, :half], t[:, half:]
            return jnp.concatenate([t1 * cs - t2 * sn, t2 * cs + t1 * sn], axis=1)

        dst = pl.multiple_of(ctx_rows + r, c)
        qs[pl.ds(dst, c), :] = (rot(ql_ref[pl.ds(r, c), :].astype(F32)) * qscale).astype(BF16)
        ks[pl.ds(dst, c), :] = rot(kl_ref[pl.ds(r, c), :].astype(F32)).astype(BF16)
        vs[pl.ds(dst, c), :] = vl_ref[pl.ds(r, c), :]
        return 0

    lax.fori_loop(0, n_lat, rope_body, 0)

    dm[...] = dsum
    sf[...] = jnp.zeros_like(sf)
    sb[...] = jnp.zeros_like(sb)

    def load(r):
        return qs[pl.ds(r, c), :], ks[pl.ds(r, c), :], vs[pl.ds(r, c), :]

    def intra_of(q, k, v):
        s = lax.dot_general(q, k, (((1,), (1,)), ((), ())), preferred_element_type=F32)
        return jnp.dot((s * dm[...]).astype(BF16), v, preferred_element_type=F32)

    def step(s_ref, q, k, v, qdec, kdec, cg):
        inter = jnp.dot(q, s_ref[...].astype(BF16), preferred_element_type=F32) * qdec
        kd = (k.astype(F32) * kdec).astype(BF16)
        s_ref[...] = cg * s_ref[...] + lax.dot_general(kd, v, (((0,), (0,)), ((), ())),
                                                       preferred_element_type=F32)
        return inter

    def finish(o, gate, out_ref, ro):
        o = o * lax.rsqrt(jnp.mean(o * o, axis=-1, keepdims=True) + NORM_EPS)
        out_ref[pl.ds(ro, c), :] = (o * (gate * jax.nn.sigmoid(gate))).astype(out_ref.dtype)

    fwd = (sf, qdec_f, kdec_f, cg_f)
    bwd = (sb, qdec_b, kdec_b, cg_b)

    for n in range(n_ctx):
        q, k, v = load(n * c)
        acc[n * c:(n + 1) * c, :] = intra_of(q, k, v) + step(fwd[0], q, k, v, *fwd[1:])
    for n in range(n_ctx - 1, -1, -1):
        q, k, v = load(n * c)
        o = acc[n * c:(n + 1) * c, :] + step(bwd[0], q, k, v, *bwd[1:])
        finish(o, gc_ref[n * c:(n + 1) * c, :].astype(F32), oc_ref, n * c)

    def first_visits(i, _):
        rf = pl.multiple_of(i * c, c)
        rb = pl.multiple_of((n_lat - 1 - i) * c, c)
        q, k, v = load(ctx_rows + rf)
        acc[pl.ds(ctx_rows + rf, c), :] = intra_of(q, k, v) + step(fwd[0], q, k, v, *fwd[1:])
        q, k, v = load(ctx_rows + rb)
        acc[pl.ds(ctx_rows + rb, c), :] = step(bwd[0], q, k, v, *bwd[1:])
        return 0

    def second_visits(i, _):
        rf = pl.multiple_of(i * c, c)
        rb = pl.multiple_of((n_lat - 1 - i) * c, c)
        q, k, v = load(ctx_rows + rf)
        o = acc[pl.ds(ctx_rows + rf, c), :] + intra_of(q, k, v) + step(fwd[0], q, k, v, *fwd[1:])
        finish(o, gl_ref[pl.ds(rf, c), :].astype(F32), ol_ref, rf)
        q, k, v = load(ctx_rows + rb)
        o = acc[pl.ds(ctx_rows + rb, c), :] + step(bwd[0], q, k, v, *bwd[1:])
        finish(o, gl_ref[pl.ds(rb, c), :].astype(F32), ol_ref, rb)
        return 0

    lax.fori_loop(0, n_lat // 2, first_visits, 0)
    lax.fori_loop(n_lat // 2, n_lat, second_visits, 0)


def _retention(p, log_gamma, cos, sin, *, bsz, n_lat, n_ctx, col0):
    t = p.shape[0]
    dh = RET_HEAD_DIM
    heads = (p.shape[1] - col0) // (4 * dh)
    qb, kb, vb, gb = (col0 // dh + i * heads for i in range(4))
    ctx_blk0 = bsz * n_lat // n_ctx

    def lat(cb):
        return pl.BlockSpec((n_lat, dh), lambda b, h: (b, cb + h))

    def ctx(cb):
        return pl.BlockSpec((n_ctx, dh), lambda b, h: (ctx_blk0 + b, cb + h))

    tab = pl.BlockSpec((n_lat, dh // 2), lambda b, h: (0, 0))
    assert n_ctx % RET_BLOCK == 0 and (n_lat // RET_BLOCK) % 2 == 0
    kern = functools.partial(_ret_kernel, n_ctx=n_ctx // RET_BLOCK, n_lat=n_lat // RET_BLOCK)
    seq = n_ctx + n_lat
    return pl.pallas_call(
        kern,
        out_shape=(jax.ShapeDtypeStruct((bsz * n_lat, heads * dh), BF16),
                   jax.ShapeDtypeStruct((bsz * n_ctx, heads * dh), BF16)),
        grid=(bsz, heads),
        in_specs=[pl.BlockSpec(memory_space=pltpu.SMEM),
                  lat(qb), lat(kb), lat(vb), lat(gb), ctx(qb), ctx(kb), ctx(vb), ctx(gb), tab, tab],
        out_specs=(pl.BlockSpec((n_lat, dh), lambda b, h: (b, h)),
                   pl.BlockSpec((n_ctx, dh), lambda b, h: (b, h))),
        scratch_shapes=[pltpu.VMEM((seq, dh), BF16), pltpu.VMEM((seq, dh), BF16), pltpu.VMEM((seq, dh), BF16),
                        pltpu.VMEM((seq, dh), F32), pltpu.VMEM((dh, dh), F32), pltpu.VMEM((dh, dh), F32),
                        pltpu.VMEM((RET_BLOCK, RET_BLOCK), F32)],
        compiler_params=_cparams(("arbitrary", "arbitrary"), 52),
        name="retention",
    )(log_gamma, p, p, p, p, p, p, p, p, cos, sin)


def _top2_sum(a, b, c, d):
    hi1, lo1 = jnp.maximum(a, b), jnp.minimum(a, b)
    hi2, lo2 = jnp.maximum(c, d), jnp.minimum(c, d)
    return jnp.maximum(hi1, hi2) + jnp.maximum(jnp.minimum(hi1, hi2), jnp.maximum(lo1, lo2))


def _route(logits_t, rb):
    sc = jax.nn.sigmoid(logits_t)
    biased = sc + rb
    per = EXPERTS_PER_GROUP
    brow = [biased[e:e + 1, :] for e in range(N_EXPERTS)]
    srow = [sc[e:e + 1, :] for e in range(N_EXPERTS)]
    gscore = [_top2_sum(*brow[per * g:per * g + per]) for g in range(N_EXPERT_GROUPS)]
    best_v = gscore[0]
    best_g = jnp.zeros_like(best_v, dtype=jnp.int32)
    for g in range(1, N_EXPERT_GROUPS):
        upd = gscore[g] > best_v
        best_v = jnp.where(upd, gscore[g], best_v)
        best_g = jnp.where(upd, g, best_g)
    vals, sels = [], []
    for j in range(per):
        v = brow[j]
        s = srow[j]
        for g in range(1, N_EXPERT_GROUPS):
            v = jnp.where(best_g == g, brow[per * g + j], v)
            s = jnp.where(best_g == g, srow[per * g + j], s)
        vals.append(v)
        sels.append(s)
    v1, i1, s1 = vals[0], jnp.zeros_like(best_g), sels[0]
    for j in range(1, per):
        upd = vals[j] > v1
        v1 = jnp.where(upd, vals[j], v1)
        s1 = jnp.where(upd, sels[j], s1)
        i1 = jnp.where(upd, j, i1)
    v2 = jnp.full_like(v1, -jnp.inf)
    i2, s2 = jnp.zeros_like(best_g), jnp.zeros_like(s1)
    for j in range(per):
        upd = (i1 != j) & (vals[j] > v2)
        v2 = jnp.where(upd, vals[j], v2)
        s2 = jnp.where(upd, sels[j], s2)
        i2 = jnp.where(upd, j, i2)
    tot = s1 + s2
    idx = jnp.concatenate([best_g * per + i1, best_g * per + i2], axis=0)
    wts = jnp.concatenate([s1 / tot, s2 / tot], axis=0)
    return idx, wts


def _mix_kernel(ysl_ref, ysc_ref, u_ref, rl_ref, rc_ref, x_ref, m_ref, dsk_ref, gw_ref, gb_ref, ws_ref, wr_ref,
                g2_ref, rw_ref, rb_ref, xo_ref, fu_ref, ei_ref, ew_ref, rank_ref, cnt_ref, run_ref, *, lat_blocks, d):
    is_lat = pl.program_id(0) < lat_blocks
    ys = jnp.where(is_lat, ysl_ref[...], ysc_ref[...])
    y = ys + dsk_ref[...] * u_ref[...]
    y = jax.nn.gelu(y, approximate=True)
    z = jnp.dot(y.astype(BF16), gw_ref[...], preferred_element_type=F32) + gb_ref[...]
    s5o = (y * jax.nn.sigmoid(z)).astype(BF16)
    r = jnp.where(is_lat, rl_ref[...], rc_ref[...])
    mixed = (jnp.dot(s5o, ws_ref[...], preferred_element_type=F32)
             + jnp.dot(r, wr_ref[...], preferred_element_type=F32))
    gate1 = m_ref[:, 2 * d:3 * d]
    x = x_ref[...] + gate1 * mixed
    xo_ref[...] = x
    ms = jnp.mean(x * x, axis=-1, keepdims=True)
    f = x * lax.rsqrt(ms + NORM_EPS) * g2_ref[...]
    f = f * (1.0 + m_ref[:, 4 * d:5 * d]) + m_ref[:, 3 * d:4 * d]
    tm = f.shape[0]
    for i in range(ROW_TILES):
        fu_ref[pl.ds(i, tm, stride=ROW_TILES), :] = f[:, 128 * i:128 * (i + 1)]
    logits_t = lax.dot_general(rw_ref[...], f, (((1,), (1,)), ((), ())), precision=HIGHEST,
                               preferred_element_type=F32)
    idx, wts = _route(logits_t, rb_ref[...])
    ei_ref[...] = idx
    ew_ref[...] = wts

    @pl.when(pl.program_id(0) == 0)
    def _():
        run_ref[...] = jnp.zeros_like(run_ref)

    e_iota = lax.broadcasted_iota(jnp.int32, (N_EXPERTS, tm), 0)
    tri = jnp.where(lax.broadcasted_iota(jnp.int32, (tm, tm), 0) <= lax.broadcasted_iota(jnp.int32, (tm, tm), 1),
                    1.0, 0.0).astype(BF16)
    hit0 = e_iota == idx[0:1, :]
    hit1 = e_iota == idx[1:2, :]
    p0 = jnp.dot(jnp.where(hit0, 1.0, 0.0).astype(BF16), tri, preferred_element_type=F32)
    p1 = jnp.dot(jnp.where(hit1, 1.0, 0.0).astype(BF16), tri, preferred_element_type=F32)
    tot0 = p0[:, tm - 1:tm]
    tot1 = p1[:, tm - 1:tm]
    before = run_ref[...]
    r0 = jnp.sum(jnp.where(hit0, before + p0, 0.0), axis=0, keepdims=True) - 1.0
    r1 = jnp.sum(jnp.where(hit1, before + tot0 + p1, 0.0), axis=0, keepdims=True) - 1.0
    rank_ref[...] = jnp.concatenate([r0, r1], axis=0).astype(jnp.int32)
    after = before + tot0 + tot1
    run_ref[...] = after
    cnt_ref[...] = after.astype(jnp.int32)


def _mix(ysl, ysc, pu, rl, rc, x, mods_cur, d_skip, glu_w, glu_b, w_s, w_r, g2, rw_t, rb, *, n_lat,
         rows_per_batch):
    t, d = x.shape
    sw = pu.shape[1]
    tm = MIX_ROW_BLOCK
    lat_blocks = n_lat // tm
    per_b = rows_per_batch // tm
    ctx_row = n_lat // rows_per_batch
    const = lambda i: (0, 0)
    row = lambda i: (i, 0)
    lat_row = lambda i: (jnp.minimum(i, lat_blocks - 1), 0)
    ctx_rowblk = lambda i: (jnp.maximum(i - lat_blocks, 0), 0)
    once = pl.Buffered(1)
    kern = functools.partial(_mix_kernel, lat_blocks=lat_blocks, d=d)
    return pl.pallas_call(
        kern,
        out_shape=(jax.ShapeDtypeStruct((t, d), F32), jax.ShapeDtypeStruct((t * ROW_TILES, 128), F32),
                   jax.ShapeDtypeStruct((TOP_K, t), jnp.int32), jax.ShapeDtypeStruct((TOP_K, t), F32),
                   jax.ShapeDtypeStruct((TOP_K, t), jnp.int32), jax.ShapeDtypeStruct((N_EXPERTS, 1), jnp.int32)),
        grid=(t // tm,),
        scratch_shapes=[pltpu.VMEM((N_EXPERTS, 1), F32)],
        in_specs=[pl.BlockSpec((tm, sw), lat_row),
                  pl.BlockSpec((tm, sw), ctx_rowblk, pipeline_mode=once),
                  pl.BlockSpec((tm, sw), row),
                  pl.BlockSpec((tm, sw), lat_row),
                  pl.BlockSpec((tm, sw), ctx_rowblk, pipeline_mode=once),
                  pl.BlockSpec((tm, d), row),
                  pl.BlockSpec((None, 1, 6 * d), lambda i: (jnp.where(i < lat_blocks, i // per_b, ctx_row), 0, 0)),
                  pl.BlockSpec((1, sw), const),
                  pl.BlockSpec((sw, sw), const, pipeline_mode=once),
                  pl.BlockSpec((1, sw), const),
                  pl.BlockSpec((sw, d), const, pipeline_mode=once),
                  pl.BlockSpec((d - sw, d), const, pipeline_mode=once),
                  pl.BlockSpec((1, d), const),
                  pl.BlockSpec((N_EXPERTS, d), const),
                  pl.BlockSpec((N_EXPERTS, 1), const)],
        out_specs=(pl.BlockSpec((tm, d), row), pl.BlockSpec((tm * ROW_TILES, 128), row),
                   pl.BlockSpec((TOP_K, tm), lambda i: (0, i)), pl.BlockSpec((TOP_K, tm), lambda i: (0, i)),
                   pl.BlockSpec((TOP_K, tm), lambda i: (0, i)), pl.BlockSpec((N_EXPERTS, 1), const)),
        compiler_params=_cparams(("arbitrary",), 58),
        name="mix",
    )(ysl, ysc, pu, rl, rc, x, mods_cur, d_skip, glu_w, glu_b, w_s, w_r, g2, rw_t, rb)


def _dispatch_plan(eidx, rank, counts):
    bm = MOE_ROWS
    n_assign = eidx.size
    nblk = -(-n_assign // bm) + N_EXPERTS
    counts = counts.reshape(N_EXPERTS)
    padded = (counts + bm - 1) // bm * bm
    pend = jnp.cumsum(padded)
    pstart = pend - padded
    experts = jnp.arange(N_EXPERTS, dtype=jnp.int32)[:, None, None]
    dest = rank + jnp.sum(jnp.where(eidx[None] == experts, pstart[:, None, None], 0), axis=0)
    blk_start = jnp.arange(nblk, dtype=jnp.int32) * bm
    bexp = jnp.minimum(jnp.sum((pend[None, :] <= blk_start[:, None]).astype(jnp.int32), axis=1), N_EXPERTS - 1)
    nused = (pend[-1] // bm).astype(jnp.int32).reshape(1)
    return dest.astype(jnp.int32), bexp.astype(jnp.int32), nused, counts.astype(jnp.int32), pstart.astype(jnp.int32)


def _dispatch_kernel(cnt_ref, pst_ref, nused_ref, dest_ref, fu_hbm, xs_hbm, zrow, zblk, sem, zsem, *,
                     tb, bm, nblk):
    i = pl.program_id(0)
    rpt = ROW_TILES

    def body(t8, _):
        for u in range(DMA_UNROLL):
            tt = t8 * DMA_UNROLL + u
            src = fu_hbm.at[pl.ds(pl.multiple_of((i * tb + tt) * rpt, rpt), rpt)]
            for k in range(TOP_K):
                dst = xs_hbm.at[pl.ds(pl.multiple_of(dest_ref[k, tt] * rpt, rpt), rpt)]
                pltpu.make_async_copy(src, dst, sem).start()
        return 0

    lax.fori_loop(0, tb // DMA_UNROLL, body, 0)

    @pl.when(i == 0)
    def _():
        zrow[...] = jnp.zeros_like(zrow)
        zblk[...] = jnp.zeros_like(zblk)

        def pad_rows(e, _):
            first = pst_ref[e] + cnt_ref[e]
            npad = (bm - cnt_ref[e] % bm) % bm

            def start(r, _):
                pltpu.make_async_copy(zrow, xs_hbm.at[pl.ds(pl.multiple_of((first + r) * rpt, rpt), rpt)],
                                      zsem).start()
                return 0

            def wait(r, _):
                pltpu.make_async_copy(zrow, xs_hbm.at[pl.ds(0, rpt)], zsem).wait()
                return 0

            lax.fori_loop(0, npad, start, 0)
            lax.fori_loop(0, npad, wait, 0)
            return 0

        lax.fori_loop(0, N_EXPERTS, pad_rows, 0)

        def pad_block(j, _):
            cp = pltpu.make_async_copy(zblk, xs_hbm.at[pl.ds(pl.multiple_of(j * bm * rpt, bm * rpt), bm * rpt)], zsem)
            cp.start()
            cp.wait()
            return 0

        lax.fori_loop(nused_ref[0], nblk, pad_block, 0)

    n_rows = TOP_K * tb * rpt
    pltpu.make_async_copy(fu_hbm.at[pl.ds(0, n_rows)], xs_hbm.at[pl.ds(0, n_rows)], sem).wait()


def _dispatch(fu, dest, counts, pstart, nused, *, nblk):
    n_tok = dest.shape[1]
    bm = MOE_ROWS
    tb = 1024
    assert n_tok % tb == 0
    kern = functools.partial(_dispatch_kernel, tb=tb, bm=bm, nblk=nblk)
    grid_spec = pltpu.PrefetchScalarGridSpec(
        num_scalar_prefetch=3,
        grid=(n_tok // tb,),
        in_specs=[pl.BlockSpec((TOP_K, tb), lambda i, c, p, n: (0, i), memory_space=pltpu.SMEM),
                  pl.BlockSpec(memory_space=pl.ANY)],
        out_specs=pl.BlockSpec(memory_space=pl.ANY),
        scratch_shapes=[pltpu.VMEM((ROW_TILES, 128), F32),
                        pltpu.VMEM((bm * ROW_TILES, 128), F32),
                        pltpu.SemaphoreType.DMA(()), pltpu.SemaphoreType.DMA(())],
    )
    return pl.pallas_call(
        kern,
        out_shape=jax.ShapeDtypeStruct((nblk * bm * ROW_TILES, 128), F32),
        grid_spec=grid_spec,
        compiler_params=_cparams(("arbitrary",), 16),
        name="dispatch",
    )(counts, pstart, nused, dest, fu)


def _expert_kernel(bexp_ref, nused_ref, xs_ref, wg_ref, wu_ref, wd_ref, ys_ref, *, bm):
    @pl.when(pl.program_id(0) < nused_ref[0])
    def _():
        xb = jnp.concatenate([xs_ref[pl.ds(i, bm, stride=ROW_TILES), :].astype(BF16) for i in range(ROW_TILES)],
                             axis=1)
        gt = jnp.dot(xb, wg_ref[...], preferred_element_type=F32)
        up = jnp.dot(xb, wu_ref[...], preferred_element_type=F32)
        hmid = (gt * jax.nn.sigmoid(gt) * up).astype(BF16)
        y = jnp.dot(hmid, wd_ref[...], preferred_element_type=F32)
        for j in range(ROW_TILES):
            ys_ref[pl.ds(j, bm, stride=ROW_TILES), :] = y[:, 128 * j:128 * (j + 1)]

    @pl.when(pl.program_id(0) >= nused_ref[0])
    def _():
        ys_ref[...] = jnp.zeros_like(ys_ref)


def _experts(xs, bexp, nused, wg, wu, wd):
    bm = MOE_ROWS
    nblk = bexp.shape[0]
    d, ff = wg.shape[1], wg.shape[2]
    grid_spec = pltpu.PrefetchScalarGridSpec(
        num_scalar_prefetch=2,
        grid=(nblk,),
        in_specs=[pl.BlockSpec((bm * ROW_TILES, 128), lambda j, be, nu: (j, 0)),
                  pl.BlockSpec((None, d, ff), lambda j, be, nu: (be[j], 0, 0)),
                  pl.BlockSpec((None, d, ff), lambda j, be, nu: (be[j], 0, 0)),
                  pl.BlockSpec((None, ff, d), lambda j, be, nu: (be[j], 0, 0))],
        out_specs=pl.BlockSpec((bm * ROW_TILES, 128), lambda j, be, nu: (j, 0)),
    )
    return pl.pallas_call(
        functools.partial(_expert_kernel, bm=bm),
        out_shape=jax.ShapeDtypeStruct((nblk * bm * ROW_TILES, 128), F32),
        grid_spec=grid_spec,
        compiler_params=_cparams(("arbitrary",), 52),
        name="experts",
    )(bexp, nused, xs, wg, wu, wd)


def _rope_tables(n_lat):
    n_rows = n_lat // GRID_W
    row = jnp.broadcast_to(jnp.arange(n_rows, dtype=F32)[:, None], (n_rows, GRID_W)).reshape(-1)
    col = jnp.broadcast_to(jnp.arange(GRID_W, dtype=F32)[None, :], (n_rows, GRID_W)).reshape(-1)
    quarter = RET_HEAD_DIM // 4
    freqs = ROPE_BASE ** (-jnp.arange(quarter, dtype=F32) / quarter)
    ang = jnp.concatenate([row[:, None] * freqs, col[:, None] * freqs], axis=-1)
    return jnp.cos(ang), jnp.sin(ang)


def kernel(x, c, ctx, c_ctx, mod_w, mod_b, norm1_g, norm2_g, w_in, s5_a_re, s5_a_im, s5_log_dt, s5_b_re, s5_b_im,
           s5_c_re, s5_c_im, s5_d, glu_w, glu_b, ret_decay_raw, w_out, router_w, router_b, exp_w_gate, exp_w_up,
           exp_w_down, final_norm_g):
    bsz, n_lat, d = x.shape
    n_ctx = ctx.shape[1]
    depth = mod_w.shape[0]
    s5w = s5_d.shape[1]
    t = bsz * (n_lat + n_ctx)
    lat_total = bsz * n_lat

    xs = jnp.concatenate([x.reshape(lat_total, d), ctx.reshape(bsz * n_ctx, d)], axis=0)
    c8 = jnp.zeros((8, d), F32).at[:bsz].set(c).at[bsz].set(c_ctx)
    mods = _mods(c8, mod_w, mod_b).reshape(depth, 8, 1, 6 * d)
    cos, sin = _rope_tables(n_lat)
    rw_t = router_w.T
    rb = router_b.reshape(N_EXPERTS, 1)

    moe = None
    for layer in range(depth):
        xs, h = _prenorm(xs, moe, mods[layer - 1] if layer else None, mods[layer], norm1_g[layer],
                         n_lat=lat_total, rows_per_batch=n_lat, final=False)
        pu, pr = _in_proj(h, w_in[layer].astype(BF16), s5w)

        w1, q, a16 = _s5_weights(s5_a_re[layer], s5_a_im[layer], s5_log_dt[layer], s5_b_re[layer],
                                 s5_b_im[layer], s5_c_re[layer], s5_c_im[layer])
        ysl, ysc = _s5_scan(pu, w1, q, a16, bsz=bsz, n_lat=n_lat, n_ctx=n_ctx)

        log_gamma = jnp.log1p(-jnp.exp2(-ret_decay_raw[layer].astype(F32)))
        rl, rc = _retention(pr, log_gamma, cos, sin, bsz=bsz, n_lat=n_lat, n_ctx=n_ctx, col0=0)

        w_o = w_out[layer].astype(BF16)
        xs, fu, eidx, ew, rank, counts = _mix(
            ysl, ysc, pu, rl, rc, xs, mods[layer], s5_d[layer].reshape(1, s5w), glu_w[layer].astype(BF16),
            glu_b[layer].reshape(1, s5w), w_o[:s5w], w_o[s5w:], norm2_g[layer].reshape(1, d), rw_t, rb,
            n_lat=lat_total, rows_per_batch=n_lat)

        dest, bexp, nused, counts, pstart = _dispatch_plan(eidx, rank, counts)
        xsort = _dispatch(fu, dest, counts, pstart, nused, nblk=bexp.shape[0])
        ys = _experts(xsort, bexp, nused, exp_w_gate[layer].astype(BF16), exp_w_up[layer].astype(BF16),
                      exp_w_down[layer].astype(BF16))
        moe = (ys, dest, ew.T)

    out = _prenorm(xs, moe, mods[depth - 1], mods[depth - 1], final_norm_g,
                   n_lat=lat_total, rows_per_batch=n_lat, final=True)
    return out.reshape(bsz, n_lat, d)
```

```python
import functools
import math

import jax
import jax.numpy as jnp
import numpy as np
from jax import lax
from jax.experimental import pallas as pl
from jax.experimental.pallas import tpu as pltpu

F32 = jnp.float32
BF16 = jnp.bfloat16
HIGHEST = lax.Precision.HIGHEST

GRID_W = 64
S5_GROUP = 16
S5_STATE = 64
S5_STEP = 16
RET_HEAD_DIM = 256
RET_BLOCK = 256
ROPE_BASE = 10000.0
N_EXPERTS = 16
N_EXPERT_GROUPS = 4
EXPERTS_PER_GROUP = 4
TOP_K = 2
NORM_EPS = 1e-6

ROW_BLOCK = 256
MIX_ROW_BLOCK = 512
MM_ROW_BLOCK = 1024
MM_COL_BLOCK = 1024
MOE_ROWS = 256
ROW_TILES = 16
DMA_UNROLL = 8
S5_TILE_GROUPS = 128 // S5_GROUP
MIB = 1024 * 1024


def _cparams(sem, vmem_mib):
    return pltpu.CompilerParams(dimension_semantics=sem, vmem_limit_bytes=int(vmem_mib * MIB))


def _mods_kernel(c_ref, w_ref, b_ref, o_ref):
    cv = c_ref[...]
    act = cv * jax.nn.sigmoid(cv)
    o_ref[0] = jnp.dot(act, w_ref[0], precision=HIGHEST, preferred_element_type=F32) + b_ref[0]


def _mods(c8, mod_w, mod_b):
    depth, d, n = mod_w.shape
    tn = 1024
    return pl.pallas_call(
        _mods_kernel,
        out_shape=jax.ShapeDtypeStruct((depth, 8, n), F32),
        grid=(depth, n // tn),
        in_specs=[pl.BlockSpec((8, d), lambda l, j: (0, 0)),
                  pl.BlockSpec((1, d, tn), lambda l, j: (l, 0, j)),
                  pl.BlockSpec((1, 1, tn), lambda l, j: (l, 0, j))],
        out_specs=pl.BlockSpec((1, 8, tn), lambda l, j: (l, 0, j)),
        compiler_params=_cparams(("arbitrary", "arbitrary"), 40),
        name="adaln_mods",
    )(c8, mod_w, mod_b.reshape(depth, 1, n))


def _prenorm_kernel(*refs, combine, final, d, tm, nblk):
    if combine:
        (dest0_ref, destn_ref, x_ref, ys_hbm, wt_ref, mprev_ref, mcur_ref, g_ref), rest = refs[:8], refs[8:]
        outs, (ybuf, sem) = rest[:-2], rest[-2:]
    else:
        (x_ref, mcur_ref, g_ref), outs = refs[:3], refs[3:]
    x = x_ref[...]
    if combine:
        i = pl.program_id(0)
        slot = i % 2
        rpt = ROW_TILES
        half = tm * rpt
        per_slot = TOP_K * half

        def gather(dref, s):
            def body(t8, _):
                for u in range(DMA_UNROLL):
                    tt = t8 * DMA_UNROLL + u
                    for k in range(TOP_K):
                        src = ys_hbm.at[pl.ds(pl.multiple_of(dref[k, tt] * rpt, rpt), rpt)]
                        dst = ybuf.at[pl.ds(pl.multiple_of(s * per_slot + k * half + tt * rpt, rpt), rpt)]
                        pltpu.make_async_copy(src, dst, sem.at[s]).start()
                return 0
            lax.fori_loop(0, tm // DMA_UNROLL, body, 0)

        @pl.when(i == 0)
        def _():
            gather(dest0_ref, 0)

        @pl.when(i + 1 < nblk)
        def _():
            gather(destn_ref, 1 - slot)

        base = pl.multiple_of(slot * per_slot, per_slot)
        pltpu.make_async_copy(ys_hbm.at[pl.ds(0, per_slot)], ybuf.at[pl.ds(base, per_slot)], sem.at[slot]).wait()

        def rows_of(k):
            return jnp.concatenate([ybuf[pl.ds(base + k * half + j, tm, stride=rpt), :] for j in range(rpt)], axis=1)

        gate2 = mprev_ref[:, 5 * d:6 * d]
        x = x + gate2 * (wt_ref[:, 0:1] * rows_of(0) + wt_ref[:, 1:2] * rows_of(1))
    ms = jnp.mean(x * x, axis=-1, keepdims=True)
    xn = x * lax.rsqrt(ms + NORM_EPS) * g_ref[...]
    if final:
        outs[0][...] = xn
    else:
        outs[0][...] = x
        shift = mcur_ref[:, 0:d]
        scale = mcur_ref[:, d:2 * d]
        outs[1][...] = (xn * (1.0 + scale) + shift).astype(BF16)


def _prenorm(x, moe, mods_prev, mods_cur, gain, *, n_lat, rows_per_batch, final):
    t, d = x.shape
    tm = ROW_BLOCK
    combine = moe is not None
    nblk = (n_lat if final else t) // tm
    lat_blocks = n_lat // tm
    per_b = rows_per_batch // tm
    ctx_row = n_lat // rows_per_batch

    def mod_idx(i):
        return (jnp.where(i < lat_blocks, i // per_b, ctx_row), 0, 0)

    row_spec = pl.BlockSpec((tm, d), lambda i: (i, 0))
    mod_spec = pl.BlockSpec((None, 1, 6 * d), mod_idx)
    g_spec = pl.BlockSpec((1, d), lambda i: (0, 0))
    ins, specs, scratch = [], [], []
    if combine:
        ys, dest, wt = moe
        ins += [dest, dest]
        specs += [pl.BlockSpec((TOP_K, tm), lambda i: (0, 0), memory_space=pltpu.SMEM),
                  pl.BlockSpec((TOP_K, tm), lambda i: (0, jnp.minimum(i + 1, nblk - 1)), memory_space=pltpu.SMEM)]
    ins.append(x)
    specs.append(row_spec)
    if combine:
        ins += [ys, wt, mods_prev]
        specs += [pl.BlockSpec(memory_space=pl.ANY), pl.BlockSpec((tm, TOP_K), lambda i: (i, 0)), mod_spec]
        scratch = [pltpu.VMEM((2 * TOP_K * tm * ROW_TILES, 128), F32), pltpu.SemaphoreType.DMA((2,))]
    ins += [mods_cur, gain.reshape(1, d)]
    specs += [mod_spec, g_spec]
    if final:
        out_shape = jax.ShapeDtypeStruct((n_lat, d), F32)
        out_specs = row_spec
    else:
        out_shape = (jax.ShapeDtypeStruct((t, d), F32), jax.ShapeDtypeStruct((t, d), BF16))
        out_specs = (row_spec, row_spec)
    return pl.pallas_call(
        functools.partial(_prenorm_kernel, combine=combine, final=final, d=d, tm=tm, nblk=nblk),
        out_shape=out_shape, grid=(nblk,), in_specs=specs, out_specs=out_specs, scratch_shapes=scratch,
        compiler_params=_cparams(("arbitrary",), 40),
        name="prenorm_final" if final else ("prenorm_combine" if combine else "prenorm"),
    )(*ins)


def _inproj_kernel(h_ref, w_ref, pu_ref, pr_ref):
    acc = jnp.dot(h_ref[...], w_ref[...], preferred_element_type=F32)

    @pl.when(pl.program_id(1) == 0)
    def _():
        pu_ref[...] = acc

    @pl.when(pl.program_id(1) > 0)
    def _():
        pr_ref[...] = acc.astype(pr_ref.dtype)


def _in_proj(h, w, layer, s5w):
    t, k = h.shape
    n = w.shape[2]
    tm, tn = MM_ROW_BLOCK, MM_COL_BLOCK
    assert s5w == tn
    return pl.pallas_call(
        _inproj_kernel,
        out_shape=(jax.ShapeDtypeStruct((t, s5w), F32), jax.ShapeDtypeStruct((t, n - s5w), BF16)),
        grid=(t // tm, n // tn),
        in_specs=[pl.BlockSpec((tm, k), lambda i, j: (i, 0)),
                  pl.BlockSpec((None, k, tn), lambda i, j: (layer, 0, j))],
        out_specs=(pl.BlockSpec((tm, tn), lambda i, j: (i, 0)),
                   pl.BlockSpec((tm, tn), lambda i, j: (i, jnp.maximum(j - 1, 0)))),
        compiler_params=_cparams(("arbitrary", "arbitrary"), 40),
        name="in_proj",
    )(h, w)


def _s5_weights(a_re, a_im, log_dt, b_re, b_im, c_re, c_im):
    ns = S5_STEP
    a_re, a_im = a_re.astype(F32), a_im.astype(F32)
    dt = jnp.exp(log_dt.astype(F32))[..., None]
    lam_re, lam_im = dt * a_re, dt * a_im
    tau = jnp.arange(ns + 1, dtype=F32)[:, None, None, None]
    mag = jnp.exp(tau * lam_re[None])
    p_re, p_im = mag * jnp.cos(tau * lam_im[None]), mag * jnp.sin(tau * lam_im[None])
    x, y = p_re[1] - 1.0, p_im[1]
    den = a_re * a_re + a_im * a_im
    k_re, k_im = (x * a_re + y * a_im) / den, (y * a_re - x * a_im) / den
    b_re, b_im = b_re.astype(F32), b_im.astype(F32)
    bb_re = k_re[..., None] * b_re - k_im[..., None] * b_im
    bb_im = k_re[..., None] * b_im + k_im[..., None] * b_re
    c_re, c_im = c_re.astype(F32), c_im.astype(F32)
    cp_re = c_re[None] * p_re[:, :, :, None, :] - c_im[None] * p_im[:, :, :, None, :]
    cp_im = c_re[None] * p_im[:, :, :, None, :] + c_im[None] * p_re[:, :, :, None, :]
    kmat = jnp.einsum('tdgon,dgni->tdgoi', jnp.concatenate([cp_re, -cp_im], axis=-1),
                      jnp.concatenate([bb_re, bb_im], axis=-2))
    s_idx = jnp.arange(ns)[:, None]
    t_idx = jnp.arange(ns)[None, :]
    lag = t_idx - s_idx
    kf = jnp.where((lag >= 0)[:, :, None, None, None], kmat[jnp.clip(lag, 0, ns - 1), 0], 0.0)
    kb = jnp.where((lag <= 0)[:, :, None, None, None], kmat[jnp.clip(-lag, 0, ns - 1), 1], 0.0)
    g = a_re.shape[1]
    m = jnp.transpose(kf + kb, (2, 0, 4, 1, 3)).reshape(g, ns * S5_GROUP, ns * S5_GROUP)

    def inject(p_r, p_i, d):
        r = p_r[:, :, :, None] * bb_re[d][None] - p_i[:, :, :, None] * bb_im[d][None]
        i = p_r[:, :, :, None] * bb_im[d][None] + p_i[:, :, :, None] * bb_re[d][None]
        tr = lambda z: jnp.transpose(z, (1, 0, 3, 2)).reshape(g, ns * S5_GROUP, S5_STATE)
        return tr(r), tr(i)

    rev = jnp.arange(ns - 1, -1, -1)
    pf_re, pf_im = inject(p_re[rev, 0], p_im[rev, 0], 0)
    pb_re, pb_im = inject(p_re[:ns, 1], p_im[:ns, 1], 1)
    w1 = jnp.concatenate([m, pf_re, pb_re, pf_im, pb_im], axis=-1)

    def carry(cp, taus, d):
        return jnp.transpose(cp[taus, d], (1, 3, 0, 2)).reshape(g, S5_STATE, ns * S5_GROUP)

    up = jnp.arange(1, ns + 1)
    down = jnp.arange(ns, 0, -1)
    q = jnp.concatenate([carry(cp_re, up, 0), carry(cp_re, down, 1),
                         -carry(cp_im, up, 0), -carry(cp_im, down, 1)], axis=1)
    a16 = jnp.stack([jnp.concatenate([p_re[ns, 0], p_re[ns, 1]], axis=-1),
                     jnp.concatenate([p_im[ns, 0], p_im[ns, 1]], axis=-1)], axis=1)
    return w1.astype(BF16), q.astype(BF16), a16


def _group_transpose(xs):
    ng = len(xs)
    grp = lax.broadcasted_iota(jnp.int32, xs[0].shape, 1) // S5_GROUP
    rolled = []
    for j in range(ng):
        y = xs[j % ng]
        for k in range(1, ng):
            y = jnp.where(grp == k, xs[(k + j) % ng], y)
        rolled.append(y if j == 0 else pltpu.roll(y, S5_GROUP * j, 1))
    outs = []
    for a in range(ng):
        o = rolled[(-a) % ng]
        for b in range(1, ng):
            o = jnp.where(grp == b, rolled[(b - a) % ng], o)
        outs.append(o)
    return outs


def _s5_kernel(ul_ref, uc_ref, w1_ref, q_ref, a_ref, yl_ref, yc_ref, uflat, st_re, st_im, yfl, *,
               lat_chunks, ctx_chunks):
    ng = S5_TILE_GROUPS
    width = S5_STEP * S5_GROUP
    n_chunks = lat_chunks + ctx_chunks
    sub = 32

    def flatten(src_ref, c0, row0, n):
        for hh in range(2):
            xs = [src_ref[pl.ds(row0 + hh * 8 + t, n, stride=S5_STEP), :] for t in range(8)]
            outs = _group_transpose(xs)
            for k in range(ng):
                uflat[k, pl.ds(c0, n), hh * 128:(hh + 1) * 128] = outs[k].astype(BF16)

    def flat_body(i, _):
        c0 = pl.multiple_of(i * sub, sub)
        flatten(ul_ref, c0, c0 * S5_STEP, sub)
        return 0

    lax.fori_loop(0, lat_chunks // sub, flat_body, 0)
    flatten(uc_ref, lat_chunks, 0, ctx_chunks)

    for k in range(ng):
        z = jnp.dot(uflat[k], w1_ref[k], preferred_element_type=F32)
        yfl[k] = z[:, :width]
        st_re[pl.ds(k, n_chunks, stride=ng), :] = z[:, width:width + 128]
        st_im[pl.ds(k, n_chunks, stride=ng), :] = z[:, width + 128:]

    lane = lax.broadcasted_iota(jnp.int32, (ng, 128), 1)
    fwd_lane = lane < S5_STATE
    a_re = a_ref[0]
    a_im = a_ref[1]

    def make_body(base, n):
        def body(i, carry):
            h_re, h_im = carry
            rf = pl.multiple_of((base + i) * ng, ng)
            rb = pl.multiple_of((base + n - 1 - i) * ng, ng)
            tf_re, tf_im = st_re[pl.ds(rf, ng), :], st_im[pl.ds(rf, ng), :]
            tb_re, tb_im = st_re[pl.ds(rb, ng), :], st_im[pl.ds(rb, ng), :]
            t_re = jnp.where(fwd_lane, tf_re, tb_re)
            t_im = jnp.where(fwd_lane, tf_im, tb_im)
            n_re = a_re * h_re - a_im * h_im + t_re
            n_im = a_re * h_im + a_im * h_re + t_im
            st_re[pl.ds(rf, ng), :] = jnp.where(fwd_lane, h_re, tf_re)
            st_im[pl.ds(rf, ng), :] = jnp.where(fwd_lane, h_im, tf_im)
            st_re[pl.ds(rb, ng), :] = jnp.where(fwd_lane, tb_re, h_re)
            st_im[pl.ds(rb, ng), :] = jnp.where(fwd_lane, tb_im, h_im)
            return n_re, n_im
        return body

    carry = (jnp.zeros((ng, 128), F32), jnp.zeros((ng, 128), F32))
    carry = lax.fori_loop(0, ctx_chunks, make_body(lat_chunks, ctx_chunks), carry)
    carry = lax.fori_loop(0, lat_chunks, make_body(0, lat_chunks), carry)

    for k in range(ng):
        hk = jnp.concatenate([st_re[pl.ds(k, n_chunks, stride=ng), :], st_im[pl.ds(k, n_chunks, stride=ng), :]],
                             axis=1).astype(BF16)
        yfl[k] = yfl[k] + jnp.dot(hk, q_ref[k], preferred_element_type=F32)

    def unflatten(dst_ref, c0, row0, n):
        for hh in range(2):
            vs = [yfl[k, pl.ds(c0, n), hh * 128:(hh + 1) * 128] for k in range(ng)]
            ts = _group_transpose(vs)
            for t in range(8):
                dst_ref[pl.ds(row0 + hh * 8 + t, n, stride=S5_STEP), :] = ts[t]

    def unflat_body(i, _):
        c0 = pl.multiple_of(i * sub, sub)
        unflatten(yl_ref, c0, c0 * S5_STEP, sub)
        return 0

    lax.fori_loop(0, lat_chunks // sub, unflat_body, 0)
    unflatten(yc_ref, lat_chunks, 0, ctx_chunks)


def _s5_scan(pu, w1, q, a16, *, bsz, n_lat, n_ctx):
    g = w1.shape[0]
    ng = S5_TILE_GROUPS
    width = S5_STEP * S5_GROUP
    lat_chunks, ctx_chunks = n_lat // S5_STEP, n_ctx // S5_STEP
    assert lat_chunks % 32 == 0 and ctx_chunks % 2 == 0 and ctx_chunks % 16 == 0
    n_chunks = lat_chunks + ctx_chunks
    ctx_blk0 = bsz * n_lat // n_ctx
    a_t = jnp.transpose(a16.reshape(g // ng, ng, 2, 128), (0, 2, 1, 3))
    kern = functools.partial(_s5_kernel, lat_chunks=lat_chunks, ctx_chunks=ctx_chunks)
    return pl.pallas_call(
        kern,
        out_shape=(jax.ShapeDtypeStruct((bsz * n_lat, g * S5_GROUP), F32),
                   jax.ShapeDtypeStruct((bsz * n_ctx, g * S5_GROUP), F32)),
        grid=(g // ng, bsz),
        in_specs=[pl.BlockSpec((n_lat, 128), lambda i, b: (b, i)),
                  pl.BlockSpec((n_ctx, 128), lambda i, b: (ctx_blk0 + b, i)),
                  pl.BlockSpec((ng, width, 2 * width), lambda i, b: (i, 0, 0)),
                  pl.BlockSpec((ng, width, width), lambda i, b: (i, 0, 0)),
                  pl.BlockSpec((None, 2, ng, 128), lambda i, b: (i, 0, 0, 0))],
        out_specs=(pl.BlockSpec((n_lat, 128), lambda i, b: (b, i)),
                   pl.BlockSpec((n_ctx, 128), lambda i, b: (b, i))),
        scratch_shapes=[pltpu.VMEM((ng, n_chunks, width), BF16),
                        pltpu.VMEM((n_chunks * ng, 128), F32),
                        pltpu.VMEM((n_chunks * ng, 128), F32),
                        pltpu.VMEM((ng, n_chunks, width), F32)],
        compiler_params=_cparams(("arbitrary", "arbitrary"), 40),
        name="s5_scan",
    )(pu, pu, w1, q, a_t)


def _ret_kernel(lg_ref, ql_ref, kl_ref, vl_ref, gl_ref, qc_ref, kc_ref, vc_ref, gc_ref, cos_ref, sin_ref,
                ol_ref, oc_ref, qs, ks, vs, acc, sf, sb, dm, *, n_ctx, n_lat):
    c = RET_BLOCK
    half = RET_HEAD_DIM // 2
    head = pl.program_id(1)
    lgf = lg_ref[0, head]
    lgb = lg_ref[1, head]
    qscale = RET_HEAD_DIM ** -0.5
    ctx_rows = n_ctx * c

    ii = lax.broadcasted_iota(jnp.int32, (c, c), 0)
    jj = lax.broadcasted_iota(jnp.int32, (c, c), 1)
    rel = (ii - jj).astype(F32)
    dsum = (jnp.where(rel >= 0, jnp.exp(lgf * jnp.maximum(rel, 0.0)), 0.0)
            + jnp.where(rel <= 0, jnp.exp(lgb * jnp.maximum(-rel, 0.0)), 0.0))
    pos = lax.broadcasted_iota(jnp.int32, (c, 1), 0).astype(F32)
    qdec_f = jnp.exp(lgf * (pos + 1.0))
    kdec_f = jnp.exp(lgf * (c - 1.0 - pos))
    qdec_b = jnp.exp(lgb * (c - pos))
    kdec_b = jnp.exp(lgb * pos)
    cg_f = jnp.exp(lgf * c)
    cg_b = jnp.exp(lgb * c)

    qs[0:ctx_rows, :] = (qc_ref[...].astype(F32) * qscale).astype(BF16)
    ks[0:ctx_rows, :] = kc_ref[...]
    vs[0:ctx_rows, :] = vc_ref[...]

    def rope_body(n, _):
        r = pl.multiple_of(n * c, c)
        cs = cos_ref[pl.ds(r, c), :]
        sn = sin_ref[pl.ds(r, c), :]

        def rot(t):
            t1, t2 = t[:, :half], t[:, half:]
            return jnp.concatenate([t1 * cs - t2 * sn, t2 * cs + t1 * sn], axis=1)

        dst = pl.multiple_of(ctx_rows + r, c)
        qs[pl.ds(dst, c), :] = (rot(ql_ref[pl.ds(r, c), :].astype(F32)) * qscale).astype(BF16)
        ks[pl.ds(dst, c), :] = rot(kl_ref[pl.ds(r, c), :].astype(F32)).astype(BF16)
        vs[pl.ds(dst, c), :] = vl_ref[pl.ds(r, c), :]
        return 0

    lax.fori_loop(0, n_lat, rope_body, 0)

    dm[...] = dsum
    sf[...] = jnp.zeros_like(sf)
    sb[...] = jnp.zeros_like(sb)

    def load(r):
        return qs[pl.ds(r, c), :], ks[pl.ds(r, c), :], vs[pl.ds(r, c), :]

    def intra_of(q, k, v):
        s = lax.dot_general(q, k, (((1,), (1,)), ((), ())), preferred_element_type=F32)
        return jnp.dot((s * dm[...]).astype(BF16), v, preferred_element_type=F32)

    def step(s_ref, q, k, v, qdec, kdec, cg):
        inter = jnp.dot(q, s_ref[...].astype(BF16), preferred_element_type=F32) * qdec
        kd = (k.astype(F32) * kdec).astype(BF16)
        s_ref[...] = cg * s_ref[...] + lax.dot_general(kd, v, (((0,), (0,)), ((), ())),
                                                       preferred_element_type=F32)
        return inter

    def finish(o, gate, out_ref, ro):
        o = o * lax.rsqrt(jnp.mean(o * o, axis=-1, keepdims=True) + NORM_EPS)
        out_ref[pl.ds(ro, c), :] = (o * (gate * jax.nn.sigmoid(gate))).astype(out_ref.dtype)

    fwd = (sf, qdec_f, kdec_f, cg_f)
    bwd = (sb, qdec_b, kdec_b, cg_b)

    for n in range(n_ctx):
        q, k, v = load(n * c)
        acc[n * c:(n + 1) * c, :] = intra_of(q, k, v) + step(fwd[0], q, k, v, *fwd[1:])
    for n in range(n_ctx - 1, -1, -1):
        q, k, v = load(n * c)
        o = acc[n * c:(n + 1) * c, :] + step(bwd[0], q, k, v, *bwd[1:])
        finish(o, gc_ref[n * c:(n + 1) * c, :].astype(F32), oc_ref, n * c)

    def first_visits(i, _):
        rf = pl.multiple_of(i * c, c)
        rb = pl.multiple_of((n_lat - 1 - i) * c, c)
        q, k, v = load(ctx_rows + rf)
        acc[pl.ds(ctx_rows + rf, c), :] = intra_of(q, k, v) + step(fwd[0], q, k, v, *fwd[1:])
        q, k, v = load(ctx_rows + rb)
        acc[pl.ds(ctx_rows + rb, c), :] = step(bwd[0], q, k, v, *bwd[1:])
        return 0

    def second_visits(i, _):
        rf = pl.multiple_of(i * c, c)
        rb = pl.multiple_of((n_lat - 1 - i) * c, c)
        q, k, v = load(ctx_rows + rf)
        o = acc[pl.ds(ctx_rows + rf, c), :] + intra_of(q, k, v) + step(fwd[0], q, k, v, *fwd[1:])
        finish(o, gl_ref[pl.ds(rf, c), :].astype(F32), ol_ref, rf)
        q, k, v = load(ctx_rows + rb)
        o = acc[pl.ds(ctx_rows + rb, c), :] + step(bwd[0], q, k, v, *bwd[1:])
        finish(o, gl_ref[pl.ds(rb, c), :].astype(F32), ol_ref, rb)
        return 0

    lax.fori_loop(0, n_lat // 2, first_visits, 0)
    lax.fori_loop(n_lat // 2, n_lat, second_visits, 0)


def _retention(p, log_gamma, cos, sin, *, bsz, n_lat, n_ctx, col0):
    t = p.shape[0]
    dh = RET_HEAD_DIM
    heads = (p.shape[1] - col0) // (4 * dh)
    qb, kb, vb, gb = (col0 // dh + i * heads for i in range(4))
    ctx_blk0 = bsz * n_lat // n_ctx

    def lat(cb):
        return pl.BlockSpec((n_lat, dh), lambda b, h: (b, cb + h))

    def ctx(cb):
        return pl.BlockSpec((n_ctx, dh), lambda b, h: (ctx_blk0 + b, cb + h))

    tab = pl.BlockSpec((n_lat, dh // 2), lambda b, h: (0, 0))
    assert n_ctx % RET_BLOCK == 0 and (n_lat // RET_BLOCK) % 2 == 0
    kern = functools.partial(_ret_kernel, n_ctx=n_ctx // RET_BLOCK, n_lat=n_lat // RET_BLOCK)
    seq = n_ctx + n_lat
    return pl.pallas_call(
        kern,
        out_shape=(jax.ShapeDtypeStruct((bsz * n_lat, heads * dh), BF16),
                   jax.ShapeDtypeStruct((bsz * n_ctx, heads * dh), BF16)),
        grid=(bsz, heads),
        in_specs=[pl.BlockSpec(memory_space=pltpu.SMEM),
                  lat(qb), lat(kb), lat(vb), lat(gb), ctx(qb), ctx(kb), ctx(vb), ctx(gb), tab, tab],
        out_specs=(pl.BlockSpec((n_lat, dh), lambda b, h: (b, h)),
                   pl.BlockSpec((n_ctx, dh), lambda b, h: (b, h))),
        scratch_shapes=[pltpu.VMEM((seq, dh), BF16), pltpu.VMEM((seq, dh), BF16), pltpu.VMEM((seq, dh), BF16),
                        pltpu.VMEM((seq, dh), F32), pltpu.VMEM((dh, dh), F32), pltpu.VMEM((dh, dh), F32),
                        pltpu.VMEM((RET_BLOCK, RET_BLOCK), F32)],
        compiler_params=_cparams(("arbitrary", "arbitrary"), 52),
        name="retention",
    )(log_gamma, p, p, p, p, p, p, p, p, cos, sin)


def _top2_sum(a, b, c, d):
    hi1, lo1 = jnp.maximum(a, b), jnp.minimum(a, b)
    hi2, lo2 = jnp.maximum(c, d), jnp.minimum(c, d)
    return jnp.maximum(hi1, hi2) + jnp.maximum(jnp.minimum(hi1, hi2), jnp.maximum(lo1, lo2))


def _route(logits_t, rb):
    sc = jax.nn.sigmoid(logits_t)
    biased = sc + rb
    per = EXPERTS_PER_GROUP
    brow = [biased[e:e + 1, :] for e in range(N_EXPERTS)]
    srow = [sc[e:e + 1, :] for e in range(N_EXPERTS)]
    gscore = [_top2_sum(*brow[per * g:per * g + per]) for g in range(N_EXPERT_GROUPS)]
    best_v = gscore[0]
    best_g = jnp.zeros_like(best_v, dtype=jnp.int32)
    for g in range(1, N_EXPERT_GROUPS):
        upd = gscore[g] > best_v
        best_v = jnp.where(upd, gscore[g], best_v)
        best_g = jnp.where(upd, g, best_g)
    vals, sels = [], []
    for j in range(per):
        v = brow[j]
        s = srow[j]
        for g in range(1, N_EXPERT_GROUPS):
            v = jnp.where(best_g == g, brow[per * g + j], v)
            s = jnp.where(best_g == g, srow[per * g + j], s)
        vals.append(v)
        sels.append(s)
    v1, i1, s1 = vals[0], jnp.zeros_like(best_g), sels[0]
    for j in range(1, per):
        upd = vals[j] > v1
        v1 = jnp.where(upd, vals[j], v1)
        s1 = jnp.where(upd, sels[j], s1)
        i1 = jnp.where(upd, j, i1)
    v2 = jnp.full_like(v1, -jnp.inf)
    i2, s2 = jnp.zeros_like(best_g), jnp.zeros_like(s1)
    for j in range(per):
        upd = (i1 != j) & (vals[j] > v2)
        v2 = jnp.where(upd, vals[j], v2)
        s2 = jnp.where(upd, sels[j], s2)
        i2 = jnp.where(upd, j, i2)
    tot = s1 + s2
    idx = jnp.concatenate([best_g * per + i1, best_g * per + i2], axis=0)
    wts = jnp.concatenate([s1 / tot, s2 / tot], axis=0)
    return idx, wts


def _mix_kernel(ysl_ref, ysc_ref, u_ref, rl_ref, rc_ref, x_ref, m_ref, dsk_ref, gw_ref, gb_ref, ws_ref, wr_ref,
                g2_ref, rw_ref, rb_ref, xo_ref, fu_ref, ei_ref, ew_ref, rank_ref, cnt_ref, run_ref, *, lat_blocks, d):
    is_lat = pl.program_id(0) < lat_blocks
    ys = jnp.where(is_lat, ysl_ref[...], ysc_ref[...])
    y = ys + dsk_ref[...] * u_ref[...]
    y = jax.nn.gelu(y, approximate=True)
    z = jnp.dot(y.astype(BF16), gw_ref[...], preferred_element_type=F32) + gb_ref[...]
    s5o = (y * jax.nn.sigmoid(z)).astype(BF16)
    r = jnp.where(is_lat, rl_ref[...], rc_ref[...])
    mixed = (jnp.dot(s5o, ws_ref[...], preferred_element_type=F32)
             + jnp.dot(r, wr_ref[...], preferred_element_type=F32))
    gate1 = m_ref[:, 2 * d:3 * d]
    x = x_ref[...] + gate1 * mixed
    xo_ref[...] = x
    ms = jnp.mean(x * x, axis=-1, keepdims=True)
    f = x * lax.rsqrt(ms + NORM_EPS) * g2_ref[...]
    f = f * (1.0 + m_ref[:, 4 * d:5 * d]) + m_ref[:, 3 * d:4 * d]
    tm = f.shape[0]
    for i in range(ROW_TILES):
        fu_ref[pl.ds(i, tm, stride=ROW_TILES), :] = f[:, 128 * i:128 * (i + 1)]
    logits_t = lax.dot_general(rw_ref[...], f, (((1,), (1,)), ((), ())), precision=HIGHEST,
                               preferred_element_type=F32)
    idx, wts = _route(logits_t, rb_ref[...])
    ei_ref[...] = idx
    ew_ref[...] = wts

    @pl.when(pl.program_id(0) == 0)
    def _():
        run_ref[...] = jnp.zeros_like(run_ref)

    e_iota = lax.broadcasted_iota(jnp.int32, (N_EXPERTS, tm), 0)
    tri = jnp.where(lax.broadcasted_iota(jnp.int32, (tm, tm), 0) <= lax.broadcasted_iota(jnp.int32, (tm, tm), 1),
                    1.0, 0.0).astype(BF16)
    hit0 = e_iota == idx[0:1, :]
    hit1 = e_iota == idx[1:2, :]
    p0 = jnp.dot(jnp.where(hit0, 1.0, 0.0).astype(BF16), tri, preferred_element_type=F32)
    p1 = jnp.dot(jnp.where(hit1, 1.0, 0.0).astype(BF16), tri, preferred_element_type=F32)
    tot0 = p0[:, tm - 1:tm]
    tot1 = p1[:, tm - 1:tm]
    before = run_ref[...]
    r0 = jnp.sum(jnp.where(hit0, before + p0, 0.0), axis=0, keepdims=True) - 1.0
    r1 = jnp.sum(jnp.where(hit1, before + tot0 + p1, 0.0), axis=0, keepdims=True) - 1.0
    rank_ref[...] = jnp.concatenate([r0, r1], axis=0).astype(jnp.int32)
    after = before + tot0 + tot1
    run_ref[...] = after
    cnt_ref[...] = after.astype(jnp.int32)


def _mix(ysl, ysc, pu, rl, rc, x, mods_cur, d_skip, glu_w, glu_b, w_out, g2, rw_t, rb, *, layer, n_lat,
         rows_per_batch):
    t, d = x.shape
    sw = pu.shape[1]
    tm = MIX_ROW_BLOCK
    lat_blocks = n_lat // tm
    per_b = rows_per_batch // tm
    ctx_row = n_lat // rows_per_batch
    const = lambda i: (0, 0)
    row = lambda i: (i, 0)
    lat_row = lambda i: (jnp.minimum(i, lat_blocks - 1), 0)
    ctx_rowblk = lambda i: (jnp.maximum(i - lat_blocks, 0), 0)
    once = pl.Buffered(1)
    kern = functools.partial(_mix_kernel, lat_blocks=lat_blocks, d=d)
    return pl.pallas_call(
        kern,
        out_shape=(jax.ShapeDtypeStruct((t, d), F32), jax.ShapeDtypeStruct((t * ROW_TILES, 128), F32),
                   jax.ShapeDtypeStruct((TOP_K, t), jnp.int32), jax.ShapeDtypeStruct((TOP_K, t), F32),
                   jax.ShapeDtypeStruct((TOP_K, t), jnp.int32), jax.ShapeDtypeStruct((N_EXPERTS, 1), jnp.int32)),
        grid=(t // tm,),
        scratch_shapes=[pltpu.VMEM((N_EXPERTS, 1), F32)],
        in_specs=[pl.BlockSpec((tm, sw), lat_row),
                  pl.BlockSpec((tm, sw), ctx_rowblk, pipeline_mode=once),
                  pl.BlockSpec((tm, sw), row),
                  pl.BlockSpec((tm, sw), lat_row),
                  pl.BlockSpec((tm, sw), ctx_rowblk, pipeline_mode=once),
                  pl.BlockSpec((tm, d), row),
                  pl.BlockSpec((None, 1, 6 * d), lambda i: (jnp.where(i < lat_blocks, i // per_b, ctx_row), 0, 0)),
                  pl.BlockSpec((1, sw), const),
                  pl.BlockSpec((None, sw, sw), lambda i: (layer, 0, 0), pipeline_mode=once),
                  pl.BlockSpec((1, sw), const),
                  pl.BlockSpec((None, sw, d), lambda i: (layer, 0, 0), pipeline_mode=once),
                  pl.BlockSpec((None, d - sw, d), lambda i: (layer, 1, 0), pipeline_mode=once),
                  pl.BlockSpec((1, d), const),
                  pl.BlockSpec((N_EXPERTS, d), const),
                  pl.BlockSpec((N_EXPERTS, 1), const)],
        out_specs=(pl.BlockSpec((tm, d), row), pl.BlockSpec((tm * ROW_TILES, 128), row),
                   pl.BlockSpec((TOP_K, tm), lambda i: (0, i)), pl.BlockSpec((TOP_K, tm), lambda i: (0, i)),
                   pl.BlockSpec((TOP_K, tm), lambda i: (0, i)), pl.BlockSpec((N_EXPERTS, 1), const)),
        compiler_params=_cparams(("arbitrary",), 58),
        name="mix",
    )(ysl, ysc, pu, rl, rc, x, mods_cur, d_skip, glu_w, glu_b, w_out, w_out, g2, rw_t, rb)


def _dispatch_plan(eidx, rank, counts):
    bm = MOE_ROWS
    n_assign = eidx.size
    nblk = -(-n_assign // bm) + N_EXPERTS
    counts = counts.reshape(N_EXPERTS)
    padded = (counts + bm - 1) // bm * bm
    pend = jnp.cumsum(padded)
    pstart = pend - padded
    experts = jnp.arange(N_EXPERTS, dtype=jnp.int32)[:, None, None]
    dest = rank + jnp.sum(jnp.where(eidx[None] == experts, pstart[:, None, None], 0), axis=0)
    blk_start = jnp.arange(nblk, dtype=jnp.int32) * bm
    bexp = jnp.minimum(jnp.sum((pend[None, :] <= blk_start[:, None]).astype(jnp.int32), axis=1), N_EXPERTS - 1)
    nused = (pend[-1] // bm).astype(jnp.int32).reshape(1)
    return dest.astype(jnp.int32), bexp.astype(jnp.int32), nused, counts.astype(jnp.int32), pstart.astype(jnp.int32)


def _dispatch_kernel(cnt_ref, pst_ref, nused_ref, dest_ref, fu_ref, xs_hbm, zrow, zblk, sem, zsem, *,
                     tb, bm, nblk):
    i = pl.program_id(0)
    rpt = ROW_TILES

    def body(t8, _):
        for u in range(DMA_UNROLL):
            tt = t8 * DMA_UNROLL + u
            src = fu_ref.at[pl.ds(pl.multiple_of(tt * rpt, rpt), rpt)]
            for k in range(TOP_K):
                dst = xs_hbm.at[pl.ds(pl.multiple_of(dest_ref[k, tt] * rpt, rpt), rpt)]
                pltpu.make_async_copy(src, dst, sem).start()
        return 0

    lax.fori_loop(0, tb // DMA_UNROLL, body, 0)

    @pl.when(i == 0)
    def _():
        zrow[...] = jnp.zeros_like(zrow)
        zblk[...] = jnp.zeros_like(zblk)

        def pad_rows(e, _):
            first = pst_ref[e] + cnt_ref[e]
            npad = (bm - cnt_ref[e] % bm) % bm

            def start(r, _):
                pltpu.make_async_copy(zrow, xs_hbm.at[pl.ds(pl.multiple_of((first + r) * rpt, rpt), rpt)],
                                      zsem).start()
                return 0

            def wait(r, _):
                pltpu.make_async_copy(zrow, xs_hbm.at[pl.ds(0, rpt)], zsem).wait()
                return 0

            lax.fori_loop(0, npad, start, 0)
            lax.fori_loop(0, npad, wait, 0)
            return 0

        lax.fori_loop(0, N_EXPERTS, pad_rows, 0)

        def pad_block(j, _):
            cp = pltpu.make_async_copy(zblk, xs_hbm.at[pl.ds(pl.multiple_of(j * bm * rpt, bm * rpt), bm * rpt)], zsem)
            cp.start()
            cp.wait()
            return 0

        lax.fori_loop(nused_ref[0], nblk, pad_block, 0)

    for k in range(TOP_K):
        pltpu.make_async_copy(fu_ref, xs_hbm.at[pl.ds(0, tb * rpt)], sem).wait()


def _dispatch(fu, dest, counts, pstart, nused, *, nblk):
    n_tok = dest.shape[1]
    bm = MOE_ROWS
    tb = 512
    assert n_tok % tb == 0
    kern = functools.partial(_dispatch_kernel, tb=tb, bm=bm, nblk=nblk)
    grid_spec = pltpu.PrefetchScalarGridSpec(
        num_scalar_prefetch=3,
        grid=(n_tok // tb,),
        in_specs=[pl.BlockSpec((TOP_K, tb), lambda i, c, p, n: (0, i), memory_space=pltpu.SMEM),
                  pl.BlockSpec((tb * ROW_TILES, 128), lambda i, c, p, n: (i, 0))],
        out_specs=pl.BlockSpec(memory_space=pl.ANY),
        scratch_shapes=[pltpu.VMEM((ROW_TILES, 128), F32),
                        pltpu.VMEM((bm * ROW_TILES, 128), F32),
                        pltpu.SemaphoreType.DMA(()), pltpu.SemaphoreType.DMA(())],
    )
    return pl.pallas_call(
        kern,
        out_shape=jax.ShapeDtypeStruct((nblk * bm * ROW_TILES, 128), F32),
        grid_spec=grid_spec,
        compiler_params=_cparams(("arbitrary",), 24),
        name="dispatch",
    )(counts, pstart, nused, dest, fu)


def _expert_kernel(bexp_ref, nused_ref, xs_ref, wg_ref, wu_ref, wd_ref, ys_ref, *, bm):
    @pl.when(pl.program_id(0) < nused_ref[0])
    def _():
        xb = jnp.concatenate([xs_ref[pl.ds(i, bm, stride=ROW_TILES), :].astype(BF16) for i in range(ROW_TILES)],
                             axis=1)
        gt = jnp.dot(xb, wg_ref[...], preferred_element_type=F32)
        up = jnp.dot(xb, wu_ref[...], preferred_element_type=F32)
        hmid = (gt * jax.nn.sigmoid(gt) * up).astype(BF16)
        y = jnp.dot(hmid, wd_ref[...], preferred_element_type=F32)
        for j in range(ROW_TILES):
            ys_ref[pl.ds(j, bm, stride=ROW_TILES), :] = y[:, 128 * j:128 * (j + 1)]

    @pl.when(pl.program_id(0) >= nused_ref[0])
    def _():
        ys_ref[...] = jnp.zeros_like(ys_ref)


def _experts(xs, bexp, nused, wg, wu, wd, layer):
    bm = MOE_ROWS
    nblk = bexp.shape[0]
    d, ff = wg.shape[2], wg.shape[3]
    grid_spec = pltpu.PrefetchScalarGridSpec(
        num_scalar_prefetch=2,
        grid=(nblk,),
        in_specs=[pl.BlockSpec((bm * ROW_TILES, 128), lambda j, be, nu: (j, 0)),
                  pl.BlockSpec((None, None, d, ff), lambda j, be, nu: (layer, be[j], 0, 0)),
                  pl.BlockSpec((None, None, d, ff), lambda j, be, nu: (layer, be[j], 0, 0)),
                  pl.BlockSpec((None, None, ff, d), lambda j, be, nu: (layer, be[j], 0, 0))],
        out_specs=pl.BlockSpec((bm * ROW_TILES, 128), lambda j, be, nu: (j, 0)),
    )
    return pl.pallas_call(
        functools.partial(_expert_kernel, bm=bm),
        out_shape=jax.ShapeDtypeStruct((nblk * bm * ROW_TILES, 128), F32),
        grid_spec=grid_spec,
        compiler_params=_cparams(("arbitrary",), 52),
        name="experts",
    )(bexp, nused, xs, wg, wu, wd)


def _rope_tables(n_lat):
    n_rows = n_lat // GRID_W
    pos = np.arange(n_lat)
    quarter = RET_HEAD_DIM // 4
    freqs = ROPE_BASE ** (-np.arange(quarter, dtype=np.float64) / quarter)
    ang = np.concatenate([(pos // GRID_W)[:, None] * freqs, (pos % GRID_W)[:, None] * freqs], axis=-1)
    assert ang.shape == (n_rows * GRID_W, RET_HEAD_DIM // 2)
    return jnp.asarray(np.cos(ang), F32), jnp.asarray(np.sin(ang), F32)


def kernel(x, c, ctx, c_ctx, mod_w, mod_b, norm1_g, norm2_g, w_in, s5_a_re, s5_a_im, s5_log_dt, s5_b_re, s5_b_im,
           s5_c_re, s5_c_im, s5_d, glu_w, glu_b, ret_decay_raw, w_out, router_w, router_b, exp_w_gate, exp_w_up,
           exp_w_down, final_norm_g):
    bsz, n_lat, d = x.shape
    n_ctx = ctx.shape[1]
    depth = mod_w.shape[0]
    s5w = s5_d.shape[1]
    t = bsz * (n_lat + n_ctx)
    lat_total = bsz * n_lat

    xs = jnp.concatenate([x.reshape(lat_total, d), ctx.reshape(bsz * n_ctx, d)], axis=0)
    c8 = jnp.zeros((8, d), F32).at[:bsz].set(c).at[bsz].set(c_ctx)
    mods = _mods(c8, mod_w, mod_b).reshape(depth, 8, 1, 6 * d)
    cos, sin = _rope_tables(n_lat)
    rw_t = router_w.T
    rb = router_b.reshape(N_EXPERTS, 1)

    w_in_b, glu_w_b, w_out_b = w_in.astype(BF16), glu_w.astype(BF16), w_out.astype(BF16)
    wg_b, wu_b, wd_b = exp_w_gate.astype(BF16), exp_w_up.astype(BF16), exp_w_down.astype(BF16)
    assert w_out.shape[1] == 2 * s5w

    moe = None
    for layer in range(depth):
        xs, h = _prenorm(xs, moe, mods[layer - 1] if layer else None, mods[layer], norm1_g[layer],
                         n_lat=lat_total, rows_per_batch=n_lat, final=False)
        pu, pr = _in_proj(h, w_in_b, layer, s5w)

        w1, q, a16 = _s5_weights(s5_a_re[layer], s5_a_im[layer], s5_log_dt[layer], s5_b_re[layer],
                                 s5_b_im[layer], s5_c_re[layer], s5_c_im[layer])
        ysl, ysc = _s5_scan(pu, w1, q, a16, bsz=bsz, n_lat=n_lat, n_ctx=n_ctx)

        log_gamma = jnp.log1p(-jnp.exp2(-ret_decay_raw[layer].astype(F32)))
        rl, rc = _retention(pr, log_gamma, cos, sin, bsz=bsz, n_lat=n_lat, n_ctx=n_ctx, col0=0)

        xs, fu, eidx, ew, rank, counts = _mix(
            ysl, ysc, pu, rl, rc, xs, mods[layer], s5_d[layer].reshape(1, s5w), glu_w_b,
            glu_b[layer].reshape(1, s5w), w_out_b, norm2_g[layer].reshape(1, d), rw_t, rb,
            layer=layer, n_lat=lat_total, rows_per_batch=n_lat)

        dest, bexp, nused, counts, pstart = _dispatch_plan(eidx, rank, counts)
        xsort = _dispatch(fu, dest, counts, pstart, nused, nblk=bexp.shape[0])
        ys = _experts(xsort, bexp, nused, wg_b, wu_b, wd_b, layer)
        moe = (ys, dest, ew.T)

    out = _prenorm(xs, moe, mods[depth - 1], mods[depth - 1], final_norm_g,
                   n_lat=lat_total, rows_per_batch=n_lat, final=True)
    return out.reshape(bsz, n_lat, d)
```

```python
import functools
import math

import jax
import jax.numpy as jnp
import numpy as np
from jax import lax
from jax.experimental import pallas as pl
from jax.experimental.pallas import tpu as pltpu

F32 = jnp.float32
BF16 = jnp.bfloat16
HIGHEST = lax.Precision.HIGHEST

GRID_W = 64
S5_GROUP = 16
S5_STATE = 64
S5_STEP = 16
RET_HEAD_DIM = 256
RET_BLOCK = 256
ROPE_BASE = 10000.0
N_EXPERTS = 16
N_EXPERT_GROUPS = 4
EXPERTS_PER_GROUP = 4
TOP_K = 2
NORM_EPS = 1e-6

ROW_BLOCK = 256
MIX_ROW_BLOCK = 512
MM_ROW_BLOCK = 1024
MM_COL_BLOCK = 1024
MOE_ROWS = 256
DMA_UNROLL = 8
S5_TILE_GROUPS = 128 // S5_GROUP
MIB = 1024 * 1024


def _cparams(sem, vmem_mib):
    return pltpu.CompilerParams(dimension_semantics=sem, vmem_limit_bytes=int(vmem_mib * MIB))


def _mods_kernel(c_ref, w_ref, b_ref, o_ref):
    cv = c_ref[...]
    act = cv * jax.nn.sigmoid(cv)
    o_ref[0] = jnp.dot(act, w_ref[0], precision=HIGHEST, preferred_element_type=F32) + b_ref[0]


def _mods(c8, mod_w, mod_b):
    depth, d, n = mod_w.shape
    tn = 1024
    return pl.pallas_call(
        _mods_kernel,
        out_shape=jax.ShapeDtypeStruct((depth, 8, n), F32),
        grid=(depth, n // tn),
        in_specs=[pl.BlockSpec((8, d), lambda l, j: (0, 0)),
                  pl.BlockSpec((1, d, tn), lambda l, j: (l, 0, j)),
                  pl.BlockSpec((1, 1, tn), lambda l, j: (l, 0, j))],
        out_specs=pl.BlockSpec((1, 8, tn), lambda l, j: (l, 0, j)),
        compiler_params=_cparams(("arbitrary", "arbitrary"), 40),
        name="adaln_mods",
    )(c8, mod_w, mod_b.reshape(depth, 1, n))


def _prenorm_kernel(*refs, combine, final, d, tm, nblk, lat_blocks):
    if combine:
        (dest0_ref, destn_ref, x_ref, ys_hbm, wt_ref, mprev_ref, mcur_ref, g_ref), rest = refs[:8], refs[8:]
        outs, (ybuf, sem) = rest[:-2], rest[-2:]
    else:
        (xl_ref, xc_ref, mcur_ref, g_ref), outs = refs[:4], refs[4:]
    if not combine:
        x = jnp.where(pl.program_id(0) < lat_blocks, xl_ref[...], xc_ref[...])
    else:
        x = x_ref[...]
        i = pl.program_id(0)
        slot = i % 2
        per_slot = TOP_K * tm

        def gather(dref, s):
            def body(t8, _):
                for u in range(DMA_UNROLL):
                    tt = t8 * DMA_UNROLL + u
                    for k in range(TOP_K):
                        src = ys_hbm.at[pl.ds(dref[k, tt], 1)]
                        dst = ybuf.at[pl.ds(pl.multiple_of(s * per_slot + k * tm + t8 * DMA_UNROLL, DMA_UNROLL) + u, 1)]
                        pltpu.make_async_copy(src, dst, sem.at[s]).start(priority=k)
                return 0
            lax.fori_loop(0, tm // DMA_UNROLL, body, 0)

        @pl.when(i == 0)
        def _():
            gather(dest0_ref, 0)

        @pl.when(i + 1 < nblk)
        def _():
            gather(destn_ref, 1 - slot)

        base = pl.multiple_of(slot * per_slot, per_slot)
        pltpu.make_async_copy(ys_hbm.at[pl.ds(0, per_slot)], ybuf.at[pl.ds(base, per_slot)], sem.at[slot]).wait()

        def rows_of(k):
            return ybuf[pl.ds(pl.multiple_of(base + k * tm, tm), tm), :]

        gate2 = mprev_ref[:, 5 * d:6 * d]
        x = x + gate2 * (wt_ref[:, 0:1] * rows_of(0) + wt_ref[:, 1:2] * rows_of(1))
    ms = jnp.mean(x * x, axis=-1, keepdims=True)
    xn = x * lax.rsqrt(ms + NORM_EPS) * g_ref[...]
    if final:
        outs[0][...] = xn
    else:
        outs[0][...] = x
        shift = mcur_ref[:, 0:d]
        scale = mcur_ref[:, d:2 * d]
        outs[1][...] = (xn * (1.0 + scale) + shift).astype(BF16)


def _prenorm(x, moe, mods_prev, mods_cur, gain, *, n_lat, rows_per_batch, final):
    combine = moe is not None
    assert combine != isinstance(x, tuple)
    t = x.shape[0] if combine else x[0].shape[0] + x[1].shape[0]
    d = x.shape[1] if combine else x[0].shape[1]
    tm = ROW_BLOCK
    nblk = (n_lat if final else t) // tm
    lat_blocks = n_lat // tm
    per_b = rows_per_batch // tm
    ctx_row = n_lat // rows_per_batch

    def mod_idx(i):
        return (jnp.where(i < lat_blocks, i // per_b, ctx_row), 0, 0)

    row_spec = pl.BlockSpec((tm, d), lambda i: (i, 0))
    mod_spec = pl.BlockSpec((None, 1, 6 * d), mod_idx)
    g_spec = pl.BlockSpec((1, d), lambda i: (0, 0))
    ins, specs, scratch = [], [], []
    if combine:
        ys, dest, wt = moe
        ins += [dest, dest]
        specs += [pl.BlockSpec((TOP_K, tm), lambda i: (0, 0), memory_space=pltpu.SMEM),
                  pl.BlockSpec((TOP_K, tm), lambda i: (0, jnp.minimum(i + 1, nblk - 1)), memory_space=pltpu.SMEM)]
    if combine:
        ins.append(x)
        specs.append(row_spec)
    else:
        ins += list(x)
        specs += [pl.BlockSpec((tm, d), lambda i: (jnp.minimum(i, lat_blocks - 1), 0)),
                  pl.BlockSpec((tm, d), lambda i: (jnp.maximum(i - lat_blocks, 0), 0))]
    if combine:
        ins += [ys, wt, mods_prev]
        specs += [pl.BlockSpec(memory_space=pl.ANY), pl.BlockSpec((tm, TOP_K), lambda i: (i, 0)), mod_spec]
        scratch = [pltpu.VMEM((2 * TOP_K * tm, d), F32), pltpu.SemaphoreType.DMA((2,))]
    ins += [mods_cur, gain.reshape(1, d)]
    specs += [mod_spec, g_spec]
    if final:
        out_shape = jax.ShapeDtypeStruct((n_lat, d), F32)
        out_specs = row_spec
    else:
        out_shape = (jax.ShapeDtypeStruct((t, d), F32), jax.ShapeDtypeStruct((t, d), BF16))
        out_specs = (row_spec, row_spec)
    return pl.pallas_call(
        functools.partial(_prenorm_kernel, combine=combine, final=final, d=d, tm=tm, nblk=nblk,
                          lat_blocks=lat_blocks),
        out_shape=out_shape, grid=(nblk,), in_specs=specs, out_specs=out_specs, scratch_shapes=scratch,
        compiler_params=_cparams(("arbitrary",), 40),
        name="prenorm_final" if final else ("prenorm_combine" if combine else "prenorm"),
    )(*ins)


def _inproj_kernel(h_ref, w_ref, pu_ref, pr_ref):
    acc = jnp.dot(h_ref[...], w_ref[...], preferred_element_type=F32)

    @pl.when(pl.program_id(1) == 0)
    def _():
        pu_ref[...] = acc

    @pl.when(pl.program_id(1) > 0)
    def _():
        pr_ref[...] = acc.astype(pr_ref.dtype)


def _in_proj(h, w, layer, s5w):
    t, k = h.shape
    n = w.shape[2]
    tm, tn = MM_ROW_BLOCK, MM_COL_BLOCK
    assert s5w == tn
    return pl.pallas_call(
        _inproj_kernel,
        out_shape=(jax.ShapeDtypeStruct((t, s5w), F32), jax.ShapeDtypeStruct((t, n - s5w), BF16)),
        grid=(t // tm, n // tn),
        in_specs=[pl.BlockSpec((tm, k), lambda i, j: (i, 0)),
                  pl.BlockSpec((None, k, tn), lambda i, j: (layer, 0, j))],
        out_specs=(pl.BlockSpec((tm, tn), lambda i, j: (i, 0)),
                   pl.BlockSpec((tm, tn), lambda i, j: (i, jnp.maximum(j - 1, 0)))),
        compiler_params=_cparams(("arbitrary", "arbitrary"), 40),
        name="in_proj",
    )(h, w)


def _s5_weights(a_re, a_im, log_dt, b_re, b_im, c_re, c_im):
    ns = S5_STEP
    a_re, a_im = a_re.astype(F32), a_im.astype(F32)
    dt = jnp.exp(log_dt.astype(F32))[..., None]
    lam_re, lam_im = dt * a_re, dt * a_im
    tau = jnp.arange(ns + 1, dtype=F32)[:, None, None, None]
    mag = jnp.exp(tau * lam_re[None])
    p_re, p_im = mag * jnp.cos(tau * lam_im[None]), mag * jnp.sin(tau * lam_im[None])
    x, y = p_re[1] - 1.0, p_im[1]
    den = a_re * a_re + a_im * a_im
    k_re, k_im = (x * a_re + y * a_im) / den, (y * a_re - x * a_im) / den
    b_re, b_im = b_re.astype(F32), b_im.astype(F32)
    bb_re = k_re[..., None] * b_re - k_im[..., None] * b_im
    bb_im = k_re[..., None] * b_im + k_im[..., None] * b_re
    c_re, c_im = c_re.astype(F32), c_im.astype(F32)
    cp_re = c_re[None] * p_re[:, :, :, None, :] - c_im[None] * p_im[:, :, :, None, :]
    cp_im = c_re[None] * p_im[:, :, :, None, :] + c_im[None] * p_re[:, :, :, None, :]
    kmat = jnp.einsum('tdgon,dgni->tdgoi', jnp.concatenate([cp_re, -cp_im], axis=-1),
                      jnp.concatenate([bb_re, bb_im], axis=-2))
    lag = np.arange(ns)[None, :] - np.arange(ns)[:, None]
    sel_f = jnp.asarray(lag[None] == np.arange(ns)[:, None, None], F32)
    sel_b = jnp.asarray(-lag[None] == np.arange(ns)[:, None, None], F32)
    g = a_re.shape[1]
    m = (jnp.einsum('lst,lgoi->gsito', sel_f, kmat[:ns, 0]) + jnp.einsum('lst,lgoi->gsito', sel_b, kmat[:ns, 1]))
    m = m.reshape(g, ns * S5_GROUP, ns * S5_GROUP)

    def inject(p_r, p_i, d):
        r = p_r[:, :, :, None] * bb_re[d][None] - p_i[:, :, :, None] * bb_im[d][None]
        i = p_r[:, :, :, None] * bb_im[d][None] + p_i[:, :, :, None] * bb_re[d][None]
        tr = lambda z: jnp.transpose(z, (1, 0, 3, 2)).reshape(g, ns * S5_GROUP, S5_STATE)
        return tr(r), tr(i)

    pf_re, pf_im = inject(p_re[ns - 1::-1, 0], p_im[ns - 1::-1, 0], 0)
    pb_re, pb_im = inject(p_re[:ns, 1], p_im[:ns, 1], 1)
    w1 = jnp.concatenate([m, pf_re, pb_re, pf_im, pb_im], axis=-1)

    def carry(cp_t):
        return jnp.transpose(cp_t, (1, 3, 0, 2)).reshape(g, S5_STATE, ns * S5_GROUP)

    up = slice(1, ns + 1)
    down = slice(ns, 0, -1)
    q = jnp.concatenate([carry(cp_re[up, 0]), carry(cp_re[down, 1]),
                         -carry(cp_im[up, 0]), -carry(cp_im[down, 1])], axis=1)
    a16 = jnp.stack([jnp.concatenate([p_re[ns, 0], p_re[ns, 1]], axis=-1),
                     jnp.concatenate([p_im[ns, 0], p_im[ns, 1]], axis=-1)], axis=1)
    return w1.astype(BF16), q.astype(BF16), a16


def _group_transpose(xs):
    ng = len(xs)
    grp = lax.broadcasted_iota(jnp.int32, xs[0].shape, 1) // S5_GROUP
    rolled = []
    for j in range(ng):
        y = xs[j % ng]
        for k in range(1, ng):
            y = jnp.where(grp == k, xs[(k + j) % ng], y)
        rolled.append(y if j == 0 else pltpu.roll(y, S5_GROUP * j, 1))
    outs = []
    for a in range(ng):
        o = rolled[(-a) % ng]
        for b in range(1, ng):
            o = jnp.where(grp == b, rolled[(b - a) % ng], o)
        outs.append(o)
    return outs


def _s5_kernel(ul_ref, uc_ref, w1_ref, q_ref, a_ref, yl_ref, yc_ref, uflat, st_re, st_im, yfl, *,
               lat_chunks, ctx_chunks):
    ng = S5_TILE_GROUPS
    width = S5_STEP * S5_GROUP
    n_chunks = lat_chunks + ctx_chunks
    sub = 32

    def flatten(src_ref, c0, row0, n):
        for hh in range(2):
            xs = [src_ref[pl.ds(row0 + hh * 8 + t, n, stride=S5_STEP), :] for t in range(8)]
            outs = _group_transpose(xs)
            for k in range(ng):
                uflat[k, pl.ds(c0, n), hh * 128:(hh + 1) * 128] = outs[k].astype(BF16)

    def flat_body(i, _):
        c0 = pl.multiple_of(i * sub, sub)
        flatten(ul_ref, c0, c0 * S5_STEP, sub)
        return 0

    lax.fori_loop(0, lat_chunks // sub, flat_body, 0)
    flatten(uc_ref, lat_chunks, 0, ctx_chunks)

    for k in range(ng):
        z = jnp.dot(uflat[k], w1_ref[k], preferred_element_type=F32)
        yfl[k] = z[:, :width]
        st_re[pl.ds(k, n_chunks, stride=ng), :] = z[:, width:width + 128]
        st_im[pl.ds(k, n_chunks, stride=ng), :] = z[:, width + 128:]

    lane = lax.broadcasted_iota(jnp.int32, (ng, 128), 1)
    fwd_lane = lane < S5_STATE
    a_re = a_ref[0]
    a_im = a_ref[1]

    def make_body(base, n):
        def body(i, carry):
            h_re, h_im = carry
            rf = pl.multiple_of((base + i) * ng, ng)
            rb = pl.multiple_of((base + n - 1 - i) * ng, ng)
            tf_re, tf_im = st_re[pl.ds(rf, ng), :], st_im[pl.ds(rf, ng), :]
            tb_re, tb_im = st_re[pl.ds(rb, ng), :], st_im[pl.ds(rb, ng), :]
            t_re = jnp.where(fwd_lane, tf_re, tb_re)
            t_im = jnp.where(fwd_lane, tf_im, tb_im)
            n_re = a_re * h_re - a_im * h_im + t_re
            n_im = a_re * h_im + a_im * h_re + t_im
            st_re[pl.ds(rf, ng), :] = jnp.where(fwd_lane, h_re, tf_re)
            st_im[pl.ds(rf, ng), :] = jnp.where(fwd_lane, h_im, tf_im)
            st_re[pl.ds(rb, ng), :] = jnp.where(fwd_lane, tb_re, h_re)
            st_im[pl.ds(rb, ng), :] = jnp.where(fwd_lane, tb_im, h_im)
            return n_re, n_im
        return body

    carry = (jnp.zeros((ng, 128), F32), jnp.zeros((ng, 128), F32))
    carry = lax.fori_loop(0, ctx_chunks, make_body(lat_chunks, ctx_chunks), carry)
    carry = lax.fori_loop(0, lat_chunks, make_body(0, lat_chunks), carry)

    for k in range(ng):
        hk = jnp.concatenate([st_re[pl.ds(k, n_chunks, stride=ng), :], st_im[pl.ds(k, n_chunks, stride=ng), :]],
                             axis=1).astype(BF16)
        yfl[k] = yfl[k] + jnp.dot(hk, q_ref[k], preferred_element_type=F32)

    def unflatten(dst_ref, c0, row0, n):
        for hh in range(2):
            vs = [yfl[k, pl.ds(c0, n), hh * 128:(hh + 1) * 128] for k in range(ng)]
            ts = _group_transpose(vs)
            for t in range(8):
                dst_ref[pl.ds(row0 + hh * 8 + t, n, stride=S5_STEP), :] = ts[t]

    def unflat_body(i, _):
        c0 = pl.multiple_of(i * sub, sub)
        unflatten(yl_ref, c0, c0 * S5_STEP, sub)
        return 0

    lax.fori_loop(0, lat_chunks // sub, unflat_body, 0)
    unflatten(yc_ref, lat_chunks, 0, ctx_chunks)


def _s5_scan(pu, w1, q, a16, layer, *, bsz, n_lat, n_ctx):
    g = w1.shape[1]
    ng = S5_TILE_GROUPS
    width = S5_STEP * S5_GROUP
    lat_chunks, ctx_chunks = n_lat // S5_STEP, n_ctx // S5_STEP
    assert lat_chunks % 32 == 0 and ctx_chunks % 2 == 0 and ctx_chunks % 16 == 0
    n_chunks = lat_chunks + ctx_chunks
    ctx_blk0 = bsz * n_lat // n_ctx
    a_t = jnp.transpose(a16.reshape(-1, g // ng, ng, 2, 128), (0, 1, 3, 2, 4))
    kern = functools.partial(_s5_kernel, lat_chunks=lat_chunks, ctx_chunks=ctx_chunks)
    return pl.pallas_call(
        kern,
        out_shape=(jax.ShapeDtypeStruct((bsz * n_lat, g * S5_GROUP), F32),
                   jax.ShapeDtypeStruct((bsz * n_ctx, g * S5_GROUP), F32)),
        grid=(g // ng, bsz),
        in_specs=[pl.BlockSpec((n_lat, 128), lambda i, b: (b, i)),
                  pl.BlockSpec((n_ctx, 128), lambda i, b: (ctx_blk0 + b, i)),
                  pl.BlockSpec((None, ng, width, 2 * width), lambda i, b: (layer, i, 0, 0)),
                  pl.BlockSpec((None, ng, width, width), lambda i, b: (layer, i, 0, 0)),
                  pl.BlockSpec((None, None, 2, ng, 128), lambda i, b: (layer, i, 0, 0, 0))],
        out_specs=(pl.BlockSpec((n_lat, 128), lambda i, b: (b, i)),
                   pl.BlockSpec((n_ctx, 128), lambda i, b: (b, i))),
        scratch_shapes=[pltpu.VMEM((ng, n_chunks, width), BF16),
                        pltpu.VMEM((n_chunks * ng, 128), F32),
                        pltpu.VMEM((n_chunks * ng, 128), F32),
                        pltpu.VMEM((ng, n_chunks, width), F32)],
        compiler_params=_cparams(("arbitrary", "arbitrary"), 40),
        name="s5_scan",
    )(pu, pu, w1, q, a_t)


def _ret_kernel(lg_ref, ql_ref, kl_ref, vl_ref, gl_ref, qc_ref, kc_ref, vc_ref, gc_ref, cos_ref, sin_ref,
                ol_ref, oc_ref, qs, ks, vs, acc, sf, sb, dm, *, n_ctx, n_lat):
    c = RET_BLOCK
    half = RET_HEAD_DIM // 2
    head = pl.program_id(1)
    lgf = lg_ref[0, head]
    lgb = lg_ref[1, head]
    qscale = RET_HEAD_DIM ** -0.5
    ctx_rows = n_ctx * c

    ii = lax.broadcasted_iota(jnp.int32, (c, c), 0)
    jj = lax.broadcasted_iota(jnp.int32, (c, c), 1)
    rel = (ii - jj).astype(F32)
    dsum = (jnp.where(rel >= 0, jnp.exp(lgf * jnp.maximum(rel, 0.0)), 0.0)
            + jnp.where(rel <= 0, jnp.exp(lgb * jnp.maximum(-rel, 0.0)), 0.0))
    pos = lax.broadcasted_iota(jnp.int32, (c, 1), 0).astype(F32)
    qdec_f = jnp.exp(lgf * (pos + 1.0))
    kdec_f = jnp.exp(lgf * (c - 1.0 - pos))
    qdec_b = jnp.exp(lgb * (c - pos))
    kdec_b = jnp.exp(lgb * pos)
    cg_f = jnp.exp(lgf * c)
    cg_b = jnp.exp(lgb * c)

    qs[0:ctx_rows, :] = (qc_ref[...].astype(F32) * qscale).astype(BF16)
    ks[0:ctx_rows, :] = kc_ref[...]
    vs[0:ctx_rows, :] = vc_ref[...]

    def rope_body(n, _):
        r = pl.multiple_of(n * c, c)
        cs = cos_ref[pl.ds(r, c), :]
        sn = sin_ref[pl.ds(r, c), :]

        def rot(t):
            t1, t2 = t[:, :half], t[:, half:]
            return jnp.concatenate([t1 * cs - t2 * sn, t2 * cs + t1 * sn], axis=1)

        dst = pl.multiple_of(ctx_rows + r, c)
        qs[pl.ds(dst, c), :] = (rot(ql_ref[pl.ds(r, c), :].astype(F32)) * qscale).astype(BF16)
        ks[pl.ds(dst, c), :] = rot(kl_ref[pl.ds(r, c), :].astype(F32)).astype(BF16)
        vs[pl.ds(dst, c), :] = vl_ref[pl.ds(r, c), :]
        return 0

    lax.fori_loop(0, n_lat, rope_body, 0)

    dm[...] = dsum
    sf[...] = jnp.zeros_like(sf)
    sb[...] = jnp.zeros_like(sb)

    def load(r):
        return qs[pl.ds(r, c), :], ks[pl.ds(r, c), :], vs[pl.ds(r, c), :]

    def intra_of(q, k, v):
        s = lax.dot_general(q, k, (((1,), (1,)), ((), ())), preferred_element_type=F32)
        return jnp.dot((s * dm[...]).astype(BF16), v, preferred_element_type=F32)

    def step(s_ref, q, k, v, qdec, kdec, cg):
        inter = jnp.dot(q, s_ref[...].astype(BF16), preferred_element_type=F32) * qdec
        kd = (k.astype(F32) * kdec).astype(BF16)
        s_ref[...] = cg * s_ref[...] + lax.dot_general(kd, v, (((0,), (0,)), ((), ())),
                                                       preferred_element_type=F32)
        return inter

    def finish(o, gate, out_ref, ro):
        o = o * lax.rsqrt(jnp.mean(o * o, axis=-1, keepdims=True) + NORM_EPS)
        out_ref[pl.ds(ro, c), :] = (o * (gate * jax.nn.sigmoid(gate))).astype(out_ref.dtype)

    fwd = (sf, qdec_f, kdec_f, cg_f)
    bwd = (sb, qdec_b, kdec_b, cg_b)

    for n in range(n_ctx):
        q, k, v = load(n * c)
        acc[n * c:(n + 1) * c, :] = intra_of(q, k, v) + step(fwd[0], q, k, v, *fwd[1:])
    for n in range(n_ctx - 1, -1, -1):
        q, k, v = load(n * c)
        o = acc[n * c:(n + 1) * c, :] + step(bwd[0], q, k, v, *bwd[1:])
        finish(o, gc_ref[n * c:(n + 1) * c, :].astype(F32), oc_ref, n * c)

    def first_visits(i, _):
        rf = pl.multiple_of(i * c, c)
        rb = pl.multiple_of((n_lat - 1 - i) * c, c)
        q, k, v = load(ctx_rows + rf)
        acc[pl.ds(ctx_rows + rf, c), :] = intra_of(q, k, v) + step(fwd[0], q, k, v, *fwd[1:])
        q, k, v = load(ctx_rows + rb)
        acc[pl.ds(ctx_rows + rb, c), :] = step(bwd[0], q, k, v, *bwd[1:])
        return 0

    def second_visits(i, _):
        rf = pl.multiple_of(i * c, c)
        rb = pl.multiple_of((n_lat - 1 - i) * c, c)
        q, k, v = load(ctx_rows + rf)
        o = acc[pl.ds(ctx_rows + rf, c), :] + intra_of(q, k, v) + step(fwd[0], q, k, v, *fwd[1:])
        finish(o, gl_ref[pl.ds(rf, c), :].astype(F32), ol_ref, rf)
        q, k, v = load(ctx_rows + rb)
        o = acc[pl.ds(ctx_rows + rb, c), :] + step(bwd[0], q, k, v, *bwd[1:])
        finish(o, gl_ref[pl.ds(rb, c), :].astype(F32), ol_ref, rb)
        return 0

    lax.fori_loop(0, n_lat // 2, first_visits, 0)
    lax.fori_loop(n_lat // 2, n_lat, second_visits, 0)


def _retention(p, log_gamma, cos, sin, *, bsz, n_lat, n_ctx, col0):
    t = p.shape[0]
    dh = RET_HEAD_DIM
    heads = (p.shape[1] - col0) // (4 * dh)
    qb, kb, vb, gb = (col0 // dh + i * heads for i in range(4))
    ctx_blk0 = bsz * n_lat // n_ctx

    def lat(cb):
        return pl.BlockSpec((n_lat, dh), lambda b, h: (b, cb + h))

    def ctx(cb):
        return pl.BlockSpec((n_ctx, dh), lambda b, h: (ctx_blk0 + b, cb + h))

    tab = pl.BlockSpec((n_lat, dh // 2), lambda b, h: (0, 0))
    assert n_ctx % RET_BLOCK == 0 and (n_lat // RET_BLOCK) % 2 == 0
    kern = functools.partial(_ret_kernel, n_ctx=n_ctx // RET_BLOCK, n_lat=n_lat // RET_BLOCK)
    seq = n_ctx + n_lat
    return pl.pallas_call(
        kern,
        out_shape=(jax.ShapeDtypeStruct((bsz * n_lat, heads * dh), BF16),
                   jax.ShapeDtypeStruct((bsz * n_ctx, heads * dh), BF16)),
        grid=(bsz, heads),
        in_specs=[pl.BlockSpec(memory_space=pltpu.SMEM),
                  lat(qb), lat(kb), lat(vb), lat(gb), ctx(qb), ctx(kb), ctx(vb), ctx(gb), tab, tab],
        out_specs=(pl.BlockSpec((n_lat, dh), lambda b, h: (b, h)),
                   pl.BlockSpec((n_ctx, dh), lambda b, h: (b, h))),
        scratch_shapes=[pltpu.VMEM((seq, dh), BF16), pltpu.VMEM((seq, dh), BF16), pltpu.VMEM((seq, dh), BF16),
                        pltpu.VMEM((seq, dh), F32), pltpu.VMEM((dh, dh), F32), pltpu.VMEM((dh, dh), F32),
                        pltpu.VMEM((RET_BLOCK, RET_BLOCK), F32)],
        compiler_params=_cparams(("arbitrary", "arbitrary"), 52),
        name="retention",
    )(log_gamma, p, p, p, p, p, p, p, p, cos, sin)


def _top2_sum(a, b, c, d):
    hi1, lo1 = jnp.maximum(a, b), jnp.minimum(a, b)
    hi2, lo2 = jnp.maximum(c, d), jnp.minimum(c, d)
    return jnp.maximum(hi1, hi2) + jnp.maximum(jnp.minimum(hi1, hi2), jnp.maximum(lo1, lo2))


def _route(logits_t, rb):
    sc = jax.nn.sigmoid(logits_t)
    biased = sc + rb
    per = EXPERTS_PER_GROUP
    brow = [biased[e:e + 1, :] for e in range(N_EXPERTS)]
    srow = [sc[e:e + 1, :] for e in range(N_EXPERTS)]
    gscore = [_top2_sum(*brow[per * g:per * g + per]) for g in range(N_EXPERT_GROUPS)]
    best_v = gscore[0]
    best_g = jnp.zeros_like(best_v, dtype=jnp.int32)
    for g in range(1, N_EXPERT_GROUPS):
        upd = gscore[g] > best_v
        best_v = jnp.where(upd, gscore[g], best_v)
        best_g = jnp.where(upd, g, best_g)
    vals, sels = [], []
    for j in range(per):
        v = brow[j]
        s = srow[j]
        for g in range(1, N_EXPERT_GROUPS):
            v = jnp.where(best_g == g, brow[per * g + j], v)
            s = jnp.where(best_g == g, srow[per * g + j], s)
        vals.append(v)
        sels.append(s)
    v1, i1, s1 = vals[0], jnp.zeros_like(best_g), sels[0]
    for j in range(1, per):
        upd = vals[j] > v1
        v1 = jnp.where(upd, vals[j], v1)
        s1 = jnp.where(upd, sels[j], s1)
        i1 = jnp.where(upd, j, i1)
    v2 = jnp.full_like(v1, -jnp.inf)
    i2, s2 = jnp.zeros_like(best_g), jnp.zeros_like(s1)
    for j in range(per):
        upd = (i1 != j) & (vals[j] > v2)
        v2 = jnp.where(upd, vals[j], v2)
        s2 = jnp.where(upd, sels[j], s2)
        i2 = jnp.where(upd, j, i2)
    tot = s1 + s2
    idx = jnp.concatenate([best_g * per + i1, best_g * per + i2], axis=0)
    wts = jnp.concatenate([s1 / tot, s2 / tot], axis=0)
    return idx, wts


def _mix_kernel(ysl_ref, ysc_ref, u_ref, rl_ref, rc_ref, x_ref, m_ref, dsk_ref, gw_ref, gb_ref, ws_ref, wr_ref,
                g2_ref, rw_ref, rb_ref, xo_ref, fu_ref, ei_ref, ew_ref, rank_ref, cnt_ref, run_ref, *, lat_blocks, d):
    is_lat = pl.program_id(0) < lat_blocks
    ys = jnp.where(is_lat, ysl_ref[...], ysc_ref[...])
    y = ys + dsk_ref[...] * u_ref[...]
    y = jax.nn.gelu(y, approximate=True)
    z = jnp.dot(y.astype(BF16), gw_ref[...], preferred_element_type=F32) + gb_ref[...]
    s5o = (y * jax.nn.sigmoid(z)).astype(BF16)
    r = jnp.where(is_lat, rl_ref[...], rc_ref[...])
    mixed = (jnp.dot(s5o, ws_ref[...], preferred_element_type=F32)
             + jnp.dot(r, wr_ref[...], preferred_element_type=F32))
    gate1 = m_ref[:, 2 * d:3 * d]
    x = x_ref[...] + gate1 * mixed
    xo_ref[...] = x
    ms = jnp.mean(x * x, axis=-1, keepdims=True)
    f = x * lax.rsqrt(ms + NORM_EPS) * g2_ref[...]
    f = f * (1.0 + m_ref[:, 4 * d:5 * d]) + m_ref[:, 3 * d:4 * d]
    tm = f.shape[0]
    fu_ref[...] = f
    logits_t = lax.dot_general(rw_ref[...], f, (((1,), (1,)), ((), ())), precision=HIGHEST,
                               preferred_element_type=F32)
    idx, wts = _route(logits_t, rb_ref[...])
    ei_ref[...] = idx
    ew_ref[...] = wts

    @pl.when(pl.program_id(0) == 0)
    def _():
        run_ref[...] = jnp.zeros_like(run_ref)

    e_iota = lax.broadcasted_iota(jnp.int32, (N_EXPERTS, tm), 0)
    tri = jnp.where(lax.broadcasted_iota(jnp.int32, (tm, tm), 0) <= lax.broadcasted_iota(jnp.int32, (tm, tm), 1),
                    1.0, 0.0).astype(BF16)
    hit0 = e_iota == idx[0:1, :]
    hit1 = e_iota == idx[1:2, :]
    p0 = jnp.dot(jnp.where(hit0, 1.0, 0.0).astype(BF16), tri, preferred_element_type=F32)
    p1 = jnp.dot(jnp.where(hit1, 1.0, 0.0).astype(BF16), tri, preferred_element_type=F32)
    tot0 = p0[:, tm - 1:tm]
    tot1 = p1[:, tm - 1:tm]
    before = run_ref[...]
    r0 = jnp.sum(jnp.where(hit0, before + p0, 0.0), axis=0, keepdims=True) - 1.0
    r1 = jnp.sum(jnp.where(hit1, before + tot0 + p1, 0.0), axis=0, keepdims=True) - 1.0
    rank_ref[...] = jnp.concatenate([r0, r1], axis=0).astype(jnp.int32)
    after = before + tot0 + tot1
    run_ref[...] = after
    cnt_ref[...] = after.astype(jnp.int32)


def _mix(ysl, ysc, pu, rl, rc, x, mods_cur, d_skip, glu_w, glu_b, w_out, g2, rw_t, rb, *, layer, n_lat,
         rows_per_batch):
    t, d = x.shape
    sw = pu.shape[1]
    tm = MIX_ROW_BLOCK
    lat_blocks = n_lat // tm
    per_b = rows_per_batch // tm
    ctx_row = n_lat // rows_per_batch
    const = lambda i: (0, 0)
    row = lambda i: (i, 0)
    lat_row = lambda i: (jnp.minimum(i, lat_blocks - 1), 0)
    ctx_rowblk = lambda i: (jnp.maximum(i - lat_blocks, 0), 0)
    once = pl.Buffered(1)
    kern = functools.partial(_mix_kernel, lat_blocks=lat_blocks, d=d)
    return pl.pallas_call(
        kern,
        out_shape=(jax.ShapeDtypeStruct((t, d), F32), jax.ShapeDtypeStruct((t, d), F32),
                   jax.ShapeDtypeStruct((TOP_K, t), jnp.int32), jax.ShapeDtypeStruct((TOP_K, t), F32),
                   jax.ShapeDtypeStruct((TOP_K, t), jnp.int32), jax.ShapeDtypeStruct((N_EXPERTS, 1), jnp.int32)),
        grid=(t // tm,),
        scratch_shapes=[pltpu.VMEM((N_EXPERTS, 1), F32)],
        in_specs=[pl.BlockSpec((tm, sw), lat_row),
                  pl.BlockSpec((tm, sw), ctx_rowblk, pipeline_mode=once),
                  pl.BlockSpec((tm, sw), row),
                  pl.BlockSpec((tm, sw), lat_row),
                  pl.BlockSpec((tm, sw), ctx_rowblk, pipeline_mode=once),
                  pl.BlockSpec((tm, d), row),
                  pl.BlockSpec((None, 1, 6 * d), lambda i: (jnp.where(i < lat_blocks, i // per_b, ctx_row), 0, 0)),
                  pl.BlockSpec((1, sw), const),
                  pl.BlockSpec((None, sw, sw), lambda i: (layer, 0, 0), pipeline_mode=once),
                  pl.BlockSpec((1, sw), const),
                  pl.BlockSpec((None, sw, d), lambda i: (layer, 0, 0), pipeline_mode=once),
                  pl.BlockSpec((None, d - sw, d), lambda i: (layer, 1, 0), pipeline_mode=once),
                  pl.BlockSpec((1, d), const),
                  pl.BlockSpec((N_EXPERTS, d), const),
                  pl.BlockSpec((N_EXPERTS, 1), const)],
        out_specs=(pl.BlockSpec((tm, d), row), pl.BlockSpec((tm, d), row),
                   pl.BlockSpec((TOP_K, tm), lambda i: (0, i)), pl.BlockSpec((TOP_K, tm), lambda i: (0, i)),
                   pl.BlockSpec((TOP_K, tm), lambda i: (0, i)), pl.BlockSpec((N_EXPERTS, 1), const)),
        compiler_params=_cparams(("arbitrary",), 58),
        name="mix",
    )(ysl, ysc, pu, rl, rc, x, mods_cur, d_skip, glu_w, glu_b, w_out, w_out, g2, rw_t, rb)


def _dispatch_plan(eidx, rank, counts):
    bm = MOE_ROWS
    n_assign = eidx.size
    nblk = -(-n_assign // bm) + N_EXPERTS
    counts = counts.reshape(N_EXPERTS)
    padded = (counts + bm - 1) // bm * bm
    pend = jnp.cumsum(padded)
    pstart = pend - padded
    experts = jnp.arange(N_EXPERTS, dtype=jnp.int32)[:, None, None]
    dest = rank + jnp.sum(jnp.where(eidx[None] == experts, pstart[:, None, None], 0), axis=0)
    blk_start = jnp.arange(nblk, dtype=jnp.int32) * bm
    bexp = jnp.minimum(jnp.sum((pend[None, :] <= blk_start[:, None]).astype(jnp.int32), axis=1), N_EXPERTS - 1)
    nused = (pend[-1] // bm).astype(jnp.int32).reshape(1)
    return dest.astype(jnp.int32), bexp.astype(jnp.int32), nused, counts.astype(jnp.int32), pstart.astype(jnp.int32)


def _dispatch_kernel(cnt_ref, pst_ref, nused_ref, dest_ref, fu_ref, xs_hbm, zblk, sem, zsem, *, tb, bm, nblk):
    i = pl.program_id(0)

    def body(t8, _):
        for u in range(DMA_UNROLL):
            tt = t8 * DMA_UNROLL + u
            src = fu_ref.at[pl.ds(pl.multiple_of(t8 * DMA_UNROLL, DMA_UNROLL) + u, 1)]
            for k in range(TOP_K):
                pltpu.make_async_copy(src, xs_hbm.at[pl.ds(dest_ref[k, tt], 1)], sem).start(priority=k)
        return 0

    lax.fori_loop(0, tb // DMA_UNROLL, body, 0)

    @pl.when(i == 0)
    def _():
        zblk[...] = jnp.zeros_like(zblk)
        zrow = zblk.at[pl.ds(0, 1)]

        def pad_rows(e, _):
            first = pst_ref[e] + cnt_ref[e]
            npad = (bm - cnt_ref[e] % bm) % bm

            def start(r, _):
                pltpu.make_async_copy(zrow, xs_hbm.at[pl.ds(first + r, 1)], zsem).start()
                return 0

            def wait(r, _):
                pltpu.make_async_copy(zrow, xs_hbm.at[pl.ds(0, 1)], zsem).wait()
                return 0

            lax.fori_loop(0, npad, start, 0)
            lax.fori_loop(0, npad, wait, 0)
            return 0

        lax.fori_loop(0, N_EXPERTS, pad_rows, 0)

        def pad_block(j, _):
            cp = pltpu.make_async_copy(zblk, xs_hbm.at[pl.ds(pl.multiple_of(j * bm, bm), bm)], zsem)
            cp.start()
            cp.wait()
            return 0

        lax.fori_loop(nused_ref[0], nblk, pad_block, 0)

    for k in range(TOP_K):
        pltpu.make_async_copy(fu_ref, xs_hbm.at[pl.ds(0, tb)], sem).wait()


def _dispatch(fu, dest, counts, pstart, nused, *, nblk):
    n_tok, d = fu.shape
    bm = MOE_ROWS
    tb = 512
    assert n_tok % tb == 0
    kern = functools.partial(_dispatch_kernel, tb=tb, bm=bm, nblk=nblk)
    grid_spec = pltpu.PrefetchScalarGridSpec(
        num_scalar_prefetch=3,
        grid=(n_tok // tb,),
        in_specs=[pl.BlockSpec((TOP_K, tb), lambda i, c, p, n: (0, i), memory_space=pltpu.SMEM),
                  pl.BlockSpec((tb, d), lambda i, c, p, n: (i, 0))],
        out_specs=pl.BlockSpec(memory_space=pl.ANY),
        scratch_shapes=[pltpu.VMEM((bm, d), F32),
                        pltpu.SemaphoreType.DMA(()), pltpu.SemaphoreType.DMA(())],
    )
    return pl.pallas_call(
        kern,
        out_shape=jax.ShapeDtypeStruct((nblk * bm, d), F32),
        grid_spec=grid_spec,
        compiler_params=_cparams(("arbitrary",), 24),
        name="dispatch",
    )(counts, pstart, nused, dest, fu)


def _expert_kernel(bexp_ref, nused_ref, xs_ref, wg_ref, wu_ref, wd_ref, ys_ref):
    @pl.when(pl.program_id(0) < nused_ref[0])
    def _():
        xb = xs_ref[...].astype(BF16)
        gt = jnp.dot(xb, wg_ref[...], preferred_element_type=F32)
        up = jnp.dot(xb, wu_ref[...], preferred_element_type=F32)
        hmid = (gt * jax.nn.sigmoid(gt) * up).astype(BF16)
        ys_ref[...] = jnp.dot(hmid, wd_ref[...], preferred_element_type=F32)

    @pl.when(pl.program_id(0) >= nused_ref[0])
    def _():
        ys_ref[...] = jnp.zeros_like(ys_ref)


def _experts(xs, bexp, nused, wg, wu, wd, layer):
    bm = MOE_ROWS
    nblk = bexp.shape[0]
    d, ff = wg.shape[2], wg.shape[3]
    grid_spec = pltpu.PrefetchScalarGridSpec(
        num_scalar_prefetch=2,
        grid=(nblk,),
        in_specs=[pl.BlockSpec((bm, d), lambda j, be, nu: (j, 0)),
                  pl.BlockSpec((None, None, d, ff), lambda j, be, nu: (layer, be[j], 0, 0)),
                  pl.BlockSpec((None, None, d, ff), lambda j, be, nu: (layer, be[j], 0, 0)),
                  pl.BlockSpec((None, None, ff, d), lambda j, be, nu: (layer, be[j], 0, 0))],
        out_specs=pl.BlockSpec((bm, d), lambda j, be, nu: (j, 0)),
    )
    return pl.pallas_call(
        _expert_kernel,
        out_shape=jax.ShapeDtypeStruct((nblk * bm, d), F32),
        grid_spec=grid_spec,
        compiler_params=_cparams(("arbitrary",), 52),
        name="experts",
    )(bexp, nused, xs, wg, wu, wd)


def _rope_tables(n_lat):
    n_rows = n_lat // GRID_W
    pos = np.arange(n_lat)
    quarter = RET_HEAD_DIM // 4
    freqs = ROPE_BASE ** (-np.arange(quarter, dtype=np.float64) / quarter)
    ang = np.concatenate([(pos // GRID_W)[:, None] * freqs, (pos % GRID_W)[:, None] * freqs], axis=-1)
    assert ang.shape == (n_rows * GRID_W, RET_HEAD_DIM // 2)
    return jnp.asarray(np.cos(ang), F32), jnp.asarray(np.sin(ang), F32)


def kernel(x, c, ctx, c_ctx, mod_w, mod_b, norm1_g, norm2_g, w_in, s5_a_re, s5_a_im, s5_log_dt, s5_b_re, s5_b_im,
           s5_c_re, s5_c_im, s5_d, glu_w, glu_b, ret_decay_raw, w_out, router_w, router_b, exp_w_gate, exp_w_up,
           exp_w_down, final_norm_g):
    bsz, n_lat, d = x.shape
    n_ctx = ctx.shape[1]
    depth = mod_w.shape[0]
    s5w = s5_d.shape[1]
    t = bsz * (n_lat + n_ctx)
    lat_total = bsz * n_lat

    xs = (x.reshape(lat_total, d), ctx.reshape(bsz * n_ctx, d))
    c8 = jnp.zeros((8, d), F32).at[:bsz].set(c).at[bsz].set(c_ctx)
    mods = _mods(c8, mod_w, mod_b).reshape(depth, 8, 1, 6 * d)
    cos, sin = _rope_tables(n_lat)
    rw_t = router_w.T
    rb = router_b.reshape(N_EXPERTS, 1)

    w_in_b, glu_w_b, w_out_b = w_in.astype(BF16), glu_w.astype(BF16), w_out.astype(BF16)
    wg_b, wu_b, wd_b = exp_w_gate.astype(BF16), exp_w_up.astype(BF16), exp_w_down.astype(BF16)
    assert w_out.shape[1] == 2 * s5w
    w1, q, a16 = jax.vmap(_s5_weights)(s5_a_re, s5_a_im, s5_log_dt, s5_b_re, s5_b_im, s5_c_re, s5_c_im)

    moe = None
    for layer in range(depth):
        xs, h = _prenorm(xs, moe, mods[layer - 1] if layer else None, mods[layer], norm1_g[layer],
                         n_lat=lat_total, rows_per_batch=n_lat, final=False)
        pu, pr = _in_proj(h, w_in_b, layer, s5w)

        ysl, ysc = _s5_scan(pu, w1, q, a16, layer, bsz=bsz, n_lat=n_lat, n_ctx=n_ctx)

        log_gamma = jnp.log1p(-jnp.exp2(-ret_decay_raw[layer].astype(F32)))
        rl, rc = _retention(pr, log_gamma, cos, sin, bsz=bsz, n_lat=n_lat, n_ctx=n_ctx, col0=0)

        xs, fu, eidx, ew, rank, counts = _mix(
            ysl, ysc, pu, rl, rc, xs, mods[layer], s5_d[layer].reshape(1, s5w), glu_w_b,
            glu_b[layer].reshape(1, s5w), w_out_b, norm2_g[layer].reshape(1, d), rw_t, rb,
            layer=layer, n_lat=lat_total, rows_per_batch=n_lat)

        dest, bexp, nused, counts, pstart = _dispatch_plan(eidx, rank, counts)
        xsort = _dispatch(fu, dest, counts, pstart, nused, nblk=bexp.shape[0])
        ys = _experts(xsort, bexp, nused, wg_b, wu_b, wd_b, layer)
        moe = (ys, dest, ew.T)

    out = _prenorm(xs, moe, mods[depth - 1], mods[depth - 1], final_norm_g,
                   n_lat=lat_total, rows_per_batch=n_lat, final=True)
    return out.reshape(bsz, n_lat, d)
```

```python
import functools
import math

import jax
import jax.numpy as jnp
import numpy as np
from jax import lax
from jax.experimental import pallas as pl
from jax.experimental.pallas import tpu as pltpu

F32 = jnp.float32
BF16 = jnp.bfloat16
HIGHEST = lax.Precision.HIGHEST

GRID_W = 64
S5_GROUP = 16
S5_STATE = 64
S5_STEP = 16
RET_HEAD_DIM = 256
RET_BLOCK = 256
ROPE_BASE = 10000.0
N_EXPERTS = 16
N_EXPERT_GROUPS = 4
EXPERTS_PER_GROUP = 4
TOP_K = 2
NORM_EPS = 1e-6

ROW_BLOCK = 256
MIX_ROW_BLOCK = 512
MM_ROW_BLOCK = 1024
MM_COL_BLOCK = 1024
MOE_ROWS = 256
DMA_UNROLL = 8
S5_TILE_GROUPS = 128 // S5_GROUP
MIB = 1024 * 1024


def _cparams(sem, vmem_mib):
    return pltpu.CompilerParams(dimension_semantics=sem, vmem_limit_bytes=int(vmem_mib * MIB))


def _mods_kernel(c_ref, w_ref, b_ref, o_ref):
    cv = c_ref[...]
    act = cv * jax.nn.sigmoid(cv)
    o_ref[0] = jnp.dot(act.astype(BF16), w_ref[0].astype(BF16), preferred_element_type=F32) + b_ref[0]


def _mods(c8, mod_w, mod_b):
    depth, d, n = mod_w.shape
    tn = 1024
    return pl.pallas_call(
        _mods_kernel,
        out_shape=jax.ShapeDtypeStruct((depth, 8, n), F32),
        grid=(depth, n // tn),
        in_specs=[pl.BlockSpec((8, d), lambda l, j: (0, 0)),
                  pl.BlockSpec((1, d, tn), lambda l, j: (l, 0, j)),
                  pl.BlockSpec((1, 1, tn), lambda l, j: (l, 0, j))],
        out_specs=pl.BlockSpec((1, 8, tn), lambda l, j: (l, 0, j)),
        compiler_params=_cparams(("arbitrary", "arbitrary"), 40),
        name="adaln_mods",
    )(c8, mod_w, mod_b.reshape(depth, 1, n))


def _prenorm_kernel(*refs, combine, final, d, tm, nblk, lat_blocks):
    if combine:
        (dest0_ref, destn_ref, x_ref, ys_hbm, wt_ref, mprev_ref, mcur_ref, g_ref), rest = refs[:8], refs[8:]
        outs, (ybuf, sem) = rest[:-2], rest[-2:]
    else:
        (xl_ref, xc_ref, mcur_ref, g_ref), outs = refs[:4], refs[4:]
    if not combine:
        x = jnp.where(pl.program_id(0) < lat_blocks, xl_ref[...], xc_ref[...])
    else:
        x = x_ref[...]
        i = pl.program_id(0)
        slot = i % 2
        per_slot = TOP_K * tm

        def gather(dref, s):
            for tt in range(tm):
                for k in range(TOP_K):
                    src = ys_hbm.at[pl.ds(dref[k, tt], 1)]
                    dst = ybuf.at[pl.ds(s * per_slot + k * tm + tt, 1)]
                    pltpu.make_async_copy(src, dst, sem.at[s]).start(priority=k)

        @pl.when(i == 0)
        def _():
            gather(dest0_ref, 0)

        for s_next in range(2):
            @pl.when((i + 1 < nblk) & (slot == 1 - s_next))
            def _():
                gather(destn_ref, s_next)

        base = pl.multiple_of(slot * per_slot, per_slot)
        pltpu.make_async_copy(ys_hbm.at[pl.ds(0, per_slot)], ybuf.at[pl.ds(base, per_slot)], sem.at[slot]).wait()

        def rows_of(k):
            return ybuf[pl.ds(pl.multiple_of(base + k * tm, tm), tm), :]

        gate2 = mprev_ref[:, 5 * d:6 * d]
        x = x + gate2 * (wt_ref[:, 0:1] * rows_of(0) + wt_ref[:, 1:2] * rows_of(1))
    ms = jnp.mean(x * x, axis=-1, keepdims=True)
    xn = x * lax.rsqrt(ms + NORM_EPS) * g_ref[...]
    if final:
        outs[0][...] = xn
    else:
        outs[0][...] = x
        shift = mcur_ref[:, 0:d]
        scale = mcur_ref[:, d:2 * d]
        outs[1][...] = (xn * (1.0 + scale) + shift).astype(BF16)


def _prenorm(x, moe, mods_prev, mods_cur, gain, *, n_lat, rows_per_batch, final):
    combine = moe is not None
    assert combine != isinstance(x, tuple)
    t = x.shape[0] if combine else x[0].shape[0] + x[1].shape[0]
    d = x.shape[1] if combine else x[0].shape[1]
    tm = ROW_BLOCK
    nblk = (n_lat if final else t) // tm
    lat_blocks = n_lat // tm
    per_b = rows_per_batch // tm
    ctx_row = n_lat // rows_per_batch

    def mod_idx(i):
        return (jnp.where(i < lat_blocks, i // per_b, ctx_row), 0, 0)

    row_spec = pl.BlockSpec((tm, d), lambda i: (i, 0))
    mod_spec = pl.BlockSpec((None, 1, 6 * d), mod_idx)
    g_spec = pl.BlockSpec((1, d), lambda i: (0, 0))
    ins, specs, scratch = [], [], []
    if combine:
        ys, dest, wt = moe
        ins += [dest, dest]
        specs += [pl.BlockSpec((TOP_K, tm), lambda i: (0, 0), memory_space=pltpu.SMEM),
                  pl.BlockSpec((TOP_K, tm), lambda i: (0, jnp.minimum(i + 1, nblk - 1)), memory_space=pltpu.SMEM)]
    if combine:
        ins.append(x)
        specs.append(row_spec)
    else:
        ins += list(x)
        specs += [pl.BlockSpec((tm, d), lambda i: (jnp.minimum(i, lat_blocks - 1), 0)),
                  pl.BlockSpec((tm, d), lambda i: (jnp.maximum(i - lat_blocks, 0), 0))]
    if combine:
        ins += [ys, wt, mods_prev]
        specs += [pl.BlockSpec(memory_space=pl.ANY), pl.BlockSpec((tm, TOP_K), lambda i: (i, 0)), mod_spec]
        scratch = [pltpu.VMEM((2 * TOP_K * tm, d), F32), pltpu.SemaphoreType.DMA((2,))]
    ins += [mods_cur, gain.reshape(1, d)]
    specs += [mod_spec, g_spec]
    if final:
        out_shape = jax.ShapeDtypeStruct((n_lat, d), F32)
        out_specs = row_spec
    else:
        out_shape = (jax.ShapeDtypeStruct((t, d), F32), jax.ShapeDtypeStruct((t, d), BF16))
        out_specs = (row_spec, row_spec)
    return pl.pallas_call(
        functools.partial(_prenorm_kernel, combine=combine, final=final, d=d, tm=tm, nblk=nblk,
                          lat_blocks=lat_blocks),
        out_shape=out_shape, grid=(nblk,), in_specs=specs, out_specs=out_specs, scratch_shapes=scratch,
        compiler_params=_cparams(("arbitrary",), 40),
        name="prenorm_final" if final else ("prenorm_combine" if combine else "prenorm"),
    )(*ins)


def _inproj_kernel(h_ref, w_ref, pu_ref, pr_ref):
    acc = jnp.dot(h_ref[...], w_ref[...], preferred_element_type=F32)

    @pl.when(pl.program_id(1) == 0)
    def _():
        pu_ref[...] = acc

    @pl.when(pl.program_id(1) > 0)
    def _():
        pr_ref[...] = acc.astype(pr_ref.dtype)


def _in_proj(h, w, layer, s5w):
    t, k = h.shape
    n = w.shape[2]
    tm, tn = MM_ROW_BLOCK, MM_COL_BLOCK
    assert s5w == tn
    return pl.pallas_call(
        _inproj_kernel,
        out_shape=(jax.ShapeDtypeStruct((t, s5w), F32), jax.ShapeDtypeStruct((t, n - s5w), BF16)),
        grid=(t // tm, n // tn),
        in_specs=[pl.BlockSpec((tm, k), lambda i, j: (i, 0)),
                  pl.BlockSpec((None, k, tn), lambda i, j: (layer, 0, j))],
        out_specs=(pl.BlockSpec((tm, tn), lambda i, j: (i, 0)),
                   pl.BlockSpec((tm, tn), lambda i, j: (i, jnp.maximum(j - 1, 0)))),
        compiler_params=_cparams(("arbitrary", "arbitrary"), 40),
        name="in_proj",
    )(h, w)


def _s5_weights(a_re, a_im, log_dt, b_re, b_im, c_re, c_im):
    ns = S5_STEP
    a_re, a_im = a_re.astype(F32), a_im.astype(F32)
    dt = jnp.exp(log_dt.astype(F32))[..., None]
    lam_re, lam_im = dt * a_re, dt * a_im
    tau = jnp.arange(ns + 1, dtype=F32)[:, None, None, None]
    mag = jnp.exp(tau * lam_re[None])
    p_re, p_im = mag * jnp.cos(tau * lam_im[None]), mag * jnp.sin(tau * lam_im[None])
    x, y = p_re[1] - 1.0, p_im[1]
    den = a_re * a_re + a_im * a_im
    k_re, k_im = (x * a_re + y * a_im) / den, (y * a_re - x * a_im) / den
    b_re, b_im = b_re.astype(F32), b_im.astype(F32)
    bb_re = k_re[..., None] * b_re - k_im[..., None] * b_im
    bb_im = k_re[..., None] * b_im + k_im[..., None] * b_re
    c_re, c_im = c_re.astype(F32), c_im.astype(F32)
    cp_re = c_re[None] * p_re[:, :, :, None, :] - c_im[None] * p_im[:, :, :, None, :]
    cp_im = c_re[None] * p_im[:, :, :, None, :] + c_im[None] * p_re[:, :, :, None, :]
    kmat = jnp.einsum('tdgon,dgni->tdgoi', jnp.concatenate([cp_re, -cp_im], axis=-1),
                      jnp.concatenate([bb_re, bb_im], axis=-2))
    lag = np.arange(ns)[None, :] - np.arange(ns)[:, None]
    sel_f = jnp.asarray(lag[None] == np.arange(ns)[:, None, None], F32)
    sel_b = jnp.asarray(-lag[None] == np.arange(ns)[:, None, None], F32)
    g = a_re.shape[1]
    m = (jnp.einsum('lst,lgoi->gsito', sel_f, kmat[:ns, 0]) + jnp.einsum('lst,lgoi->gsito', sel_b, kmat[:ns, 1]))
    m = m.reshape(g, ns * S5_GROUP, ns * S5_GROUP)

    def inject(p_r, p_i, d):
        r = p_r[:, :, :, None] * bb_re[d][None] - p_i[:, :, :, None] * bb_im[d][None]
        i = p_r[:, :, :, None] * bb_im[d][None] + p_i[:, :, :, None] * bb_re[d][None]
        tr = lambda z: jnp.transpose(z, (1, 0, 3, 2)).reshape(g, ns * S5_GROUP, S5_STATE)
        return tr(r), tr(i)

    pf_re, pf_im = inject(p_re[ns - 1::-1, 0], p_im[ns - 1::-1, 0], 0)
    pb_re, pb_im = inject(p_re[:ns, 1], p_im[:ns, 1], 1)
    w1 = jnp.concatenate([z.astype(BF16) for z in (m, pf_re, pb_re, pf_im, pb_im)], axis=-1)

    def carry(cp_t):
        return jnp.transpose(cp_t, (1, 3, 0, 2)).reshape(g, S5_STATE, ns * S5_GROUP)

    up = slice(1, ns + 1)
    down = slice(ns, 0, -1)
    q = jnp.concatenate([carry(cp_re[up, 0]).astype(BF16), carry(cp_re[down, 1]).astype(BF16),
                         (-carry(cp_im[up, 0])).astype(BF16), (-carry(cp_im[down, 1])).astype(BF16)],
                        axis=1)
    a16 = jnp.stack([jnp.concatenate([p_re[ns, 0], p_re[ns, 1]], axis=-1),
                     jnp.concatenate([p_im[ns, 0], p_im[ns, 1]], axis=-1)], axis=1)
    return w1.astype(BF16), q.astype(BF16), a16


def _group_transpose(xs):
    ng = len(xs)
    grp = lax.broadcasted_iota(jnp.int32, xs[0].shape, 1) // S5_GROUP
    rolled = []
    for j in range(ng):
        y = xs[j % ng]
        for k in range(1, ng):
            y = jnp.where(grp == k, xs[(k + j) % ng], y)
        rolled.append(y if j == 0 else pltpu.roll(y, S5_GROUP * j, 1))
    outs = []
    for a in range(ng):
        o = rolled[(-a) % ng]
        for b in range(1, ng):
            o = jnp.where(grp == b, rolled[(b - a) % ng], o)
        outs.append(o)
    return outs


def _s5_kernel(ul_ref, uc_ref, w1_ref, q_ref, a_ref, yl_ref, yc_ref, uflat, st_re, st_im, yfl, *,
               lat_chunks, ctx_chunks):
    ng = S5_TILE_GROUPS
    width = S5_STEP * S5_GROUP
    n_chunks = lat_chunks + ctx_chunks
    sub = 32

    def flatten(src_ref, c0, row0, n):
        for hh in range(2):
            xs = [src_ref[pl.ds(row0 + hh * 8 + t, n, stride=S5_STEP), :] for t in range(8)]
            outs = _group_transpose(xs)
            for k in range(ng):
                uflat[k, pl.ds(c0, n), hh * 128:(hh + 1) * 128] = outs[k].astype(BF16)

    def flat_body(i, _):
        c0 = pl.multiple_of(i * sub, sub)
        flatten(ul_ref, c0, c0 * S5_STEP, sub)
        return 0

    lax.fori_loop(0, lat_chunks // sub, flat_body, 0)
    flatten(uc_ref, lat_chunks, 0, ctx_chunks)

    for k in range(ng):
        z = jnp.dot(uflat[k], w1_ref[k], preferred_element_type=F32)
        yfl[k] = z[:, :width]
        st_re[pl.ds(k, n_chunks, stride=ng), :] = z[:, width:width + 128]
        st_im[pl.ds(k, n_chunks, stride=ng), :] = z[:, width + 128:]

    lane = lax.broadcasted_iota(jnp.int32, (ng, 128), 1)
    fwd_lane = lane < S5_STATE
    a_re = a_ref[0]
    a_im = a_ref[1]

    def make_body(base, n):
        def body(i, carry):
            h_re, h_im = carry
            rf = pl.multiple_of((base + i) * ng, ng)
            rb = pl.multiple_of((base + n - 1 - i) * ng, ng)
            tf_re, tf_im = st_re[pl.ds(rf, ng), :], st_im[pl.ds(rf, ng), :]
            tb_re, tb_im = st_re[pl.ds(rb, ng), :], st_im[pl.ds(rb, ng), :]
            t_re = jnp.where(fwd_lane, tf_re, tb_re)
            t_im = jnp.where(fwd_lane, tf_im, tb_im)
            n_re = a_re * h_re - a_im * h_im + t_re
            n_im = a_re * h_im + a_im * h_re + t_im
            st_re[pl.ds(rf, ng), :] = jnp.where(fwd_lane, h_re, tf_re)
            st_im[pl.ds(rf, ng), :] = jnp.where(fwd_lane, h_im, tf_im)
            st_re[pl.ds(rb, ng), :] = jnp.where(fwd_lane, tb_re, h_re)
            st_im[pl.ds(rb, ng), :] = jnp.where(fwd_lane, tb_im, h_im)
            return n_re, n_im
        return body

    carry = (jnp.zeros((ng, 128), F32), jnp.zeros((ng, 128), F32))
    carry = lax.fori_loop(0, ctx_chunks, make_body(lat_chunks, ctx_chunks), carry)
    carry = lax.fori_loop(0, lat_chunks, make_body(0, lat_chunks), carry)

    for k in range(ng):
        hk = jnp.concatenate([st_re[pl.ds(k, n_chunks, stride=ng), :], st_im[pl.ds(k, n_chunks, stride=ng), :]],
                             axis=1).astype(BF16)
        yfl[k] = yfl[k] + jnp.dot(hk, q_ref[k], preferred_element_type=F32)

    def unflatten(dst_ref, c0, row0, n):
        for hh in range(2):
            vs = [yfl[k, pl.ds(c0, n), hh * 128:(hh + 1) * 128] for k in range(ng)]
            ts = _group_transpose(vs)
            for t in range(8):
                dst_ref[pl.ds(row0 + hh * 8 + t, n, stride=S5_STEP), :] = ts[t]

    def unflat_body(i, _):
        c0 = pl.multiple_of(i * sub, sub)
        unflatten(yl_ref, c0, c0 * S5_STEP, sub)
        return 0

    lax.fori_loop(0, lat_chunks // sub, unflat_body, 0)
    unflatten(yc_ref, lat_chunks, 0, ctx_chunks)


def _s5_scan(pu, w1, q, a16, layer, *, bsz, n_lat, n_ctx):
    g = w1.shape[1]
    ng = S5_TILE_GROUPS
    width = S5_STEP * S5_GROUP
    lat_chunks, ctx_chunks = n_lat // S5_STEP, n_ctx // S5_STEP
    assert lat_chunks % 32 == 0 and ctx_chunks % 2 == 0 and ctx_chunks % 16 == 0
    n_chunks = lat_chunks + ctx_chunks
    ctx_blk0 = bsz * n_lat // n_ctx
    a_t = jnp.transpose(a16.reshape(-1, g // ng, ng, 2, 128), (0, 1, 3, 2, 4))
    kern = functools.partial(_s5_kernel, lat_chunks=lat_chunks, ctx_chunks=ctx_chunks)
    return pl.pallas_call(
        kern,
        out_shape=(jax.ShapeDtypeStruct((bsz * n_lat, g * S5_GROUP), F32),
                   jax.ShapeDtypeStruct((bsz * n_ctx, g * S5_GROUP), F32)),
        grid=(g // ng, bsz),
        in_specs=[pl.BlockSpec((n_lat, 128), lambda i, b: (b, i)),
                  pl.BlockSpec((n_ctx, 128), lambda i, b: (ctx_blk0 + b, i)),
                  pl.BlockSpec((None, ng, width, 2 * width), lambda i, b: (layer, i, 0, 0)),
                  pl.BlockSpec((None, ng, width, width), lambda i, b: (layer, i, 0, 0)),
                  pl.BlockSpec((None, None, 2, ng, 128), lambda i, b: (layer, i, 0, 0, 0))],
        out_specs=(pl.BlockSpec((n_lat, 128), lambda i, b: (b, i)),
                   pl.BlockSpec((n_ctx, 128), lambda i, b: (b, i))),
        scratch_shapes=[pltpu.VMEM((ng, n_chunks, width), BF16),
                        pltpu.VMEM((n_chunks * ng, 128), F32),
                        pltpu.VMEM((n_chunks * ng, 128), F32),
                        pltpu.VMEM((ng, n_chunks, width), F32)],
        compiler_params=_cparams(("arbitrary", "arbitrary"), 40),
        name="s5_scan",
    )(pu, pu, w1, q, a_t)


def _ret_kernel(lg_ref, ql_ref, kl_ref, vl_ref, gl_ref, qc_ref, kc_ref, vc_ref, gc_ref, cos_ref, sin_ref,
                ol_ref, oc_ref, qs, ks, vs, acc, sf, sb, dm, *, n_ctx, n_lat):
    c = RET_BLOCK
    half = RET_HEAD_DIM // 2
    head = pl.program_id(1)
    lgf = lg_ref[0, head]
    lgb = lg_ref[1, head]
    qscale = RET_HEAD_DIM ** -0.5
    ctx_rows = n_ctx * c

    ii = lax.broadcasted_iota(jnp.int32, (c, c), 0)
    jj = lax.broadcasted_iota(jnp.int32, (c, c), 1)
    rel = (ii - jj).astype(F32)
    dsum = (jnp.where(rel >= 0, jnp.exp(lgf * jnp.maximum(rel, 0.0)), 0.0)
            + jnp.where(rel <= 0, jnp.exp(lgb * jnp.maximum(-rel, 0.0)), 0.0))
    pos = lax.broadcasted_iota(jnp.int32, (c, 1), 0).astype(F32)
    qdec_f = jnp.exp(lgf * (pos + 1.0))
    kdec_f = jnp.exp(lgf * (c - 1.0 - pos))
    qdec_b = jnp.exp(lgb * (c - pos))
    kdec_b = jnp.exp(lgb * pos)
    cg_f = jnp.exp(lgf * c)
    cg_b = jnp.exp(lgb * c)

    qs[0:ctx_rows, :] = (qc_ref[...].astype(F32) * qscale).astype(BF16)
    ks[0:ctx_rows, :] = kc_ref[...]
    vs[0:ctx_rows, :] = vc_ref[...]

    def rope_body(n, _):
        r = pl.multiple_of(n * c, c)
        cs = cos_ref[pl.ds(r, c), :]
        sn = sin_ref[pl.ds(r, c), :]

        def rot(t):
            t1, t2 = t[:, :half], t[:, half:]
            return jnp.concatenate([t1 * cs - t2 * sn, t2 * cs + t1 * sn], axis=1)

        dst = pl.multiple_of(ctx_rows + r, c)
        qs[pl.ds(dst, c), :] = (rot(ql_ref[pl.ds(r, c), :].astype(F32)) * qscale).astype(BF16)
        ks[pl.ds(dst, c), :] = rot(kl_ref[pl.ds(r, c), :].astype(F32)).astype(BF16)
        vs[pl.ds(dst, c), :] = vl_ref[pl.ds(r, c), :]
        return 0

    lax.fori_loop(0, n_lat, rope_body, 0)

    dm[...] = dsum
    sf[...] = jnp.zeros_like(sf)
    sb[...] = jnp.zeros_like(sb)

    def load(r):
        return qs[pl.ds(r, c), :], ks[pl.ds(r, c), :], vs[pl.ds(r, c), :]

    def intra_of(q, k, v):
        s = lax.dot_general(q, k, (((1,), (1,)), ((), ())), preferred_element_type=F32)
        return jnp.dot((s * dm[...]).astype(BF16), v, preferred_element_type=F32)

    def step(s_ref, q, k, v, qdec, kdec, cg):
        inter = jnp.dot(q, s_ref[...].astype(BF16), preferred_element_type=F32) * qdec
        kd = (k.astype(F32) * kdec).astype(BF16)
        s_ref[...] = cg * s_ref[...] + lax.dot_general(kd, v, (((0,), (0,)), ((), ())),
                                                       preferred_element_type=F32)
        return inter

    def finish(o, gate, out_ref, ro):
        o = o * lax.rsqrt(jnp.mean(o * o, axis=-1, keepdims=True) + NORM_EPS)
        out_ref[pl.ds(ro, c), :] = (o * (gate * jax.nn.sigmoid(gate))).astype(out_ref.dtype)

    fwd = (sf, qdec_f, kdec_f, cg_f)
    bwd = (sb, qdec_b, kdec_b, cg_b)

    for n in range(n_ctx):
        q, k, v = load(n * c)
        acc[n * c:(n + 1) * c, :] = intra_of(q, k, v) + step(fwd[0], q, k, v, *fwd[1:])
    for n in range(n_ctx - 1, -1, -1):
        q, k, v = load(n * c)
        o = acc[n * c:(n + 1) * c, :] + step(bwd[0], q, k, v, *bwd[1:])
        finish(o, gc_ref[n * c:(n + 1) * c, :].astype(F32), oc_ref, n * c)

    def first_visits(i, _):
        rf = pl.multiple_of(i * c, c)
        rb = pl.multiple_of((n_lat - 1 - i) * c, c)
        q, k, v = load(ctx_rows + rf)
        acc[pl.ds(ctx_rows + rf, c), :] = intra_of(q, k, v) + step(fwd[0], q, k, v, *fwd[1:])
        q, k, v = load(ctx_rows + rb)
        acc[pl.ds(ctx_rows + rb, c), :] = step(bwd[0], q, k, v, *bwd[1:])
        return 0

    def second_visits(i, _):
        rf = pl.multiple_of(i * c, c)
        rb = pl.multiple_of((n_lat - 1 - i) * c, c)
        q, k, v = load(ctx_rows + rf)
        o = acc[pl.ds(ctx_rows + rf, c), :] + intra_of(q, k, v) + step(fwd[0], q, k, v, *fwd[1:])
        finish(o, gl_ref[pl.ds(rf, c), :].astype(F32), ol_ref, rf)
        q, k, v = load(ctx_rows + rb)
        o = acc[pl.ds(ctx_rows + rb, c), :] + step(bwd[0], q, k, v, *bwd[1:])
        finish(o, gl_ref[pl.ds(rb, c), :].astype(F32), ol_ref, rb)
        return 0

    lax.fori_loop(0, n_lat // 2, first_visits, 0)
    lax.fori_loop(n_lat // 2, n_lat, second_visits, 0)


def _retention(p, log_gamma, cos, sin, *, bsz, n_lat, n_ctx, col0):
    t = p.shape[0]
    dh = RET_HEAD_DIM
    heads = (p.shape[1] - col0) // (4 * dh)
    qb, kb, vb, gb = (col0 // dh + i * heads for i in range(4))
    ctx_blk0 = bsz * n_lat // n_ctx

    def lat(cb):
        return pl.BlockSpec((n_lat, dh), lambda b, h: (b, cb + h))

    def ctx(cb):
        return pl.BlockSpec((n_ctx, dh), lambda b, h: (ctx_blk0 + b, cb + h))

    tab = pl.BlockSpec((n_lat, dh // 2), lambda b, h: (0, 0))
    assert n_ctx % RET_BLOCK == 0 and (n_lat // RET_BLOCK) % 2 == 0
    kern = functools.partial(_ret_kernel, n_ctx=n_ctx // RET_BLOCK, n_lat=n_lat // RET_BLOCK)
    seq = n_ctx + n_lat
    return pl.pallas_call(
        kern,
        out_shape=(jax.ShapeDtypeStruct((bsz * n_lat, heads * dh), BF16),
                   jax.ShapeDtypeStruct((bsz * n_ctx, heads * dh), BF16)),
        grid=(bsz, heads),
        in_specs=[pl.BlockSpec(memory_space=pltpu.SMEM),
                  lat(qb), lat(kb), lat(vb), lat(gb), ctx(qb), ctx(kb), ctx(vb), ctx(gb), tab, tab],
        out_specs=(pl.BlockSpec((n_lat, dh), lambda b, h: (b, h)),
                   pl.BlockSpec((n_ctx, dh), lambda b, h: (b, h))),
        scratch_shapes=[pltpu.VMEM((seq, dh), BF16), pltpu.VMEM((seq, dh), BF16), pltpu.VMEM((seq, dh), BF16),
                        pltpu.VMEM((seq, dh), F32), pltpu.VMEM((dh, dh), F32), pltpu.VMEM((dh, dh), F32),
                        pltpu.VMEM((RET_BLOCK, RET_BLOCK), F32)],
        compiler_params=_cparams(("arbitrary", "arbitrary"), 52),
        name="retention",
    )(log_gamma, p, p, p, p, p, p, p, p, cos, sin)


def _top2_sum(a, b, c, d):
    hi1, lo1 = jnp.maximum(a, b), jnp.minimum(a, b)
    hi2, lo2 = jnp.maximum(c, d), jnp.minimum(c, d)
    return jnp.maximum(hi1, hi2) + jnp.maximum(jnp.minimum(hi1, hi2), jnp.maximum(lo1, lo2))


def _route(logits_t, rb):
    sc = jax.nn.sigmoid(logits_t)
    biased = sc + rb
    per = EXPERTS_PER_GROUP
    brow = [biased[e:e + 1, :] for e in range(N_EXPERTS)]
    srow = [sc[e:e + 1, :] for e in range(N_EXPERTS)]
    gscore = [_top2_sum(*brow[per * g:per * g + per]) for g in range(N_EXPERT_GROUPS)]
    best_v = gscore[0]
    best_g = jnp.zeros_like(best_v, dtype=jnp.int32)
    for g in range(1, N_EXPERT_GROUPS):
        upd = gscore[g] > best_v
        best_v = jnp.where(upd, gscore[g], best_v)
        best_g = jnp.where(upd, g, best_g)
    vals, sels = [], []
    for j in range(per):
        v = brow[j]
        s = srow[j]
        for g in range(1, N_EXPERT_GROUPS):
            v = jnp.where(best_g == g, brow[per * g + j], v)
            s = jnp.where(best_g == g, srow[per * g + j], s)
        vals.append(v)
        sels.append(s)
    v1, i1, s1 = vals[0], jnp.zeros_like(best_g), sels[0]
    for j in range(1, per):
        upd = vals[j] > v1
        v1 = jnp.where(upd, vals[j], v1)
        s1 = jnp.where(upd, sels[j], s1)
        i1 = jnp.where(upd, j, i1)
    v2 = jnp.full_like(v1, -jnp.inf)
    i2, s2 = jnp.zeros_like(best_g), jnp.zeros_like(s1)
    for j in range(per):
        upd = (i1 != j) & (vals[j] > v2)
        v2 = jnp.where(upd, vals[j], v2)
        s2 = jnp.where(upd, sels[j], s2)
        i2 = jnp.where(upd, j, i2)
    tot = s1 + s2
    idx = jnp.concatenate([best_g * per + i1, best_g * per + i2], axis=0)
    wts = jnp.concatenate([s1 / tot, s2 / tot], axis=0)
    return idx, wts


def _mix_kernel(ysl_ref, ysc_ref, u_ref, rl_ref, rc_ref, x_ref, m_ref, dsk_ref, gw_ref, gb_ref, ws_ref, wr_ref,
                g2_ref, rw_ref, rb_ref, xo_ref, fu_ref, ei_ref, ew_ref, rank_ref, cnt_ref, run_ref, *, lat_blocks, d):
    is_lat = pl.program_id(0) < lat_blocks
    ys = jnp.where(is_lat, ysl_ref[...], ysc_ref[...])
    y = ys + dsk_ref[...] * u_ref[...]
    y = jax.nn.gelu(y, approximate=True)
    z = jnp.dot(y.astype(BF16), gw_ref[...], preferred_element_type=F32) + gb_ref[...]
    s5o = (y * jax.nn.sigmoid(z)).astype(BF16)
    r = jnp.where(is_lat, rl_ref[...], rc_ref[...])
    mixed = (jnp.dot(s5o, ws_ref[...], preferred_element_type=F32)
             + jnp.dot(r, wr_ref[...], preferred_element_type=F32))
    gate1 = m_ref[:, 2 * d:3 * d]
    x = x_ref[...] + gate1 * mixed
    xo_ref[...] = x
    ms = jnp.mean(x * x, axis=-1, keepdims=True)
    f = x * lax.rsqrt(ms + NORM_EPS) * g2_ref[...]
    f = f * (1.0 + m_ref[:, 4 * d:5 * d]) + m_ref[:, 3 * d:4 * d]
    tm = f.shape[0]
    fu_ref[...] = f
    logits_t = lax.dot_general(rw_ref[...], f, (((1,), (1,)), ((), ())), precision=HIGHEST,
                               preferred_element_type=F32)
    idx, wts = _route(logits_t, rb_ref[...])
    ei_ref[...] = idx
    ew_ref[...] = wts

    @pl.when(pl.program_id(0) == 0)
    def _():
        run_ref[...] = jnp.zeros_like(run_ref)

    e_iota = lax.broadcasted_iota(jnp.int32, (N_EXPERTS, tm), 0)
    tri = jnp.where(lax.broadcasted_iota(jnp.int32, (tm, tm), 0) <= lax.broadcasted_iota(jnp.int32, (tm, tm), 1),
                    1.0, 0.0).astype(BF16)
    hit0 = e_iota == idx[0:1, :]
    hit1 = e_iota == idx[1:2, :]
    p0 = jnp.dot(jnp.where(hit0, 1.0, 0.0).astype(BF16), tri, preferred_element_type=F32)
    p1 = jnp.dot(jnp.where(hit1, 1.0, 0.0).astype(BF16), tri, preferred_element_type=F32)
    tot0 = p0[:, tm - 1:tm]
    tot1 = p1[:, tm - 1:tm]
    before = run_ref[...]
    r0 = jnp.sum(jnp.where(hit0, before + p0, 0.0), axis=0, keepdims=True) - 1.0
    r1 = jnp.sum(jnp.where(hit1, before + tot0 + p1, 0.0), axis=0, keepdims=True) - 1.0
    rank_ref[...] = jnp.concatenate([r0, r1], axis=0).astype(jnp.int32)
    after = before + tot0 + tot1
    run_ref[...] = after
    cnt_ref[...] = after.astype(jnp.int32)


def _mix(ysl, ysc, pu, rl, rc, x, mods_cur, d_skip, glu_w, glu_b, w_out, g2, rw_t, rb, *, layer, n_lat,
         rows_per_batch):
    t, d = x.shape
    sw = pu.shape[1]
    tm = MIX_ROW_BLOCK
    lat_blocks = n_lat // tm
    per_b = rows_per_batch // tm
    ctx_row = n_lat // rows_per_batch
    const = lambda i: (0, 0)
    row = lambda i: (i, 0)
    lat_row = lambda i: (jnp.minimum(i, lat_blocks - 1), 0)
    ctx_rowblk = lambda i: (jnp.maximum(i - lat_blocks, 0), 0)
    once = pl.Buffered(1)
    kern = functools.partial(_mix_kernel, lat_blocks=lat_blocks, d=d)
    return pl.pallas_call(
        kern,
        out_shape=(jax.ShapeDtypeStruct((t, d), F32), jax.ShapeDtypeStruct((t, d), F32),
                   jax.ShapeDtypeStruct((TOP_K, t), jnp.int32), jax.ShapeDtypeStruct((TOP_K, t), F32),
                   jax.ShapeDtypeStruct((TOP_K, t), jnp.int32), jax.ShapeDtypeStruct((N_EXPERTS, 1), jnp.int32)),
        grid=(t // tm,),
        scratch_shapes=[pltpu.VMEM((N_EXPERTS, 1), F32)],
        in_specs=[pl.BlockSpec((tm, sw), lat_row),
                  pl.BlockSpec((tm, sw), ctx_rowblk, pipeline_mode=once),
                  pl.BlockSpec((tm, sw), row),
                  pl.BlockSpec((tm, sw), lat_row),
                  pl.BlockSpec((tm, sw), ctx_rowblk, pipeline_mode=once),
                  pl.BlockSpec((tm, d), row),
                  pl.BlockSpec((None, 1, 6 * d), lambda i: (jnp.where(i < lat_blocks, i // per_b, ctx_row), 0, 0)),
                  pl.BlockSpec((1, sw), const),
                  pl.BlockSpec((None, sw, sw), lambda i: (layer, 0, 0), pipeline_mode=once),
                  pl.BlockSpec((1, sw), const),
                  pl.BlockSpec((None, sw, d), lambda i: (layer, 0, 0), pipeline_mode=once),
                  pl.BlockSpec((None, d - sw, d), lambda i: (layer, 1, 0), pipeline_mode=once),
                  pl.BlockSpec((1, d), const),
                  pl.BlockSpec((N_EXPERTS, d), const),
                  pl.BlockSpec((N_EXPERTS, 1), const)],
        out_specs=(pl.BlockSpec((tm, d), row), pl.BlockSpec((tm, d), row),
                   pl.BlockSpec((TOP_K, tm), lambda i: (0, i)), pl.BlockSpec((TOP_K, tm), lambda i: (0, i)),
                   pl.BlockSpec((TOP_K, tm), lambda i: (0, i)), pl.BlockSpec((N_EXPERTS, 1), const)),
        compiler_params=_cparams(("arbitrary",), 58),
        name="mix",
    )(ysl, ysc, pu, rl, rc, x, mods_cur, d_skip, glu_w, glu_b, w_out, w_out, g2, rw_t, rb)


def _dispatch_plan(eidx, rank, counts):
    bm = MOE_ROWS
    n_assign = eidx.size
    nblk = -(-n_assign // bm) + N_EXPERTS
    counts = counts.reshape(N_EXPERTS)
    padded = (counts + bm - 1) // bm * bm
    pend = jnp.cumsum(padded)
    pstart = pend - padded
    experts = jnp.arange(N_EXPERTS, dtype=jnp.int32)[:, None, None]
    dest = rank + jnp.sum(jnp.where(eidx[None] == experts, pstart[:, None, None], 0), axis=0)
    blk_start = jnp.arange(nblk, dtype=jnp.int32) * bm
    bexp = jnp.minimum(jnp.sum((pend[None, :] <= blk_start[:, None]).astype(jnp.int32), axis=1), N_EXPERTS - 1)
    nused = (pend[-1] // bm).astype(jnp.int32).reshape(1)
    return dest.astype(jnp.int32), bexp.astype(jnp.int32), nused, counts.astype(jnp.int32), pstart.astype(jnp.int32)


def _dispatch_kernel(cnt_ref, pst_ref, nused_ref, dest_ref, fu_ref, xs_hbm, zblk, sem, zsem, *, tb, bm, nblk):
    i = pl.program_id(0)

    def body(t8, _):
        for u in range(DMA_UNROLL):
            tt = t8 * DMA_UNROLL + u
            src = fu_ref.at[pl.ds(pl.multiple_of(t8 * DMA_UNROLL, DMA_UNROLL) + u, 1)]
            for k in range(TOP_K):
                pltpu.make_async_copy(src, xs_hbm.at[pl.ds(dest_ref[k, tt], 1)], sem).start(priority=k)
        return 0

    lax.fori_loop(0, tb // DMA_UNROLL, body, 0)

    @pl.when(i == 0)
    def _():
        zblk[...] = jnp.zeros_like(zblk)
        zrow = zblk.at[pl.ds(0, 1)]

        def pad_rows(e, _):
            first = pst_ref[e] + cnt_ref[e]
            npad = (bm - cnt_ref[e] % bm) % bm

            def start(r, _):
                pltpu.make_async_copy(zrow, xs_hbm.at[pl.ds(first + r, 1)], zsem).start()
                return 0

            def wait(r, _):
                pltpu.make_async_copy(zrow, xs_hbm.at[pl.ds(0, 1)], zsem).wait()
                return 0

            lax.fori_loop(0, npad, start, 0)
            lax.fori_loop(0, npad, wait, 0)
            return 0

        lax.fori_loop(0, N_EXPERTS, pad_rows, 0)

        def pad_block(j, _):
            cp = pltpu.make_async_copy(zblk, xs_hbm.at[pl.ds(pl.multiple_of(j * bm, bm), bm)], zsem)
            cp.start()
            cp.wait()
            return 0

        lax.fori_loop(nused_ref[0], nblk, pad_block, 0)

    for k in range(TOP_K):
        pltpu.make_async_copy(fu_ref, xs_hbm.at[pl.ds(0, tb)], sem).wait()


def _dispatch(fu, dest, counts, pstart, nused, *, nblk):
    n_tok, d = fu.shape
    bm = MOE_ROWS
    tb = 512
    assert n_tok % tb == 0
    kern = functools.partial(_dispatch_kernel, tb=tb, bm=bm, nblk=nblk)
    grid_spec = pltpu.PrefetchScalarGridSpec(
        num_scalar_prefetch=3,
        grid=(n_tok // tb,),
        in_specs=[pl.BlockSpec((TOP_K, tb), lambda i, c, p, n: (0, i), memory_space=pltpu.SMEM),
                  pl.BlockSpec((tb, d), lambda i, c, p, n: (i, 0))],
        out_specs=pl.BlockSpec(memory_space=pl.ANY),
        scratch_shapes=[pltpu.VMEM((bm, d), F32),
                        pltpu.SemaphoreType.DMA(()), pltpu.SemaphoreType.DMA(())],
    )
    return pl.pallas_call(
        kern,
        out_shape=jax.ShapeDtypeStruct((nblk * bm, d), F32),
        grid_spec=grid_spec,
        compiler_params=_cparams(("arbitrary",), 24),
        name="dispatch",
    )(counts, pstart, nused, dest, fu)


def _expert_kernel(bexp_ref, nused_ref, xs_ref, wg_ref, wu_ref, wd_ref, ys_ref):
    @pl.when(pl.program_id(0) < nused_ref[0])
    def _():
        xb = xs_ref[...].astype(BF16)
        gt = jnp.dot(xb, wg_ref[...], preferred_element_type=F32)
        up = jnp.dot(xb, wu_ref[...], preferred_element_type=F32)
        hmid = (gt * jax.nn.sigmoid(gt) * up).astype(BF16)
        ys_ref[...] = jnp.dot(hmid, wd_ref[...], preferred_element_type=F32)

    @pl.when(pl.program_id(0) >= nused_ref[0])
    def _():
        ys_ref[...] = jnp.zeros_like(ys_ref)


def _experts(xs, bexp, nused, wg, wu, wd, layer):
    bm = MOE_ROWS
    nblk = bexp.shape[0]
    d, ff = wg.shape[2], wg.shape[3]
    grid_spec = pltpu.PrefetchScalarGridSpec(
        num_scalar_prefetch=2,
        grid=(nblk,),
        in_specs=[pl.BlockSpec((bm, d), lambda j, be, nu: (j, 0)),
                  pl.BlockSpec((None, None, d, ff), lambda j, be, nu: (layer, be[j], 0, 0)),
                  pl.BlockSpec((None, None, d, ff), lambda j, be, nu: (layer, be[j], 0, 0)),
                  pl.BlockSpec((None, None, ff, d), lambda j, be, nu: (layer, be[j], 0, 0))],
        out_specs=pl.BlockSpec((bm, d), lambda j, be, nu: (j, 0)),
    )
    return pl.pallas_call(
        _expert_kernel,
        out_shape=jax.ShapeDtypeStruct((nblk * bm, d), F32),
        grid_spec=grid_spec,
        compiler_params=_cparams(("arbitrary",), 52),
        name="experts",
    )(bexp, nused, xs, wg, wu, wd)


def _rope_tables(n_lat):
    n_rows = n_lat // GRID_W
    pos = np.arange(n_lat)
    quarter = RET_HEAD_DIM // 4
    freqs = ROPE_BASE ** (-np.arange(quarter, dtype=np.float64) / quarter)
    ang = np.concatenate([(pos // GRID_W)[:, None] * freqs, (pos % GRID_W)[:, None] * freqs], axis=-1)
    assert ang.shape == (n_rows * GRID_W, RET_HEAD_DIM // 2)
    return jnp.asarray(np.cos(ang), F32), jnp.asarray(np.sin(ang), F32)


def kernel(x, c, ctx, c_ctx, mod_w, mod_b, norm1_g, norm2_g, w_in, s5_a_re, s5_a_im, s5_log_dt, s5_b_re, s5_b_im,
           s5_c_re, s5_c_im, s5_d, glu_w, glu_b, ret_decay_raw, w_out, router_w, router_b, exp_w_gate, exp_w_up,
           exp_w_down, final_norm_g):
    bsz, n_lat, d = x.shape
    n_ctx = ctx.shape[1]
    depth = mod_w.shape[0]
    s5w = s5_d.shape[1]
    t = bsz * (n_lat + n_ctx)
    lat_total = bsz * n_lat

    xs = (x.reshape(lat_total, d), ctx.reshape(bsz * n_ctx, d))
    c8 = jnp.zeros((8, d), F32).at[:bsz].set(c).at[bsz].set(c_ctx)
    mods = _mods(c8, mod_w, mod_b).reshape(depth, 8, 1, 6 * d)
    cos, sin = _rope_tables(n_lat)
    rw_t = router_w.T
    rb = router_b.reshape(N_EXPERTS, 1)

    w_in_b, glu_w_b, w_out_b = w_in.astype(BF16), glu_w.astype(BF16), w_out.astype(BF16)
    wg_b, wu_b, wd_b = exp_w_gate.astype(BF16), exp_w_up.astype(BF16), exp_w_down.astype(BF16)
    assert w_out.shape[1] == 2 * s5w
    w1, q, a16 = jax.vmap(_s5_weights)(s5_a_re, s5_a_im, s5_log_dt, s5_b_re, s5_b_im, s5_c_re, s5_c_im)

    moe = None
    for layer in range(depth):
        xs, h = _prenorm(xs, moe, mods[layer - 1] if layer else None, mods[layer], norm1_g[layer],
                         n_lat=lat_total, rows_per_batch=n_lat, final=False)
        pu, pr = _in_proj(h, w_in_b, layer, s5w)

        ysl, ysc = _s5_scan(pu, w1, q, a16, layer, bsz=bsz, n_lat=n_lat, n_ctx=n_ctx)

        log_gamma = jnp.log1p(-jnp.exp2(-ret_decay_raw[layer].astype(F32)))
        rl, rc = _retention(pr, log_gamma, cos, sin, bsz=bsz, n_lat=n_lat, n_ctx=n_ctx, col0=0)

        xs, fu, eidx, ew, rank, counts = _mix(
            ysl, ysc, pu, rl, rc, xs, mods[layer], s5_d[layer].reshape(1, s5w), glu_w_b,
            glu_b[layer].reshape(1, s5w), w_out_b, norm2_g[layer].reshape(1, d), rw_t, rb,
            layer=layer, n_lat=lat_total, rows_per_batch=n_lat)

        dest, bexp, nused, counts, pstart = _dispatch_plan(eidx, rank, counts)
        xsort = _dispatch(fu, dest, counts, pstart, nused, nblk=bexp.shape[0])
        ys = _experts(xsort, bexp, nused, wg_b, wu_b, wd_b, layer)
        moe = (ys, dest, ew.T)

    out = _prenorm(xs, moe, mods[depth - 1], mods[depth - 1], final_norm_g,
                   n_lat=lat_total, rows_per_batch=n_lat, final=True)
    return out.reshape(bsz, n_lat, d)
```

```python
import functools
import math

import jax
import jax.numpy as jnp
import numpy as np
from jax import lax
from jax.experimental import pallas as pl
from jax.experimental.pallas import tpu as pltpu

F32 = jnp.float32
BF16 = jnp.bfloat16
HIGHEST = lax.Precision.HIGHEST

GRID_W = 64
S5_GROUP = 16
S5_STATE = 64
S5_STEP = 16
RET_HEAD_DIM = 256
RET_BLOCK = 256
ROPE_BASE = 10000.0
N_EXPERTS = 16
N_EXPERT_GROUPS = 4
EXPERTS_PER_GROUP = 4
TOP_K = 2
NORM_EPS = 1e-6

ROW_BLOCK = 256
MIX_ROW_BLOCK = 512
MM_ROW_BLOCK = 1024
MM_COL_BLOCK = 1024
MOE_ROWS = 256
S5_TILE_GROUPS = 128 // S5_GROUP
MIB = 1024 * 1024


def _cparams(sem, vmem_mib):
    return pltpu.CompilerParams(dimension_semantics=sem, vmem_limit_bytes=int(vmem_mib * MIB))


def _mods_kernel(c_ref, w_ref, b_ref, o_ref):
    cv = c_ref[...]
    act = cv * jax.nn.sigmoid(cv)
    o_ref[0] = jnp.dot(act.astype(BF16), w_ref[0].astype(BF16), preferred_element_type=F32) + b_ref[0]


def _mods(c8, mod_w, mod_b):
    depth, d, n = mod_w.shape
    tn = 1024
    return pl.pallas_call(
        _mods_kernel,
        out_shape=jax.ShapeDtypeStruct((depth, 8, n), F32),
        grid=(depth, n // tn),
        in_specs=[pl.BlockSpec((8, d), lambda l, j: (0, 0)),
                  pl.BlockSpec((1, d, tn), lambda l, j: (l, 0, j)),
                  pl.BlockSpec((1, 1, tn), lambda l, j: (l, 0, j))],
        out_specs=pl.BlockSpec((1, 8, tn), lambda l, j: (l, 0, j)),
        compiler_params=_cparams(("arbitrary", "arbitrary"), 40),
        name="adaln_mods",
    )(c8, mod_w, mod_b.reshape(depth, 1, n))


def _prenorm_kernel(*refs, combine, final, d, tm, nblk, lat_blocks):
    if combine:
        (dest0_ref, destn_ref, x_ref, ys_hbm, wt_ref, mprev_ref, mcur_ref, g_ref), rest = refs[:8], refs[8:]
        outs, (ybuf, sem) = rest[:-2], rest[-2:]
    else:
        (xl_ref, xc_ref, mcur_ref, g_ref), outs = refs[:4], refs[4:]
    if not combine:
        x = jnp.where(pl.program_id(0) < lat_blocks, xl_ref[...], xc_ref[...])
    else:
        x = x_ref[...]
        i = pl.program_id(0)
        slot = i % 2
        per_slot = TOP_K * tm

        def gather(dref, s):
            for tt in range(tm):
                for k in range(TOP_K):
                    src = ys_hbm.at[pl.ds(dref[k, tt], 1)]
                    dst = ybuf.at[pl.ds(s * per_slot + k * tm + tt, 1)]
                    pltpu.make_async_copy(src, dst, sem.at[s]).start(priority=k)

        @pl.when(i == 0)
        def _():
            gather(dest0_ref, 0)

        for s_next in range(2):
            @pl.when((i + 1 < nblk) & (slot == 1 - s_next))
            def _():
                gather(destn_ref, s_next)

        base = pl.multiple_of(slot * per_slot, per_slot)
        pltpu.make_async_copy(ys_hbm.at[pl.ds(0, per_slot)], ybuf.at[pl.ds(base, per_slot)], sem.at[slot]).wait()

        def rows_of(k):
            return ybuf[pl.ds(pl.multiple_of(base + k * tm, tm), tm), :]

        gate2 = mprev_ref[:, 5 * d:6 * d]
        x = x + gate2 * (wt_ref[:, 0:1] * rows_of(0) + wt_ref[:, 1:2] * rows_of(1))
    ms = jnp.mean(x * x, axis=-1, keepdims=True)
    xn = x * lax.rsqrt(ms + NORM_EPS) * g_ref[...]
    if final:
        outs[0][...] = xn
    else:
        outs[0][...] = x
        shift = mcur_ref[:, 0:d]
        scale = mcur_ref[:, d:2 * d]
        outs[1][...] = (xn * (1.0 + scale) + shift).astype(BF16)


def _prenorm(x, moe, mods_prev, mods_cur, gain, *, n_lat, rows_per_batch, final):
    combine = moe is not None
    assert combine != isinstance(x, tuple)
    t = x.shape[0] if combine else x[0].shape[0] + x[1].shape[0]
    d = x.shape[1] if combine else x[0].shape[1]
    tm = ROW_BLOCK
    nblk = (n_lat if final else t) // tm
    lat_blocks = n_lat // tm
    per_b = rows_per_batch // tm
    ctx_row = n_lat // rows_per_batch

    def mod_idx(i):
        return (jnp.where(i < lat_blocks, i // per_b, ctx_row), 0, 0)

    row_spec = pl.BlockSpec((tm, d), lambda i: (i, 0))
    mod_spec = pl.BlockSpec((None, 1, 6 * d), mod_idx)
    g_spec = pl.BlockSpec((1, d), lambda i: (0, 0))
    ins, specs, scratch = [], [], []
    if combine:
        ys, dest, wt = moe
        ins += [dest, dest]
        specs += [pl.BlockSpec((TOP_K, tm), lambda i: (0, 0), memory_space=pltpu.SMEM),
                  pl.BlockSpec((TOP_K, tm), lambda i: (0, jnp.minimum(i + 1, nblk - 1)), memory_space=pltpu.SMEM)]
    if combine:
        ins.append(x)
        specs.append(row_spec)
    else:
        ins += list(x)
        specs += [pl.BlockSpec((tm, d), lambda i: (jnp.minimum(i, lat_blocks - 1), 0)),
                  pl.BlockSpec((tm, d), lambda i: (jnp.maximum(i - lat_blocks, 0), 0))]
    if combine:
        ins += [ys, wt, mods_prev]
        specs += [pl.BlockSpec(memory_space=pl.ANY), pl.BlockSpec((tm, TOP_K), lambda i: (i, 0)), mod_spec]
        scratch = [pltpu.VMEM((2 * TOP_K * tm, d), F32), pltpu.SemaphoreType.DMA((2,))]
    ins += [mods_cur, gain.reshape(1, d)]
    specs += [mod_spec, g_spec]
    if final:
        out_shape = jax.ShapeDtypeStruct((n_lat, d), F32)
        out_specs = row_spec
    else:
        out_shape = (jax.ShapeDtypeStruct((t, d), F32), jax.ShapeDtypeStruct((t, d), BF16))
        out_specs = (row_spec, row_spec)
    return pl.pallas_call(
        functools.partial(_prenorm_kernel, combine=combine, final=final, d=d, tm=tm, nblk=nblk,
                          lat_blocks=lat_blocks),
        out_shape=out_shape, grid=(nblk,), in_specs=specs, out_specs=out_specs, scratch_shapes=scratch,
        compiler_params=_cparams(("arbitrary",), 40),
        name="prenorm_final" if final else ("prenorm_combine" if combine else "prenorm"),
    )(*ins)


def _inproj_kernel(h_ref, w_ref, pu_ref, pr_ref):
    acc = jnp.dot(h_ref[...], w_ref[...], preferred_element_type=F32)

    @pl.when(pl.program_id(1) == 0)
    def _():
        pu_ref[...] = acc

    @pl.when(pl.program_id(1) > 0)
    def _():
        pr_ref[...] = acc.astype(pr_ref.dtype)


def _in_proj(h, w, layer, s5w):
    t, k = h.shape
    n = w.shape[2]
    tm, tn = MM_ROW_BLOCK, MM_COL_BLOCK
    assert s5w == tn
    return pl.pallas_call(
        _inproj_kernel,
        out_shape=(jax.ShapeDtypeStruct((t, s5w), F32), jax.ShapeDtypeStruct((t, n - s5w), BF16)),
        grid=(t // tm, n // tn),
        in_specs=[pl.BlockSpec((tm, k), lambda i, j: (i, 0)),
                  pl.BlockSpec((None, k, tn), lambda i, j: (layer, 0, j))],
        out_specs=(pl.BlockSpec((tm, tn), lambda i, j: (i, 0)),
                   pl.BlockSpec((tm, tn), lambda i, j: (i, jnp.maximum(j - 1, 0)))),
        compiler_params=_cparams(("arbitrary", "arbitrary"), 40),
        name="in_proj",
    )(h, w)


def _s5_weights(a_re, a_im, log_dt, b_re, b_im, c_re, c_im):
    ns = S5_STEP
    a_re, a_im = a_re.astype(F32), a_im.astype(F32)
    dt = jnp.exp(log_dt.astype(F32))[..., None]
    lam_re, lam_im = dt * a_re, dt * a_im
    tau = jnp.arange(ns + 1, dtype=F32)[:, None, None, None]
    mag = jnp.exp(tau * lam_re[None])
    p_re, p_im = mag * jnp.cos(tau * lam_im[None]), mag * jnp.sin(tau * lam_im[None])
    x, y = p_re[1] - 1.0, p_im[1]
    den = a_re * a_re + a_im * a_im
    k_re, k_im = (x * a_re + y * a_im) / den, (y * a_re - x * a_im) / den
    b_re, b_im = b_re.astype(F32), b_im.astype(F32)
    bb_re = k_re[..., None] * b_re - k_im[..., None] * b_im
    bb_im = k_re[..., None] * b_im + k_im[..., None] * b_re
    c_re, c_im = c_re.astype(F32), c_im.astype(F32)
    cp_re = c_re[None] * p_re[:, :, :, None, :] - c_im[None] * p_im[:, :, :, None, :]
    cp_im = c_re[None] * p_im[:, :, :, None, :] + c_im[None] * p_re[:, :, :, None, :]
    kmat = jnp.einsum('tdgon,dgni->tdgoi', jnp.concatenate([cp_re, -cp_im], axis=-1),
                      jnp.concatenate([bb_re, bb_im], axis=-2))
    lag = np.arange(ns)[None, :] - np.arange(ns)[:, None]
    sel_f = jnp.asarray(lag[None] == np.arange(ns)[:, None, None], F32)
    sel_b = jnp.asarray(-lag[None] == np.arange(ns)[:, None, None], F32)
    g = a_re.shape[1]
    m = (jnp.einsum('lst,lgoi->gsito', sel_f, kmat[:ns, 0]) + jnp.einsum('lst,lgoi->gsito', sel_b, kmat[:ns, 1]))
    m = m.reshape(g, ns * S5_GROUP, ns * S5_GROUP)

    def inject(p_r, p_i, d):
        r = p_r[:, :, :, None] * bb_re[d][None] - p_i[:, :, :, None] * bb_im[d][None]
        i = p_r[:, :, :, None] * bb_im[d][None] + p_i[:, :, :, None] * bb_re[d][None]
        tr = lambda z: jnp.transpose(z, (1, 0, 3, 2)).reshape(g, ns * S5_GROUP, S5_STATE)
        return tr(r), tr(i)

    pf_re, pf_im = inject(p_re[ns - 1::-1, 0], p_im[ns - 1::-1, 0], 0)
    pb_re, pb_im = inject(p_re[:ns, 1], p_im[:ns, 1], 1)
    w1 = jnp.concatenate([z.astype(BF16) for z in (m, pf_re, pb_re, pf_im, pb_im)], axis=-1)

    def carry(cp_t):
        return jnp.transpose(cp_t, (1, 3, 0, 2)).reshape(g, S5_STATE, ns * S5_GROUP)

    up = slice(1, ns + 1)
    pd_re, pd_im = p_re[ns:0:-1, 1], p_im[ns:0:-1, 1]
    cpd_re = c_re[1][None] * pd_re[:, :, None, :] - c_im[1][None] * pd_im[:, :, None, :]
    cpd_im = c_re[1][None] * pd_im[:, :, None, :] + c_im[1][None] * pd_re[:, :, None, :]
    q = jnp.concatenate([carry(cp_re[up, 0]).astype(BF16), carry(cpd_re).astype(BF16),
                         (-carry(cp_im[up, 0])).astype(BF16), (-carry(cpd_im)).astype(BF16)],
                        axis=1)
    a16 = jnp.stack([jnp.concatenate([p_re[ns, 0], p_re[ns, 1]], axis=-1),
                     jnp.concatenate([p_im[ns, 0], p_im[ns, 1]], axis=-1)], axis=1)
    return w1.astype(BF16), q.astype(BF16), a16


def _group_transpose(xs):
    ng = len(xs)
    grp = lax.broadcasted_iota(jnp.int32, xs[0].shape, 1) // S5_GROUP
    rolled = []
    for j in range(ng):
        y = xs[j % ng]
        for k in range(1, ng):
            y = jnp.where(grp == k, xs[(k + j) % ng], y)
        rolled.append(y if j == 0 else pltpu.roll(y, S5_GROUP * j, 1))
    outs = []
    for a in range(ng):
        o = rolled[(-a) % ng]
        for b in range(1, ng):
            o = jnp.where(grp == b, rolled[(b - a) % ng], o)
        outs.append(o)
    return outs


def _s5_kernel(ul_ref, uc_ref, w1_ref, q_ref, a_ref, yl_ref, yc_ref, uflat, st_re, st_im, yfl, *,
               lat_chunks, ctx_chunks):
    ng = S5_TILE_GROUPS
    width = S5_STEP * S5_GROUP
    n_chunks = lat_chunks + ctx_chunks
    sub = 32

    def flatten(src_ref, c0, row0, n):
        for hh in range(2):
            xs = [src_ref[pl.ds(row0 + hh * 8 + t, n, stride=S5_STEP), :] for t in range(8)]
            outs = _group_transpose(xs)
            for k in range(ng):
                uflat[k, pl.ds(c0, n), hh * 128:(hh + 1) * 128] = outs[k].astype(BF16)

    def flat_body(i, _):
        c0 = pl.multiple_of(i * sub, sub)
        flatten(ul_ref, c0, c0 * S5_STEP, sub)
        return 0

    lax.fori_loop(0, lat_chunks // sub, flat_body, 0)
    flatten(uc_ref, lat_chunks, 0, ctx_chunks)

    for k in range(ng):
        z = jnp.dot(uflat[k], w1_ref[k], preferred_element_type=F32)
        yfl[k] = z[:, :width]
        st_re[pl.ds(k, n_chunks, stride=ng), :] = z[:, width:width + 128]
        st_im[pl.ds(k, n_chunks, stride=ng), :] = z[:, width + 128:]

    lane = lax.broadcasted_iota(jnp.int32, (ng, 128), 1)
    fwd_lane = lane < S5_STATE
    a_re = a_ref[0]
    a_im = a_ref[1]

    def make_body(base, n):
        def body(i, carry):
            h_re, h_im = carry
            rf = pl.multiple_of((base + i) * ng, ng)
            rb = pl.multiple_of((base + n - 1 - i) * ng, ng)
            tf_re, tf_im = st_re[pl.ds(rf, ng), :], st_im[pl.ds(rf, ng), :]
            tb_re, tb_im = st_re[pl.ds(rb, ng), :], st_im[pl.ds(rb, ng), :]
            t_re = jnp.where(fwd_lane, tf_re, tb_re)
            t_im = jnp.where(fwd_lane, tf_im, tb_im)
            n_re = a_re * h_re - a_im * h_im + t_re
            n_im = a_re * h_im + a_im * h_re + t_im
            st_re[pl.ds(rf, ng), :] = jnp.where(fwd_lane, h_re, tf_re)
            st_im[pl.ds(rf, ng), :] = jnp.where(fwd_lane, h_im, tf_im)
            st_re[pl.ds(rb, ng), :] = jnp.where(fwd_lane, tb_re, h_re)
            st_im[pl.ds(rb, ng), :] = jnp.where(fwd_lane, tb_im, h_im)
            return n_re, n_im
        return body

    carry = (jnp.zeros((ng, 128), F32), jnp.zeros((ng, 128), F32))
    carry = lax.fori_loop(0, ctx_chunks, make_body(lat_chunks, ctx_chunks), carry)
    carry = lax.fori_loop(0, lat_chunks, make_body(0, lat_chunks), carry)

    for k in range(ng):
        hk = jnp.concatenate([st_re[pl.ds(k, n_chunks, stride=ng), :], st_im[pl.ds(k, n_chunks, stride=ng), :]],
                             axis=1).astype(BF16)
        yfl[k] = yfl[k] + jnp.dot(hk, q_ref[k], preferred_element_type=F32)

    def unflatten(dst_ref, c0, row0, n):
        for hh in range(2):
            vs = [yfl[k, pl.ds(c0, n), hh * 128:(hh + 1) * 128] for k in range(ng)]
            ts = _group_transpose(vs)
            for t in range(8):
                dst_ref[pl.ds(row0 + hh * 8 + t, n, stride=S5_STEP), :] = ts[t]

    def unflat_body(i, _):
        c0 = pl.multiple_of(i * sub, sub)
        unflatten(yl_ref, c0, c0 * S5_STEP, sub)
        return 0

    lax.fori_loop(0, lat_chunks // sub, unflat_body, 0)
    unflatten(yc_ref, lat_chunks, 0, ctx_chunks)


def _s5_scan(pu, w1, q, a16, layer, *, bsz, n_lat, n_ctx):
    g = w1.shape[1]
    ng = S5_TILE_GROUPS
    width = S5_STEP * S5_GROUP
    lat_chunks, ctx_chunks = n_lat // S5_STEP, n_ctx // S5_STEP
    assert lat_chunks % 32 == 0 and ctx_chunks % 2 == 0 and ctx_chunks % 16 == 0
    n_chunks = lat_chunks + ctx_chunks
    ctx_blk0 = bsz * n_lat // n_ctx
    a_t = jnp.transpose(a16.reshape(-1, g // ng, ng, 2, 128), (0, 1, 3, 2, 4))
    kern = functools.partial(_s5_kernel, lat_chunks=lat_chunks, ctx_chunks=ctx_chunks)
    return pl.pallas_call(
        kern,
        out_shape=(jax.ShapeDtypeStruct((bsz * n_lat, g * S5_GROUP), F32),
                   jax.ShapeDtypeStruct((bsz * n_ctx, g * S5_GROUP), F32)),
        grid=(g // ng, bsz),
        in_specs=[pl.BlockSpec((n_lat, 128), lambda i, b: (b, i)),
                  pl.BlockSpec((n_ctx, 128), lambda i, b: (ctx_blk0 + b, i)),
                  pl.BlockSpec((None, ng, width, 2 * width), lambda i, b: (layer, i, 0, 0)),
                  pl.BlockSpec((None, ng, width, width), lambda i, b: (layer, i, 0, 0)),
                  pl.BlockSpec((None, None, 2, ng, 128), lambda i, b: (layer, i, 0, 0, 0))],
        out_specs=(pl.BlockSpec((n_lat, 128), lambda i, b: (b, i)),
                   pl.BlockSpec((n_ctx, 128), lambda i, b: (b, i))),
        scratch_shapes=[pltpu.VMEM((ng, n_chunks, width), BF16),
                        pltpu.VMEM((n_chunks * ng, 128), F32),
                        pltpu.VMEM((n_chunks * ng, 128), F32),
                        pltpu.VMEM((ng, n_chunks, width), F32)],
        compiler_params=_cparams(("arbitrary", "arbitrary"), 40),
        name="s5_scan",
    )(pu, pu, w1, q, a_t)


def _ret_kernel(lg_ref, ql_ref, kl_ref, vl_ref, gl_ref, qc_ref, kc_ref, vc_ref, gc_ref, cos_ref, sin_ref,
                ol_ref, oc_ref, qs, ks, vs, acc, sf, sb, dm, *, n_ctx, n_lat):
    c = RET_BLOCK
    half = RET_HEAD_DIM // 2
    head = pl.program_id(1)
    lgf = lg_ref[0, head]
    lgb = lg_ref[1, head]
    qscale = RET_HEAD_DIM ** -0.5
    ctx_rows = n_ctx * c

    ii = lax.broadcasted_iota(jnp.int32, (c, c), 0)
    jj = lax.broadcasted_iota(jnp.int32, (c, c), 1)
    rel = (ii - jj).astype(F32)
    dsum = (jnp.where(rel >= 0, jnp.exp(lgf * jnp.maximum(rel, 0.0)), 0.0)
            + jnp.where(rel <= 0, jnp.exp(lgb * jnp.maximum(-rel, 0.0)), 0.0))
    pos = lax.broadcasted_iota(jnp.int32, (c, 1), 0).astype(F32)
    qdec_f = jnp.exp(lgf * (pos + 1.0))
    kdec_f = jnp.exp(lgf * (c - 1.0 - pos))
    qdec_b = jnp.exp(lgb * (c - pos))
    kdec_b = jnp.exp(lgb * pos)
    cg_f = jnp.exp(lgf * c)
    cg_b = jnp.exp(lgb * c)

    qs[0:ctx_rows, :] = (qc_ref[...].astype(F32) * qscale).astype(BF16)
    ks[0:ctx_rows, :] = kc_ref[...]
    vs[0:ctx_rows, :] = vc_ref[...]

    def rope_body(n, _):
        r = pl.multiple_of(n * c, c)
        cs = cos_ref[pl.ds(r, c), :]
        sn = sin_ref[pl.ds(r, c), :]

        def rot(t):
            t1, t2 = t[:, :half], t[:, half:]
            return jnp.concatenate([t1 * cs - t2 * sn, t2 * cs + t1 * sn], axis=1)

        dst = pl.multiple_of(ctx_rows + r, c)
        qs[pl.ds(dst, c), :] = (rot(ql_ref[pl.ds(r, c), :].astype(F32)) * qscale).astype(BF16)
        ks[pl.ds(dst, c), :] = rot(kl_ref[pl.ds(r, c), :].astype(F32)).astype(BF16)
        vs[pl.ds(dst, c), :] = vl_ref[pl.ds(r, c), :]
        return 0

    lax.fori_loop(0, n_lat, rope_body, 0)

    dm[...] = dsum
    sf[...] = jnp.zeros_like(sf)
    sb[...] = jnp.zeros_like(sb)

    def load(r):
        return qs[pl.ds(r, c), :], ks[pl.ds(r, c), :], vs[pl.ds(r, c), :]

    def intra_of(q, k, v):
        s = lax.dot_general(q, k, (((1,), (1,)), ((), ())), preferred_element_type=F32)
        return jnp.dot((s * dm[...]).astype(BF16), v, preferred_element_type=F32)

    def step(s_ref, q, k, v, qdec, kdec, cg):
        inter = jnp.dot(q, s_ref[...].astype(BF16), preferred_element_type=F32) * qdec
        kd = (k.astype(F32) * kdec).astype(BF16)
        s_ref[...] = cg * s_ref[...] + lax.dot_general(kd, v, (((0,), (0,)), ((), ())),
                                                       preferred_element_type=F32)
        return inter

    def finish(o, gate, out_ref, ro):
        o = o * lax.rsqrt(jnp.mean(o * o, axis=-1, keepdims=True) + NORM_EPS)
        out_ref[pl.ds(ro, c), :] = (o * (gate * jax.nn.sigmoid(gate))).astype(out_ref.dtype)

    fwd = (sf, qdec_f, kdec_f, cg_f)
    bwd = (sb, qdec_b, kdec_b, cg_b)

    for n in range(n_ctx):
        q, k, v = load(n * c)
        acc[n * c:(n + 1) * c, :] = intra_of(q, k, v) + step(fwd[0], q, k, v, *fwd[1:])
    for n in range(n_ctx - 1, -1, -1):
        q, k, v = load(n * c)
        o = acc[n * c:(n + 1) * c, :] + step(bwd[0], q, k, v, *bwd[1:])
        finish(o, gc_ref[n * c:(n + 1) * c, :].astype(F32), oc_ref, n * c)

    def first_visits(i, _):
        rf = pl.multiple_of(i * c, c)
        rb = pl.multiple_of((n_lat - 1 - i) * c, c)
        q, k, v = load(ctx_rows + rf)
        acc[pl.ds(ctx_rows + rf, c), :] = intra_of(q, k, v) + step(fwd[0], q, k, v, *fwd[1:])
        q, k, v = load(ctx_rows + rb)
        acc[pl.ds(ctx_rows + rb, c), :] = step(bwd[0], q, k, v, *bwd[1:])
        return 0

    def second_visits(i, _):
        rf = pl.multiple_of(i * c, c)
        rb = pl.multiple_of((n_lat - 1 - i) * c, c)
        q, k, v = load(ctx_rows + rf)
        o = acc[pl.ds(ctx_rows + rf, c), :] + intra_of(q, k, v) + step(fwd[0], q, k, v, *fwd[1:])
        finish(o, gl_ref[pl.ds(rf, c), :].astype(F32), ol_ref, rf)
        q, k, v = load(ctx_rows + rb)
        o = acc[pl.ds(ctx_rows + rb, c), :] + step(bwd[0], q, k, v, *bwd[1:])
        finish(o, gl_ref[pl.ds(rb, c), :].astype(F32), ol_ref, rb)
        return 0

    lax.fori_loop(0, n_lat // 2, first_visits, 0)
    lax.fori_loop(n_lat // 2, n_lat, second_visits, 0)


def _retention(p, log_gamma, cos, sin, *, bsz, n_lat, n_ctx, col0):
    t = p.shape[0]
    dh = RET_HEAD_DIM
    heads = (p.shape[1] - col0) // (4 * dh)
    qb, kb, vb, gb = (col0 // dh + i * heads for i in range(4))
    ctx_blk0 = bsz * n_lat // n_ctx

    def lat(cb):
        return pl.BlockSpec((n_lat, dh), lambda b, h: (b, cb + h))

    def ctx(cb):
        return pl.BlockSpec((n_ctx, dh), lambda b, h: (ctx_blk0 + b, cb + h))

    tab = pl.BlockSpec((n_lat, dh // 2), lambda b, h: (0, 0))
    assert n_ctx % RET_BLOCK == 0 and (n_lat // RET_BLOCK) % 2 == 0
    kern = functools.partial(_ret_kernel, n_ctx=n_ctx // RET_BLOCK, n_lat=n_lat // RET_BLOCK)
    seq = n_ctx + n_lat
    return pl.pallas_call(
        kern,
        out_shape=(jax.ShapeDtypeStruct((bsz * n_lat, heads * dh), BF16),
                   jax.ShapeDtypeStruct((bsz * n_ctx, heads * dh), BF16)),
        grid=(bsz, heads),
        in_specs=[pl.BlockSpec(memory_space=pltpu.SMEM),
                  lat(qb), lat(kb), lat(vb), lat(gb), ctx(qb), ctx(kb), ctx(vb), ctx(gb), tab, tab],
        out_specs=(pl.BlockSpec((n_lat, dh), lambda b, h: (b, h)),
                   pl.BlockSpec((n_ctx, dh), lambda b, h: (b, h))),
        scratch_shapes=[pltpu.VMEM((seq, dh), BF16), pltpu.VMEM((seq, dh), BF16), pltpu.VMEM((seq, dh), BF16),
                        pltpu.VMEM((seq, dh), F32), pltpu.VMEM((dh, dh), F32), pltpu.VMEM((dh, dh), F32),
                        pltpu.VMEM((RET_BLOCK, RET_BLOCK), F32)],
        compiler_params=_cparams(("arbitrary", "arbitrary"), 52),
        name="retention",
    )(log_gamma, p, p, p, p, p, p, p, p, cos, sin)


def _top2_sum(a, b, c, d):
    hi1, lo1 = jnp.maximum(a, b), jnp.minimum(a, b)
    hi2, lo2 = jnp.maximum(c, d), jnp.minimum(c, d)
    return jnp.maximum(hi1, hi2) + jnp.maximum(jnp.minimum(hi1, hi2), jnp.maximum(lo1, lo2))


def _route(logits_t, rb):
    sc = jax.nn.sigmoid(logits_t)
    biased = sc + rb
    per = EXPERTS_PER_GROUP
    brow = [biased[e:e + 1, :] for e in range(N_EXPERTS)]
    srow = [sc[e:e + 1, :] for e in range(N_EXPERTS)]
    gscore = [_top2_sum(*brow[per * g:per * g + per]) for g in range(N_EXPERT_GROUPS)]
    best_v = gscore[0]
    best_g = jnp.zeros_like(best_v, dtype=jnp.int32)
    for g in range(1, N_EXPERT_GROUPS):
        upd = gscore[g] > best_v
        best_v = jnp.where(upd, gscore[g], best_v)
        best_g = jnp.where(upd, g, best_g)
    vals, sels = [], []
    for j in range(per):
        v = brow[j]
        s = srow[j]
        for g in range(1, N_EXPERT_GROUPS):
            v = jnp.where(best_g == g, brow[per * g + j], v)
            s = jnp.where(best_g == g, srow[per * g + j], s)
        vals.append(v)
        sels.append(s)
    v1, i1, s1 = vals[0], jnp.zeros_like(best_g), sels[0]
    for j in range(1, per):
        upd = vals[j] > v1
        v1 = jnp.where(upd, vals[j], v1)
        s1 = jnp.where(upd, sels[j], s1)
        i1 = jnp.where(upd, j, i1)
    v2 = jnp.full_like(v1, -jnp.inf)
    i2, s2 = jnp.zeros_like(best_g), jnp.zeros_like(s1)
    for j in range(per):
        upd = (i1 != j) & (vals[j] > v2)
        v2 = jnp.where(upd, vals[j], v2)
        s2 = jnp.where(upd, sels[j], s2)
        i2 = jnp.where(upd, j, i2)
    tot = s1 + s2
    idx = jnp.concatenate([best_g * per + i1, best_g * per + i2], axis=0)
    wts = jnp.concatenate([s1 / tot, s2 / tot], axis=0)
    return idx, wts


def _mix_kernel(ysl_ref, ysc_ref, u_ref, rl_ref, rc_ref, x_ref, m_ref, dsk_ref, gw_ref, gb_ref, ws_ref, wr_ref,
                g2_ref, rw_ref, rb_ref, xo_ref, fu_ref, ei_ref, ew_ref, rank_ref, cnt_ref, run_ref, *, lat_blocks, d):
    is_lat = pl.program_id(0) < lat_blocks
    ys = jnp.where(is_lat, ysl_ref[...], ysc_ref[...])
    y = ys + dsk_ref[...] * u_ref[...]
    y = jax.nn.gelu(y, approximate=True)
    z = jnp.dot(y.astype(BF16), gw_ref[...], preferred_element_type=F32) + gb_ref[...]
    s5o = (y * jax.nn.sigmoid(z)).astype(BF16)
    r = jnp.where(is_lat, rl_ref[...], rc_ref[...])
    mixed = (jnp.dot(s5o, ws_ref[...], preferred_element_type=F32)
             + jnp.dot(r, wr_ref[...], preferred_element_type=F32))
    gate1 = m_ref[:, 2 * d:3 * d]
    x = x_ref[...] + gate1 * mixed
    xo_ref[...] = x
    ms = jnp.mean(x * x, axis=-1, keepdims=True)
    f = x * lax.rsqrt(ms + NORM_EPS) * g2_ref[...]
    f = f * (1.0 + m_ref[:, 4 * d:5 * d]) + m_ref[:, 3 * d:4 * d]
    tm = f.shape[0]
    fu_ref[...] = f
    logits_t = lax.dot_general(rw_ref[...], f, (((1,), (1,)), ((), ())), precision=HIGHEST,
                               preferred_element_type=F32)
    idx, wts = _route(logits_t, rb_ref[...])
    ei_ref[...] = idx
    ew_ref[...] = wts

    @pl.when(pl.program_id(0) == 0)
    def _():
        run_ref[...] = jnp.zeros_like(run_ref)

    e_iota = lax.broadcasted_iota(jnp.int32, (N_EXPERTS, tm), 0)
    tri = jnp.where(lax.broadcasted_iota(jnp.int32, (tm, tm), 0) <= lax.broadcasted_iota(jnp.int32, (tm, tm), 1),
                    1.0, 0.0).astype(BF16)
    hit0 = e_iota == idx[0:1, :]
    hit1 = e_iota == idx[1:2, :]
    p0 = jnp.dot(jnp.where(hit0, 1.0, 0.0).astype(BF16), tri, preferred_element_type=F32)
    p1 = jnp.dot(jnp.where(hit1, 1.0, 0.0).astype(BF16), tri, preferred_element_type=F32)
    tot0 = p0[:, tm - 1:tm]
    tot1 = p1[:, tm - 1:tm]
    before = run_ref[...]
    r0 = jnp.sum(jnp.where(hit0, before + p0, 0.0), axis=0, keepdims=True) - 1.0
    r1 = jnp.sum(jnp.where(hit1, before + tot0 + p1, 0.0), axis=0, keepdims=True) - 1.0
    rank_ref[...] = jnp.concatenate([r0, r1], axis=0).astype(jnp.int32)
    after = before + tot0 + tot1
    run_ref[...] = after
    cnt_ref[...] = after.astype(jnp.int32)


def _mix(ysl, ysc, pu, rl, rc, x, mods_cur, d_skip, glu_w, glu_b, w_out, g2, rw_t, rb, *, layer, n_lat,
         rows_per_batch):
    t, d = x.shape
    sw = pu.shape[1]
    tm = MIX_ROW_BLOCK
    lat_blocks = n_lat // tm
    per_b = rows_per_batch // tm
    ctx_row = n_lat // rows_per_batch
    const = lambda i: (0, 0)
    row = lambda i: (i, 0)
    lat_row = lambda i: (jnp.minimum(i, lat_blocks - 1), 0)
    ctx_rowblk = lambda i: (jnp.maximum(i - lat_blocks, 0), 0)
    once = pl.Buffered(1)
    kern = functools.partial(_mix_kernel, lat_blocks=lat_blocks, d=d)
    return pl.pallas_call(
        kern,
        out_shape=(jax.ShapeDtypeStruct((t, d), F32), jax.ShapeDtypeStruct((t, d), F32),
                   jax.ShapeDtypeStruct((TOP_K, t), jnp.int32), jax.ShapeDtypeStruct((TOP_K, t), F32),
                   jax.ShapeDtypeStruct((TOP_K, t), jnp.int32), jax.ShapeDtypeStruct((N_EXPERTS, 1), jnp.int32)),
        grid=(t // tm,),
        scratch_shapes=[pltpu.VMEM((N_EXPERTS, 1), F32)],
        in_specs=[pl.BlockSpec((tm, sw), lat_row),
                  pl.BlockSpec((tm, sw), ctx_rowblk, pipeline_mode=once),
                  pl.BlockSpec((tm, sw), row),
                  pl.BlockSpec((tm, sw), lat_row),
                  pl.BlockSpec((tm, sw), ctx_rowblk, pipeline_mode=once),
                  pl.BlockSpec((tm, d), row),
                  pl.BlockSpec((None, 1, 6 * d), lambda i: (jnp.where(i < lat_blocks, i // per_b, ctx_row), 0, 0)),
                  pl.BlockSpec((1, sw), const),
                  pl.BlockSpec((None, sw, sw), lambda i: (layer, 0, 0), pipeline_mode=once),
                  pl.BlockSpec((1, sw), const),
                  pl.BlockSpec((None, sw, d), lambda i: (layer, 0, 0), pipeline_mode=once),
                  pl.BlockSpec((None, d - sw, d), lambda i: (layer, 1, 0), pipeline_mode=once),
                  pl.BlockSpec((1, d), const),
                  pl.BlockSpec((N_EXPERTS, d), const),
                  pl.BlockSpec((N_EXPERTS, 1), const)],
        out_specs=(pl.BlockSpec((tm, d), row), pl.BlockSpec((tm, d), row),
                   pl.BlockSpec((TOP_K, tm), lambda i: (0, i)), pl.BlockSpec((TOP_K, tm), lambda i: (0, i)),
                   pl.BlockSpec((TOP_K, tm), lambda i: (0, i)), pl.BlockSpec((N_EXPERTS, 1), const)),
        compiler_params=_cparams(("arbitrary",), 58),
        name="mix",
    )(ysl, ysc, pu, rl, rc, x, mods_cur, d_skip, glu_w, glu_b, w_out, w_out, g2, rw_t, rb)


def _dispatch_plan(eidx, rank, counts):
    bm = MOE_ROWS
    n_assign = eidx.size
    nblk = -(-n_assign // bm) + N_EXPERTS
    counts = counts.reshape(N_EXPERTS)
    padded = (counts + bm - 1) // bm * bm
    pend = jnp.cumsum(padded)
    pstart = pend - padded
    experts = jnp.arange(N_EXPERTS, dtype=jnp.int32)[:, None, None]
    dest = rank + jnp.sum(jnp.where(eidx[None] == experts, pstart[:, None, None], 0), axis=0)
    blk_start = jnp.arange(nblk, dtype=jnp.int32) * bm
    bexp = jnp.minimum(jnp.sum((pend[None, :] <= blk_start[:, None]).astype(jnp.int32), axis=1), N_EXPERTS - 1)
    nused = (pend[-1] // bm).astype(jnp.int32).reshape(1)
    return dest.astype(jnp.int32), bexp.astype(jnp.int32), nused, counts.astype(jnp.int32), pstart.astype(jnp.int32)


def _dispatch_kernel(cnt_ref, pst_ref, nused_ref, dest_ref, fu_ref, xs_hbm, zblk, sem, zsem, *, tb, bm, nblk):
    i = pl.program_id(0)

    for tt in range(tb):
        for k in range(TOP_K):
            pltpu.make_async_copy(fu_ref.at[pl.ds(tt, 1)], xs_hbm.at[pl.ds(dest_ref[k, tt], 1)],
                                  sem).start(priority=k)

    @pl.when(i == 0)
    def _():
        zblk[...] = jnp.zeros_like(zblk)
        zrow = zblk.at[pl.ds(0, 1)]

        def pad_rows(e, _):
            first = pst_ref[e] + cnt_ref[e]
            npad = (bm - cnt_ref[e] % bm) % bm

            def start(r, _):
                pltpu.make_async_copy(zrow, xs_hbm.at[pl.ds(first + r, 1)], zsem).start()
                return 0

            def wait(r, _):
                pltpu.make_async_copy(zrow, xs_hbm.at[pl.ds(0, 1)], zsem).wait()
                return 0

            lax.fori_loop(0, npad, start, 0)
            lax.fori_loop(0, npad, wait, 0)
            return 0

        lax.fori_loop(0, N_EXPERTS, pad_rows, 0)

        def pad_block(j, _):
            cp = pltpu.make_async_copy(zblk, xs_hbm.at[pl.ds(pl.multiple_of(j * bm, bm), bm)], zsem)
            cp.start()
            cp.wait()
            return 0

        lax.fori_loop(nused_ref[0], nblk, pad_block, 0)

    for k in range(TOP_K):
        pltpu.make_async_copy(fu_ref, xs_hbm.at[pl.ds(0, tb)], sem).wait()


def _dispatch(fu, dest, counts, pstart, nused, *, nblk):
    n_tok, d = fu.shape
    bm = MOE_ROWS
    tb = ROW_BLOCK
    assert n_tok % tb == 0
    kern = functools.partial(_dispatch_kernel, tb=tb, bm=bm, nblk=nblk)
    grid_spec = pltpu.PrefetchScalarGridSpec(
        num_scalar_prefetch=3,
        grid=(n_tok // tb,),
        in_specs=[pl.BlockSpec((TOP_K, tb), lambda i, c, p, n: (0, i), memory_space=pltpu.SMEM),
                  pl.BlockSpec((tb, d), lambda i, c, p, n: (i, 0))],
        out_specs=pl.BlockSpec(memory_space=pl.ANY),
        scratch_shapes=[pltpu.VMEM((bm, d), F32),
                        pltpu.SemaphoreType.DMA(()), pltpu.SemaphoreType.DMA(())],
    )
    return pl.pallas_call(
        kern,
        out_shape=jax.ShapeDtypeStruct((nblk * bm, d), F32),
        grid_spec=grid_spec,
        compiler_params=_cparams(("arbitrary",), 24),
        name="dispatch",
    )(counts, pstart, nused, dest, fu)


def _expert_kernel(bexp_ref, nused_ref, xs_ref, wg_ref, wu_ref, wd_ref, ys_ref):
    @pl.when(pl.program_id(0) < nused_ref[0])
    def _():
        xb = xs_ref[...].astype(BF16)
        gt = jnp.dot(xb, wg_ref[...], preferred_element_type=F32)
        up = jnp.dot(xb, wu_ref[...], preferred_element_type=F32)
        hmid = (gt * jax.nn.sigmoid(gt) * up).astype(BF16)
        ys_ref[...] = jnp.dot(hmid, wd_ref[...], preferred_element_type=F32)

    @pl.when(pl.program_id(0) >= nused_ref[0])
    def _():
        ys_ref[...] = jnp.zeros_like(ys_ref)


def _experts(xs, bexp, nused, wg, wu, wd, layer):
    bm = MOE_ROWS
    nblk = bexp.shape[0]
    d, ff = wg.shape[2], wg.shape[3]
    grid_spec = pltpu.PrefetchScalarGridSpec(
        num_scalar_prefetch=2,
        grid=(nblk,),
        in_specs=[pl.BlockSpec((bm, d), lambda j, be, nu: (j, 0)),
                  pl.BlockSpec((None, None, d, ff), lambda j, be, nu: (layer, be[j], 0, 0)),
                  pl.BlockSpec((None, None, d, ff), lambda j, be, nu: (layer, be[j], 0, 0)),
                  pl.BlockSpec((None, None, ff, d), lambda j, be, nu: (layer, be[j], 0, 0))],
        out_specs=pl.BlockSpec((bm, d), lambda j, be, nu: (j, 0)),
    )
    return pl.pallas_call(
        _expert_kernel,
        out_shape=jax.ShapeDtypeStruct((nblk * bm, d), F32),
        grid_spec=grid_spec,
        compiler_params=_cparams(("arbitrary",), 52),
        name="experts",
    )(bexp, nused, xs, wg, wu, wd)


def _rope_tables(n_lat):
    n_rows = n_lat // GRID_W
    pos = np.arange(n_lat)
    quarter = RET_HEAD_DIM // 4
    freqs = ROPE_BASE ** (-np.arange(quarter, dtype=np.float64) / quarter)
    ang = np.concatenate([(pos // GRID_W)[:, None] * freqs, (pos % GRID_W)[:, None] * freqs], axis=-1)
    assert ang.shape == (n_rows * GRID_W, RET_HEAD_DIM // 2)
    return jnp.asarray(np.cos(ang), F32), jnp.asarray(np.sin(ang), F32)


def kernel(x, c, ctx, c_ctx, mod_w, mod_b, norm1_g, norm2_g, w_in, s5_a_re, s5_a_im, s5_log_dt, s5_b_re, s5_b_im,
           s5_c_re, s5_c_im, s5_d, glu_w, glu_b, ret_decay_raw, w_out, router_w, router_b, exp_w_gate, exp_w_up,
           exp_w_down, final_norm_g):
    bsz, n_lat, d = x.shape
    n_ctx = ctx.shape[1]
    depth = mod_w.shape[0]
    s5w = s5_d.shape[1]
    t = bsz * (n_lat + n_ctx)
    lat_total = bsz * n_lat

    xs = (x.reshape(lat_total, d), ctx.reshape(bsz * n_ctx, d))
    c8 = jnp.zeros((8, d), F32).at[:bsz].set(c).at[bsz].set(c_ctx)
    mods = _mods(c8, mod_w, mod_b).reshape(depth, 8, 1, 6 * d)
    cos, sin = _rope_tables(n_lat)
    rw_t = router_w.T
    rb = router_b.reshape(N_EXPERTS, 1)

    w_in_b, glu_w_b, w_out_b = w_in.astype(BF16), glu_w.astype(BF16), w_out.astype(BF16)
    wg_b, wu_b, wd_b = exp_w_gate.astype(BF16), exp_w_up.astype(BF16), exp_w_down.astype(BF16)
    assert w_out.shape[1] == 2 * s5w
    w1, q, a16 = jax.vmap(_s5_weights)(s5_a_re, s5_a_im, s5_log_dt, s5_b_re, s5_b_im, s5_c_re, s5_c_im)

    moe = None
    for layer in range(depth):
        xs, h = _prenorm(xs, moe, mods[layer - 1] if layer else None, mods[layer], norm1_g[layer],
                         n_lat=lat_total, rows_per_batch=n_lat, final=False)
        pu, pr = _in_proj(h, w_in_b, layer, s5w)

        ysl, ysc = _s5_scan(pu, w1, q, a16, layer, bsz=bsz, n_lat=n_lat, n_ctx=n_ctx)

        log_gamma = jnp.log1p(-jnp.exp2(-ret_decay_raw[layer].astype(F32)))
        rl, rc = _retention(pr, log_gamma, cos, sin, bsz=bsz, n_lat=n_lat, n_ctx=n_ctx, col0=0)

        xs, fu, eidx, ew, rank, counts = _mix(
            ysl, ysc, pu, rl, rc, xs, mods[layer], s5_d[layer].reshape(1, s5w), glu_w_b,
            glu_b[layer].reshape(1, s5w), w_out_b, norm2_g[layer].reshape(1, d), rw_t, rb,
            layer=layer, n_lat=lat_total, rows_per_batch=n_lat)

        dest, bexp, nused, counts, pstart = _dispatch_plan(eidx, rank, counts)
        xsort = _dispatch(fu, dest, counts, pstart, nused, nblk=bexp.shape[0])
        ys = _experts(xsort, bexp, nused, wg_b, wu_b, wd_b, layer)
        moe = (ys, dest, ew.T)

    out = _prenorm(xs, moe, mods[depth - 1], mods[depth - 1], final_norm_g,
                   n_lat=lat_total, rows_per_batch=n_lat, final=True)
    return out.reshape(bsz, n_lat, d)
```

```python
import functools
import math

import jax
import jax.numpy as jnp
import numpy as np
from jax import lax
from jax.experimental import pallas as pl
from jax.experimental.pallas import tpu as pltpu

F32 = jnp.float32
BF16 = jnp.bfloat16
HIGHEST = lax.Precision.HIGHEST

GRID_W = 64
S5_GROUP = 16
S5_STATE = 64
S5_STEP = 16
RET_HEAD_DIM = 256
RET_BLOCK = 256
ROPE_BASE = 10000.0
N_EXPERTS = 16
N_EXPERT_GROUPS = 4
EXPERTS_PER_GROUP = 4
TOP_K = 2
NORM_EPS = 1e-6

ROW_BLOCK = 256
MIX_ROW_BLOCK = 512
MM_ROW_BLOCK = 1024
MM_COL_BLOCK = 1024
MOE_ROWS = 256
S5_TILE_GROUPS = 128 // S5_GROUP
MIB = 1024 * 1024


def _cparams(sem, vmem_mib):
    return pltpu.CompilerParams(dimension_semantics=sem, vmem_limit_bytes=int(vmem_mib * MIB))


def _mods_kernel(c_ref, w_ref, b_ref, o_ref):
    cv = c_ref[...]
    act = cv * jax.nn.sigmoid(cv)
    o_ref[0] = jnp.dot(act.astype(BF16), w_ref[0].astype(BF16), preferred_element_type=F32) + b_ref[0]


def _mods(c8, mod_w, mod_b):
    depth, d, n = mod_w.shape
    tn = 1024
    return pl.pallas_call(
        _mods_kernel,
        out_shape=jax.ShapeDtypeStruct((depth, 8, n), F32),
        grid=(depth, n // tn),
        in_specs=[pl.BlockSpec((8, d), lambda l, j: (0, 0)),
                  pl.BlockSpec((1, d, tn), lambda l, j: (l, 0, j)),
                  pl.BlockSpec((1, 1, tn), lambda l, j: (l, 0, j))],
        out_specs=pl.BlockSpec((1, 8, tn), lambda l, j: (l, 0, j)),
        compiler_params=_cparams(("arbitrary", "arbitrary"), 40),
        name="adaln_mods",
    )(c8, mod_w, mod_b.reshape(depth, 1, n))


def _prenorm_kernel(*refs, combine, final, d, tm, nblk, lat_blocks):
    if combine:
        (dest0_ref, destn_ref, x_ref, ys_hbm, wt_ref, mprev_ref, mcur_ref, g_ref), rest = refs[:8], refs[8:]
        outs, (ybuf, sem) = rest[:-2], rest[-2:]
    else:
        (xl_ref, xc_ref, mcur_ref, g_ref), outs = refs[:4], refs[4:]
    if not combine:
        x = jnp.where(pl.program_id(0) < lat_blocks, xl_ref[...], xc_ref[...])
    else:
        x = x_ref[...]
        i = pl.program_id(0)
        slot = i % 2
        per_slot = TOP_K * tm

        def gather(dref, s):
            for tt in range(tm):
                for k in range(TOP_K):
                    src = ys_hbm.at[pl.ds(dref[k, tt], 1)]
                    dst = ybuf.at[pl.ds(s * per_slot + k * tm + tt, 1)]
                    pltpu.make_async_copy(src, dst, sem.at[s]).start(priority=k)

        @pl.when(i == 0)
        def _():
            gather(dest0_ref, 0)

        for s_next in range(2):
            @pl.when((i + 1 < nblk) & (slot == 1 - s_next))
            def _():
                gather(destn_ref, s_next)

        base = pl.multiple_of(slot * per_slot, per_slot)
        pltpu.make_async_copy(ys_hbm.at[pl.ds(0, per_slot)], ybuf.at[pl.ds(base, per_slot)], sem.at[slot]).wait()

        def rows_of(k):
            return ybuf[pl.ds(pl.multiple_of(base + k * tm, tm), tm), :]

        gate2 = mprev_ref[:, 5 * d:6 * d]
        x = x + gate2 * (wt_ref[:, 0:1] * rows_of(0) + wt_ref[:, 1:2] * rows_of(1))
    ms = jnp.mean(x * x, axis=-1, keepdims=True)
    xn = x * lax.rsqrt(ms + NORM_EPS) * g_ref[...]
    if final:
        outs[0][...] = xn
    else:
        outs[0][...] = x
        shift = mcur_ref[:, 0:d]
        scale = mcur_ref[:, d:2 * d]
        outs[1][...] = (xn * (1.0 + scale) + shift).astype(BF16)


def _prenorm(x, moe, mods_prev, mods_cur, gain, *, n_lat, rows_per_batch, final):
    combine = moe is not None
    assert combine != isinstance(x, tuple)
    t = x.shape[0] if combine else x[0].shape[0] + x[1].shape[0]
    d = x.shape[1] if combine else x[0].shape[1]
    tm = ROW_BLOCK
    nblk = (n_lat if final else t) // tm
    lat_blocks = n_lat // tm
    per_b = rows_per_batch // tm
    ctx_row = n_lat // rows_per_batch

    def mod_idx(i):
        return (jnp.where(i < lat_blocks, i // per_b, ctx_row), 0, 0)

    row_spec = pl.BlockSpec((tm, d), lambda i: (i, 0))
    mod_spec = pl.BlockSpec((None, 1, 6 * d), mod_idx)
    g_spec = pl.BlockSpec((1, d), lambda i: (0, 0))
    ins, specs, scratch = [], [], []
    if combine:
        ys, dest, wt = moe
        ins += [dest, dest]
        specs += [pl.BlockSpec((TOP_K, tm), lambda i: (0, 0), memory_space=pltpu.SMEM),
                  pl.BlockSpec((TOP_K, tm), lambda i: (0, jnp.minimum(i + 1, nblk - 1)), memory_space=pltpu.SMEM)]
    if combine:
        ins.append(x)
        specs.append(row_spec)
    else:
        ins += list(x)
        specs += [pl.BlockSpec((tm, d), lambda i: (jnp.minimum(i, lat_blocks - 1), 0)),
                  pl.BlockSpec((tm, d), lambda i: (jnp.maximum(i - lat_blocks, 0), 0))]
    if combine:
        ins += [ys, wt, mods_prev]
        specs += [pl.BlockSpec(memory_space=pl.ANY), pl.BlockSpec((tm, TOP_K), lambda i: (i, 0)), mod_spec]
        scratch = [pltpu.VMEM((2 * TOP_K * tm, d), F32), pltpu.SemaphoreType.DMA((2,))]
    ins += [mods_cur, gain.reshape(1, d)]
    specs += [mod_spec, g_spec]
    if final:
        out_shape = jax.ShapeDtypeStruct((n_lat, d), F32)
        out_specs = row_spec
    else:
        out_shape = (jax.ShapeDtypeStruct((t, d), F32), jax.ShapeDtypeStruct((t, d), BF16))
        out_specs = (row_spec, row_spec)
    return pl.pallas_call(
        functools.partial(_prenorm_kernel, combine=combine, final=final, d=d, tm=tm, nblk=nblk,
                          lat_blocks=lat_blocks),
        out_shape=out_shape, grid=(nblk,), in_specs=specs, out_specs=out_specs, scratch_shapes=scratch,
        compiler_params=_cparams(("arbitrary",), 40),
        name="prenorm_final" if final else ("prenorm_combine" if combine else "prenorm"),
    )(*ins)


def _inproj_kernel(h_ref, w_ref, pu_ref, pr_ref):
    acc = jnp.dot(h_ref[...], w_ref[...], preferred_element_type=F32)

    @pl.when(pl.program_id(1) == 0)
    def _():
        pu_ref[...] = acc

    @pl.when(pl.program_id(1) > 0)
    def _():
        pr_ref[...] = acc.astype(pr_ref.dtype)


def _in_proj(h, w, layer, s5w):
    t, k = h.shape
    n = w.shape[2]
    tm, tn = MM_ROW_BLOCK, MM_COL_BLOCK
    assert s5w == tn
    return pl.pallas_call(
        _inproj_kernel,
        out_shape=(jax.ShapeDtypeStruct((t, s5w), F32), jax.ShapeDtypeStruct((t, n - s5w), BF16)),
        grid=(t // tm, n // tn),
        in_specs=[pl.BlockSpec((tm, k), lambda i, j: (i, 0)),
                  pl.BlockSpec((None, k, tn), lambda i, j: (layer, 0, j))],
        out_specs=(pl.BlockSpec((tm, tn), lambda i, j: (i, 0)),
                   pl.BlockSpec((tm, tn), lambda i, j: (i, jnp.maximum(j - 1, 0)))),
        compiler_params=_cparams(("arbitrary", "arbitrary"), 40),
        name="in_proj",
    )(h, w)


def _s5_weights(a_re, a_im, log_dt, b_re, b_im, c_re, c_im):
    ns = S5_STEP
    a_re, a_im = a_re.astype(F32), a_im.astype(F32)
    dt = jnp.exp(log_dt.astype(F32))[..., None]
    lam_re, lam_im = dt * a_re, dt * a_im
    tau = jnp.arange(ns + 1, dtype=F32)[:, None, None, None]
    mag = jnp.exp(tau * lam_re[None])
    p_re, p_im = mag * jnp.cos(tau * lam_im[None]), mag * jnp.sin(tau * lam_im[None])
    x, y = p_re[1] - 1.0, p_im[1]
    den = a_re * a_re + a_im * a_im
    k_re, k_im = (x * a_re + y * a_im) / den, (y * a_re - x * a_im) / den
    b_re, b_im = b_re.astype(F32), b_im.astype(F32)
    bb_re = k_re[..., None] * b_re - k_im[..., None] * b_im
    bb_im = k_re[..., None] * b_im + k_im[..., None] * b_re
    c_re, c_im = c_re.astype(F32), c_im.astype(F32)
    cp_re = c_re[None] * p_re[:, :, :, None, :] - c_im[None] * p_im[:, :, :, None, :]
    cp_im = c_re[None] * p_im[:, :, :, None, :] + c_im[None] * p_re[:, :, :, None, :]
    kmat = jnp.einsum('tdgon,dgni->tdgoi', jnp.concatenate([cp_re, -cp_im], axis=-1),
                      jnp.concatenate([bb_re, bb_im], axis=-2))
    lag = np.arange(ns)[None, :] - np.arange(ns)[:, None]
    sel_f = jnp.asarray(lag[None] == np.arange(ns)[:, None, None], F32)
    sel_b = jnp.asarray(-lag[None] == np.arange(ns)[:, None, None], F32)
    g = a_re.shape[1]
    m = (jnp.einsum('lst,lgoi->gsito', sel_f, kmat[:ns, 0]) + jnp.einsum('lst,lgoi->gsito', sel_b, kmat[:ns, 1]))
    m = m.reshape(g, ns * S5_GROUP, ns * S5_GROUP)

    def inject(p_r, p_i, d):
        r = p_r[:, :, :, None] * bb_re[d][None] - p_i[:, :, :, None] * bb_im[d][None]
        i = p_r[:, :, :, None] * bb_im[d][None] + p_i[:, :, :, None] * bb_re[d][None]
        tr = lambda z: jnp.transpose(z, (1, 0, 3, 2)).reshape(g, ns * S5_GROUP, S5_STATE)
        return tr(r), tr(i)

    pf_re, pf_im = inject(p_re[ns - 1::-1, 0], p_im[ns - 1::-1, 0], 0)
    pb_re, pb_im = inject(p_re[:ns, 1], p_im[:ns, 1], 1)
    w1 = jnp.concatenate([z.astype(BF16) for z in (m, pf_re, pb_re, pf_im, pb_im)], axis=-1)

    def carry(cp_t):
        return jnp.transpose(cp_t, (1, 3, 0, 2)).reshape(g, S5_STATE, ns * S5_GROUP)

    up = slice(1, ns + 1)
    pd_re, pd_im = p_re[ns:0:-1, 1], p_im[ns:0:-1, 1]
    cpd_re = c_re[1][None] * pd_re[:, :, None, :] - c_im[1][None] * pd_im[:, :, None, :]
    cpd_im = c_re[1][None] * pd_im[:, :, None, :] + c_im[1][None] * pd_re[:, :, None, :]
    q = jnp.concatenate([carry(cp_re[up, 0]).astype(BF16), carry(cpd_re).astype(BF16),
                         (-carry(cp_im[up, 0])).astype(BF16), (-carry(cpd_im)).astype(BF16)],
                        axis=1)
    a16 = jnp.stack([jnp.concatenate([p_re[ns, 0], p_re[ns, 1]], axis=-1),
                     jnp.concatenate([p_im[ns, 0], p_im[ns, 1]], axis=-1)], axis=1)
    return w1.astype(BF16), q.astype(BF16), a16


def _group_transpose(xs):
    ng = len(xs)
    grp = lax.broadcasted_iota(jnp.int32, xs[0].shape, 1) // S5_GROUP
    rolled = []
    for j in range(ng):
        y = xs[j % ng]
        for k in range(1, ng):
            y = jnp.where(grp == k, xs[(k + j) % ng], y)
        rolled.append(y if j == 0 else pltpu.roll(y, S5_GROUP * j, 1))
    outs = []
    for a in range(ng):
        o = rolled[(-a) % ng]
        for b in range(1, ng):
            o = jnp.where(grp == b, rolled[(b - a) % ng], o)
        outs.append(o)
    return outs


def _s5_kernel(ul_ref, uc_ref, w1_ref, q_ref, a_ref, yl_ref, yc_ref, uflat, st_re, st_im, yfl, *,
               lat_chunks, ctx_chunks):
    ng = S5_TILE_GROUPS
    width = S5_STEP * S5_GROUP
    n_chunks = lat_chunks + ctx_chunks
    sub = 32

    def flatten(src_ref, c0, row0, n):
        for hh in range(2):
            xs = [src_ref[pl.ds(row0 + hh * 8 + t, n, stride=S5_STEP), :] for t in range(8)]
            outs = _group_transpose(xs)
            for k in range(ng):
                uflat[k, pl.ds(c0, n), hh * 128:(hh + 1) * 128] = outs[k].astype(BF16)

    def flat_body(i, _):
        c0 = pl.multiple_of(i * sub, sub)
        flatten(ul_ref, c0, c0 * S5_STEP, sub)
        return 0

    lax.fori_loop(0, lat_chunks // sub, flat_body, 0)
    flatten(uc_ref, lat_chunks, 0, ctx_chunks)

    for k in range(ng):
        z = jnp.dot(uflat[k], w1_ref[k], preferred_element_type=F32)
        yfl[k] = z[:, :width]
        st_re[pl.ds(k, n_chunks, stride=ng), :] = z[:, width:width + 128]
        st_im[pl.ds(k, n_chunks, stride=ng), :] = z[:, width + 128:]

    lane = lax.broadcasted_iota(jnp.int32, (ng, 128), 1)
    fwd_lane = lane < S5_STATE
    a_re = a_ref[0]
    a_im = a_ref[1]

    def make_body(base, n):
        def body(i, carry):
            h_re, h_im = carry
            rf = pl.multiple_of((base + i) * ng, ng)
            rb = pl.multiple_of((base + n - 1 - i) * ng, ng)
            tf_re, tf_im = st_re[pl.ds(rf, ng), :], st_im[pl.ds(rf, ng), :]
            tb_re, tb_im = st_re[pl.ds(rb, ng), :], st_im[pl.ds(rb, ng), :]
            t_re = jnp.where(fwd_lane, tf_re, tb_re)
            t_im = jnp.where(fwd_lane, tf_im, tb_im)
            n_re = a_re * h_re - a_im * h_im + t_re
            n_im = a_re * h_im + a_im * h_re + t_im
            st_re[pl.ds(rf, ng), :] = jnp.where(fwd_lane, h_re, tf_re)
            st_im[pl.ds(rf, ng), :] = jnp.where(fwd_lane, h_im, tf_im)
            st_re[pl.ds(rb, ng), :] = jnp.where(fwd_lane, tb_re, h_re)
            st_im[pl.ds(rb, ng), :] = jnp.where(fwd_lane, tb_im, h_im)
            return n_re, n_im
        return body

    carry = (jnp.zeros((ng, 128), F32), jnp.zeros((ng, 128), F32))
    carry = lax.fori_loop(0, ctx_chunks, make_body(lat_chunks, ctx_chunks), carry)
    carry = lax.fori_loop(0, lat_chunks, make_body(0, lat_chunks), carry)

    for k in range(ng):
        hk = jnp.concatenate([st_re[pl.ds(k, n_chunks, stride=ng), :], st_im[pl.ds(k, n_chunks, stride=ng), :]],
                             axis=1).astype(BF16)
        yfl[k] = yfl[k] + jnp.dot(hk, q_ref[k], preferred_element_type=F32)

    def unflatten(dst_ref, c0, row0, n):
        for hh in range(2):
            vs = [yfl[k, pl.ds(c0, n), hh * 128:(hh + 1) * 128] for k in range(ng)]
            ts = _group_transpose(vs)
            for t in range(8):
                dst_ref[pl.ds(row0 + hh * 8 + t, n, stride=S5_STEP), :] = ts[t]

    def unflat_body(i, _):
        c0 = pl.multiple_of(i * sub, sub)
        unflatten(yl_ref, c0, c0 * S5_STEP, sub)
        return 0

    lax.fori_loop(0, lat_chunks // sub, unflat_body, 0)
    unflatten(yc_ref, lat_chunks, 0, ctx_chunks)


def _s5_scan(pu, w1, q, a16, layer, *, bsz, n_lat, n_ctx):
    g = w1.shape[1]
    ng = S5_TILE_GROUPS
    width = S5_STEP * S5_GROUP
    lat_chunks, ctx_chunks = n_lat // S5_STEP, n_ctx // S5_STEP
    assert lat_chunks % 32 == 0 and ctx_chunks % 2 == 0 and ctx_chunks % 16 == 0
    n_chunks = lat_chunks + ctx_chunks
    ctx_blk0 = bsz * n_lat // n_ctx
    a_t = jnp.transpose(a16.reshape(-1, g // ng, ng, 2, 128), (0, 1, 3, 2, 4))
    kern = functools.partial(_s5_kernel, lat_chunks=lat_chunks, ctx_chunks=ctx_chunks)
    return pl.pallas_call(
        kern,
        out_shape=(jax.ShapeDtypeStruct((bsz * n_lat, g * S5_GROUP), F32),
                   jax.ShapeDtypeStruct((bsz * n_ctx, g * S5_GROUP), F32)),
        grid=(g // ng, bsz),
        in_specs=[pl.BlockSpec((n_lat, 128), lambda i, b: (b, i)),
                  pl.BlockSpec((n_ctx, 128), lambda i, b: (ctx_blk0 + b, i)),
                  pl.BlockSpec((None, ng, width, 2 * width), lambda i, b: (layer, i, 0, 0)),
                  pl.BlockSpec((None, ng, width, width), lambda i, b: (layer, i, 0, 0)),
                  pl.BlockSpec((None, None, 2, ng, 128), lambda i, b: (layer, i, 0, 0, 0))],
        out_specs=(pl.BlockSpec((n_lat, 128), lambda i, b: (b, i)),
                   pl.BlockSpec((n_ctx, 128), lambda i, b: (b, i))),
        scratch_shapes=[pltpu.VMEM((ng, n_chunks, width), BF16),
                        pltpu.VMEM((n_chunks * ng, 128), F32),
                        pltpu.VMEM((n_chunks * ng, 128), F32),
                        pltpu.VMEM((ng, n_chunks, width), F32)],
        compiler_params=_cparams(("arbitrary", "arbitrary"), 40),
        name="s5_scan",
    )(pu, pu, w1, q, a_t)


def _ret_kernel(lg_ref, ql_ref, kl_ref, vl_ref, gl_ref, qc_ref, kc_ref, vc_ref, gc_ref, cos_ref, sin_ref,
                ol_ref, oc_ref, qs, ks, vs, acc, sf, sb, dm, *, n_ctx, n_lat):
    c = RET_BLOCK
    half = RET_HEAD_DIM // 2
    head = pl.program_id(1)
    lgf = lg_ref[0, head]
    lgb = lg_ref[1, head]
    qscale = RET_HEAD_DIM ** -0.5
    ctx_rows = n_ctx * c

    ii = lax.broadcasted_iota(jnp.int32, (c, c), 0)
    jj = lax.broadcasted_iota(jnp.int32, (c, c), 1)
    rel = (ii - jj).astype(F32)
    dsum = (jnp.where(rel >= 0, jnp.exp(lgf * jnp.maximum(rel, 0.0)), 0.0)
            + jnp.where(rel <= 0, jnp.exp(lgb * jnp.maximum(-rel, 0.0)), 0.0))
    pos = lax.broadcasted_iota(jnp.int32, (c, 1), 0).astype(F32)
    qdec_f = jnp.exp(lgf * (pos + 1.0))
    kdec_f = jnp.exp(lgf * (c - 1.0 - pos))
    qdec_b = jnp.exp(lgb * (c - pos))
    kdec_b = jnp.exp(lgb * pos)
    cg_f = jnp.exp(lgf * c)
    cg_b = jnp.exp(lgb * c)

    qs[0:ctx_rows, :] = (qc_ref[...].astype(F32) * qscale).astype(BF16)
    ks[0:ctx_rows, :] = kc_ref[...]
    vs[0:ctx_rows, :] = vc_ref[...]

    def rope_body(n, _):
        r = pl.multiple_of(n * c, c)
        cs = cos_ref[pl.ds(r, c), :]
        sn = sin_ref[pl.ds(r, c), :]

        def rot(t):
            t1, t2 = t[:, :half], t[:, half:]
            return jnp.concatenate([t1 * cs - t2 * sn, t2 * cs + t1 * sn], axis=1)

        dst = pl.multiple_of(ctx_rows + r, c)
        qs[pl.ds(dst, c), :] = (rot(ql_ref[pl.ds(r, c), :].astype(F32)) * qscale).astype(BF16)
        ks[pl.ds(dst, c), :] = rot(kl_ref[pl.ds(r, c), :].astype(F32)).astype(BF16)
        vs[pl.ds(dst, c), :] = vl_ref[pl.ds(r, c), :]
        return 0

    lax.fori_loop(0, n_lat, rope_body, 0)

    dm[...] = dsum
    sf[...] = jnp.zeros_like(sf)
    sb[...] = jnp.zeros_like(sb)

    def load(r):
        return qs[pl.ds(r, c), :], ks[pl.ds(r, c), :], vs[pl.ds(r, c), :]

    def intra_of(q, k, v):
        s = lax.dot_general(q, k, (((1,), (1,)), ((), ())), preferred_element_type=F32)
        return jnp.dot((s * dm[...]).astype(BF16), v, preferred_element_type=F32)

    def step(s_ref, q, k, v, qdec, kdec, cg):
        inter = jnp.dot(q, s_ref[...].astype(BF16), preferred_element_type=F32) * qdec
        kd = (k.astype(F32) * kdec).astype(BF16)
        s_ref[...] = cg * s_ref[...] + lax.dot_general(kd, v, (((0,), (0,)), ((), ())),
                                                       preferred_element_type=F32)
        return inter

    def finish(o, gate, out_ref, ro):
        o = o * lax.rsqrt(jnp.mean(o * o, axis=-1, keepdims=True) + NORM_EPS)
        out_ref[pl.ds(ro, c), :] = (o * (gate * jax.nn.sigmoid(gate))).astype(out_ref.dtype)

    fwd = (sf, qdec_f, kdec_f, cg_f)
    bwd = (sb, qdec_b, kdec_b, cg_b)

    for n in range(n_ctx):
        q, k, v = load(n * c)
        acc[n * c:(n + 1) * c, :] = intra_of(q, k, v) + step(fwd[0], q, k, v, *fwd[1:])
    for n in range(n_ctx - 1, -1, -1):
        q, k, v = load(n * c)
        o = acc[n * c:(n + 1) * c, :] + step(bwd[0], q, k, v, *bwd[1:])
        finish(o, gc_ref[n * c:(n + 1) * c, :].astype(F32), oc_ref, n * c)

    def first_visits(i, _):
        rf = pl.multiple_of(i * c, c)
        rb = pl.multiple_of((n_lat - 1 - i) * c, c)
        q, k, v = load(ctx_rows + rf)
        acc[pl.ds(ctx_rows + rf, c), :] = intra_of(q, k, v) + step(fwd[0], q, k, v, *fwd[1:])
        q, k, v = load(ctx_rows + rb)
        acc[pl.ds(ctx_rows + rb, c), :] = step(bwd[0], q, k, v, *bwd[1:])
        return 0

    def second_visits(i, _):
        rf = pl.multiple_of(i * c, c)
        rb = pl.multiple_of((n_lat - 1 - i) * c, c)
        q, k, v = load(ctx_rows + rf)
        o = acc[pl.ds(ctx_rows + rf, c), :] + intra_of(q, k, v) + step(fwd[0], q, k, v, *fwd[1:])
        finish(o, gl_ref[pl.ds(rf, c), :].astype(F32), ol_ref, rf)
        q, k, v = load(ctx_rows + rb)
        o = acc[pl.ds(ctx_rows + rb, c), :] + step(bwd[0], q, k, v, *bwd[1:])
        finish(o, gl_ref[pl.ds(rb, c), :].astype(F32), ol_ref, rb)
        return 0

    lax.fori_loop(0, n_lat // 2, first_visits, 0)
    lax.fori_loop(n_lat // 2, n_lat, second_visits, 0)


def _retention(p, log_gamma, cos, sin, *, bsz, n_lat, n_ctx, col0):
    t = p.shape[0]
    dh = RET_HEAD_DIM
    heads = (p.shape[1] - col0) // (4 * dh)
    qb, kb, vb, gb = (col0 // dh + i * heads for i in range(4))
    ctx_blk0 = bsz * n_lat // n_ctx

    def lat(cb):
        return pl.BlockSpec((n_lat, dh), lambda b, h: (b, cb + h))

    def ctx(cb):
        return pl.BlockSpec((n_ctx, dh), lambda b, h: (ctx_blk0 + b, cb + h))

    tab = pl.BlockSpec((n_lat, dh // 2), lambda b, h: (0, 0))
    assert n_ctx % RET_BLOCK == 0 and (n_lat // RET_BLOCK) % 2 == 0
    kern = functools.partial(_ret_kernel, n_ctx=n_ctx // RET_BLOCK, n_lat=n_lat // RET_BLOCK)
    seq = n_ctx + n_lat
    return pl.pallas_call(
        kern,
        out_shape=(jax.ShapeDtypeStruct((bsz * n_lat, heads * dh), BF16),
                   jax.ShapeDtypeStruct((bsz * n_ctx, heads * dh), BF16)),
        grid=(bsz, heads),
        in_specs=[pl.BlockSpec(memory_space=pltpu.SMEM),
                  lat(qb), lat(kb), lat(vb), lat(gb), ctx(qb), ctx(kb), ctx(vb), ctx(gb), tab, tab],
        out_specs=(pl.BlockSpec((n_lat, dh), lambda b, h: (b, h)),
                   pl.BlockSpec((n_ctx, dh), lambda b, h: (b, h))),
        scratch_shapes=[pltpu.VMEM((seq, dh), BF16), pltpu.VMEM((seq, dh), BF16), pltpu.VMEM((seq, dh), BF16),
                        pltpu.VMEM((seq, dh), F32), pltpu.VMEM((dh, dh), F32), pltpu.VMEM((dh, dh), F32),
                        pltpu.VMEM((RET_BLOCK, RET_BLOCK), F32)],
        compiler_params=_cparams(("arbitrary", "arbitrary"), 52),
        name="retention",
    )(log_gamma, p, p, p, p, p, p, p, p, cos, sin)


def _top2_sum(a, b, c, d):
    hi1, lo1 = jnp.maximum(a, b), jnp.minimum(a, b)
    hi2, lo2 = jnp.maximum(c, d), jnp.minimum(c, d)
    return jnp.maximum(hi1, hi2) + jnp.maximum(jnp.minimum(hi1, hi2), jnp.maximum(lo1, lo2))


def _route(logits_t, rb):
    sc = jax.nn.sigmoid(logits_t)
    biased = sc + rb
    per = EXPERTS_PER_GROUP
    brow = [biased[e:e + 1, :] for e in range(N_EXPERTS)]
    srow = [sc[e:e + 1, :] for e in range(N_EXPERTS)]
    gscore = [_top2_sum(*brow[per * g:per * g + per]) for g in range(N_EXPERT_GROUPS)]
    best_v = gscore[0]
    best_g = jnp.zeros_like(best_v, dtype=jnp.int32)
    for g in range(1, N_EXPERT_GROUPS):
        upd = gscore[g] > best_v
        best_v = jnp.where(upd, gscore[g], best_v)
        best_g = jnp.where(upd, g, best_g)
    vals, sels = [], []
    for j in range(per):
        v = brow[j]
        s = srow[j]
        for g in range(1, N_EXPERT_GROUPS):
            v = jnp.where(best_g == g, brow[per * g + j], v)
            s = jnp.where(best_g == g, srow[per * g + j], s)
        vals.append(v)
        sels.append(s)
    v1, i1, s1 = vals[0], jnp.zeros_like(best_g), sels[0]
    for j in range(1, per):
        upd = vals[j] > v1
        v1 = jnp.where(upd, vals[j], v1)
        s1 = jnp.where(upd, sels[j], s1)
        i1 = jnp.where(upd, j, i1)
    v2 = jnp.full_like(v1, -jnp.inf)
    i2, s2 = jnp.zeros_like(best_g), jnp.zeros_like(s1)
    for j in range(per):
        upd = (i1 != j) & (vals[j] > v2)
        v2 = jnp.where(upd, vals[j], v2)
        s2 = jnp.where(upd, sels[j], s2)
        i2 = jnp.where(upd, j, i2)
    tot = s1 + s2
    idx = jnp.concatenate([best_g * per + i1, best_g * per + i2], axis=0)
    wts = jnp.concatenate([s1 / tot, s2 / tot], axis=0)
    return idx, wts


def _mix_kernel(ysl_ref, ysc_ref, u_ref, rl_ref, rc_ref, x_ref, m_ref, dsk_ref, gw_ref, gb_ref, ws_ref, wr_ref,
                g2_ref, rw_ref, rb_ref, xo_ref, fu_ref, ei_ref, ew_ref, rank_ref, cnt_ref, run_ref, *, lat_blocks, d):
    is_lat = pl.program_id(0) < lat_blocks
    ys = jnp.where(is_lat, ysl_ref[...], ysc_ref[...])
    y = ys + dsk_ref[...] * u_ref[...]
    y = jax.nn.gelu(y, approximate=True)
    z = jnp.dot(y.astype(BF16), gw_ref[...], preferred_element_type=F32) + gb_ref[...]
    s5o = (y * jax.nn.sigmoid(z)).astype(BF16)
    r = jnp.where(is_lat, rl_ref[...], rc_ref[...])
    mixed = (jnp.dot(s5o, ws_ref[...], preferred_element_type=F32)
             + jnp.dot(r, wr_ref[...], preferred_element_type=F32))
    gate1 = m_ref[:, 2 * d:3 * d]
    x = x_ref[...] + gate1 * mixed
    xo_ref[...] = x
    ms = jnp.mean(x * x, axis=-1, keepdims=True)
    f = x * lax.rsqrt(ms + NORM_EPS) * g2_ref[...]
    f = f * (1.0 + m_ref[:, 4 * d:5 * d]) + m_ref[:, 3 * d:4 * d]
    tm = f.shape[0]
    fu_ref[...] = f
    logits_t = lax.dot_general(rw_ref[...], f, (((1,), (1,)), ((), ())), precision=HIGHEST,
                               preferred_element_type=F32)
    idx, wts = _route(logits_t, rb_ref[...])
    ei_ref[...] = idx
    ew_ref[...] = wts

    @pl.when(pl.program_id(0) == 0)
    def _():
        run_ref[...] = jnp.zeros_like(run_ref)

    e_iota = lax.broadcasted_iota(jnp.int32, (N_EXPERTS, tm), 0)
    tri = jnp.where(lax.broadcasted_iota(jnp.int32, (tm, tm), 0) <= lax.broadcasted_iota(jnp.int32, (tm, tm), 1),
                    1.0, 0.0).astype(BF16)
    hit0 = e_iota == idx[0:1, :]
    hit1 = e_iota == idx[1:2, :]
    p0 = jnp.dot(jnp.where(hit0, 1.0, 0.0).astype(BF16), tri, preferred_element_type=F32)
    p1 = jnp.dot(jnp.where(hit1, 1.0, 0.0).astype(BF16), tri, preferred_element_type=F32)
    tot0 = p0[:, tm - 1:tm]
    tot1 = p1[:, tm - 1:tm]
    before = run_ref[...]
    r0 = jnp.sum(jnp.where(hit0, before + p0, 0.0), axis=0, keepdims=True) - 1.0
    r1 = jnp.sum(jnp.where(hit1, before + tot0 + p1, 0.0), axis=0, keepdims=True) - 1.0
    rank_ref[...] = jnp.concatenate([r0, r1], axis=0).astype(jnp.int32)
    after = before + tot0 + tot1
    run_ref[...] = after
    cnt_ref[...] = after.astype(jnp.int32)


def _mix(ysl, ysc, pu, rl, rc, x, mods_cur, d_skip, glu_w, glu_b, w_out, g2, rw_t, rb, *, layer, n_lat,
         rows_per_batch):
    t, d = x.shape
    sw = pu.shape[1]
    tm = MIX_ROW_BLOCK
    lat_blocks = n_lat // tm
    per_b = rows_per_batch // tm
    ctx_row = n_lat // rows_per_batch
    const = lambda i: (0, 0)
    row = lambda i: (i, 0)
    lat_row = lambda i: (jnp.minimum(i, lat_blocks - 1), 0)
    ctx_rowblk = lambda i: (jnp.maximum(i - lat_blocks, 0), 0)
    once = pl.Buffered(1)
    kern = functools.partial(_mix_kernel, lat_blocks=lat_blocks, d=d)
    return pl.pallas_call(
        kern,
        out_shape=(jax.ShapeDtypeStruct((t, d), F32), jax.ShapeDtypeStruct((t, d), F32),
                   jax.ShapeDtypeStruct((TOP_K, t), jnp.int32), jax.ShapeDtypeStruct((TOP_K, t), F32),
                   jax.ShapeDtypeStruct((TOP_K, t), jnp.int32), jax.ShapeDtypeStruct((N_EXPERTS, 1), jnp.int32)),
        grid=(t // tm,),
        scratch_shapes=[pltpu.VMEM((N_EXPERTS, 1), F32)],
        in_specs=[pl.BlockSpec((tm, sw), lat_row),
                  pl.BlockSpec((tm, sw), ctx_rowblk, pipeline_mode=once),
                  pl.BlockSpec((tm, sw), row),
                  pl.BlockSpec((tm, sw), lat_row),
                  pl.BlockSpec((tm, sw), ctx_rowblk, pipeline_mode=once),
                  pl.BlockSpec((tm, d), row),
                  pl.BlockSpec((None, 1, 6 * d), lambda i: (jnp.where(i < lat_blocks, i // per_b, ctx_row), 0, 0)),
                  pl.BlockSpec((1, sw), const),
                  pl.BlockSpec((None, sw, sw), lambda i: (layer, 0, 0), pipeline_mode=once),
                  pl.BlockSpec((1, sw), const),
                  pl.BlockSpec((None, sw, d), lambda i: (layer, 0, 0), pipeline_mode=once),
                  pl.BlockSpec((None, d - sw, d), lambda i: (layer, 1, 0), pipeline_mode=once),
                  pl.BlockSpec((1, d), const),
                  pl.BlockSpec((N_EXPERTS, d), const),
                  pl.BlockSpec((N_EXPERTS, 1), const)],
        out_specs=(pl.BlockSpec((tm, d), row), pl.BlockSpec((tm, d), row),
                   pl.BlockSpec((TOP_K, tm), lambda i: (0, i)), pl.BlockSpec((TOP_K, tm), lambda i: (0, i)),
                   pl.BlockSpec((TOP_K, tm), lambda i: (0, i)), pl.BlockSpec((N_EXPERTS, 1), const)),
        compiler_params=_cparams(("arbitrary",), 58),
        name="mix",
    )(ysl, ysc, pu, rl, rc, x, mods_cur, d_skip, glu_w, glu_b, w_out, w_out, g2, rw_t, rb)


def _dispatch_plan(eidx, rank, counts):
    bm = MOE_ROWS
    n_assign = eidx.size
    nblk = -(-n_assign // bm) + N_EXPERTS
    counts = counts.reshape(N_EXPERTS)
    padded = (counts + bm - 1) // bm * bm
    pend = jnp.cumsum(padded)
    pstart = pend - padded
    experts = jnp.arange(N_EXPERTS, dtype=jnp.int32)[:, None, None]
    dest = rank + jnp.sum(jnp.where(eidx[None] == experts, pstart[:, None, None], 0), axis=0)
    blk_start = jnp.arange(nblk, dtype=jnp.int32) * bm
    bexp = jnp.minimum(jnp.sum((pend[None, :] <= blk_start[:, None]).astype(jnp.int32), axis=1), N_EXPERTS - 1)
    nused = (pend[-1] // bm).astype(jnp.int32).reshape(1)
    return dest.astype(jnp.int32), bexp.astype(jnp.int32), nused, counts.astype(jnp.int32), pstart.astype(jnp.int32)


def _dispatch_kernel(cnt_ref, pst_ref, nused_ref, dest_ref, fu_ref, wg_ref, wu_ref, wd_ref,
                     xs_hbm, wgb_ref, wub_ref, wdb_ref, zblk, sem, zsem, *, tb, bm, nblk, n_conv):
    i = pl.program_id(0)

    for tt in range(tb):
        for k in range(TOP_K):
            pltpu.make_async_copy(fu_ref.at[pl.ds(tt, 1)], xs_hbm.at[pl.ds(dest_ref[k, tt], 1)],
                                  sem).start(priority=k)

    @pl.when(i < n_conv)
    def _():
        wgb_ref[...] = wg_ref[...].astype(BF16)
        wub_ref[...] = wu_ref[...].astype(BF16)
        wdb_ref[...] = wd_ref[...].astype(BF16)

    @pl.when(i == 0)
    def _():
        zblk[...] = jnp.zeros_like(zblk)
        zrow = zblk.at[pl.ds(0, 1)]

        def pad_rows(e, _):
            first = pst_ref[e] + cnt_ref[e]
            npad = (bm - cnt_ref[e] % bm) % bm

            def start(r, _):
                pltpu.make_async_copy(zrow, xs_hbm.at[pl.ds(first + r, 1)], zsem).start()
                return 0

            def wait(r, _):
                pltpu.make_async_copy(zrow, xs_hbm.at[pl.ds(0, 1)], zsem).wait()
                return 0

            lax.fori_loop(0, npad, start, 0)
            lax.fori_loop(0, npad, wait, 0)
            return 0

        lax.fori_loop(0, N_EXPERTS, pad_rows, 0)

        def pad_block(j, _):
            cp = pltpu.make_async_copy(zblk, xs_hbm.at[pl.ds(pl.multiple_of(j * bm, bm), bm)], zsem)
            cp.start()
            cp.wait()
            return 0

        lax.fori_loop(nused_ref[0], nblk, pad_block, 0)

    for k in range(TOP_K):
        pltpu.make_async_copy(fu_ref, xs_hbm.at[pl.ds(0, tb)], sem).wait()


def _dispatch(fu, dest, counts, pstart, nused, wg, wu, wd, layer, *, nblk):
    n_tok, d = fu.shape
    bm = MOE_ROWS
    tb = ROW_BLOCK
    assert n_tok % tb == 0
    steps = n_tok // tb
    depth, n_exp, _, ff = wg.shape
    n_conv = 1 << (steps.bit_length() - 1)
    rows_up, rows_dn = n_exp * d // n_conv, n_exp * ff // n_conv
    assert rows_up * n_conv == n_exp * d and rows_dn * n_conv == n_exp * ff and rows_dn % 16 == 0
    slab = lambda i, c, p, n: (layer, jnp.minimum(i, n_conv - 1), 0)
    out_slab = lambda i, c, p, n: (jnp.minimum(i, n_conv - 1), 0)
    kern = functools.partial(_dispatch_kernel, tb=tb, bm=bm, nblk=nblk, n_conv=n_conv)
    grid_spec = pltpu.PrefetchScalarGridSpec(
        num_scalar_prefetch=3,
        grid=(steps,),
        in_specs=[pl.BlockSpec((TOP_K, tb), lambda i, c, p, n: (0, i), memory_space=pltpu.SMEM),
                  pl.BlockSpec((tb, d), lambda i, c, p, n: (i, 0)),
                  pl.BlockSpec((None, rows_up, ff), slab),
                  pl.BlockSpec((None, rows_up, ff), slab),
                  pl.BlockSpec((None, rows_dn, d), slab)],
        out_specs=(pl.BlockSpec(memory_space=pl.ANY),
                   pl.BlockSpec((rows_up, ff), out_slab),
                   pl.BlockSpec((rows_up, ff), out_slab),
                   pl.BlockSpec((rows_dn, d), out_slab)),
        scratch_shapes=[pltpu.VMEM((bm, d), F32),
                        pltpu.SemaphoreType.DMA(()), pltpu.SemaphoreType.DMA(())],
    )
    xs, wgb, wub, wdb = pl.pallas_call(
        kern,
        out_shape=(jax.ShapeDtypeStruct((nblk * bm, d), F32),
                   jax.ShapeDtypeStruct((n_exp * d, ff), BF16),
                   jax.ShapeDtypeStruct((n_exp * d, ff), BF16),
                   jax.ShapeDtypeStruct((n_exp * ff, d), BF16)),
        grid_spec=grid_spec,
        compiler_params=_cparams(("arbitrary",), 40),
        name="dispatch",
    )(counts, pstart, nused, dest, fu, wg.reshape(depth, n_exp * d, ff), wu.reshape(depth, n_exp * d, ff),
      wd.reshape(depth, n_exp * ff, d))
    return xs, wgb.reshape(n_exp, d, ff), wub.reshape(n_exp, d, ff), wdb.reshape(n_exp, ff, d)


def _expert_kernel(bexp_ref, nused_ref, xs_ref, wg_ref, wu_ref, wd_ref, ys_ref):
    @pl.when(pl.program_id(0) < nused_ref[0])
    def _():
        xb = xs_ref[...].astype(BF16)
        gt = jnp.dot(xb, wg_ref[...], preferred_element_type=F32)
        up = jnp.dot(xb, wu_ref[...], preferred_element_type=F32)
        hmid = (gt * jax.nn.sigmoid(gt) * up).astype(BF16)
        ys_ref[...] = jnp.dot(hmid, wd_ref[...], preferred_element_type=F32)

    @pl.when(pl.program_id(0) >= nused_ref[0])
    def _():
        ys_ref[...] = jnp.zeros_like(ys_ref)


def _experts(xs, bexp, nused, wg, wu, wd):
    bm = MOE_ROWS
    nblk = bexp.shape[0]
    d, ff = wg.shape[1], wg.shape[2]
    grid_spec = pltpu.PrefetchScalarGridSpec(
        num_scalar_prefetch=2,
        grid=(nblk,),
        in_specs=[pl.BlockSpec((bm, d), lambda j, be, nu: (j, 0)),
                  pl.BlockSpec((None, d, ff), lambda j, be, nu: (be[j], 0, 0)),
                  pl.BlockSpec((None, d, ff), lambda j, be, nu: (be[j], 0, 0)),
                  pl.BlockSpec((None, ff, d), lambda j, be, nu: (be[j], 0, 0))],
        out_specs=pl.BlockSpec((bm, d), lambda j, be, nu: (j, 0)),
    )
    return pl.pallas_call(
        _expert_kernel,
        out_shape=jax.ShapeDtypeStruct((nblk * bm, d), F32),
        grid_spec=grid_spec,
        compiler_params=_cparams(("arbitrary",), 52),
        name="experts",
    )(bexp, nused, xs, wg, wu, wd)


def _rope_tables(n_lat):
    n_rows = n_lat // GRID_W
    pos = np.arange(n_lat)
    quarter = RET_HEAD_DIM // 4
    freqs = ROPE_BASE ** (-np.arange(quarter, dtype=np.float64) / quarter)
    ang = np.concatenate([(pos // GRID_W)[:, None] * freqs, (pos % GRID_W)[:, None] * freqs], axis=-1)
    assert ang.shape == (n_rows * GRID_W, RET_HEAD_DIM // 2)
    return jnp.asarray(np.cos(ang), F32), jnp.asarray(np.sin(ang), F32)


def kernel(x, c, ctx, c_ctx, mod_w, mod_b, norm1_g, norm2_g, w_in, s5_a_re, s5_a_im, s5_log_dt, s5_b_re, s5_b_im,
           s5_c_re, s5_c_im, s5_d, glu_w, glu_b, ret_decay_raw, w_out, router_w, router_b, exp_w_gate, exp_w_up,
           exp_w_down, final_norm_g):
    bsz, n_lat, d = x.shape
    n_ctx = ctx.shape[1]
    depth = mod_w.shape[0]
    s5w = s5_d.shape[1]
    t = bsz * (n_lat + n_ctx)
    lat_total = bsz * n_lat

    xs = (x.reshape(lat_total, d), ctx.reshape(bsz * n_ctx, d))
    c8 = jnp.zeros((8, d), F32).at[:bsz].set(c).at[bsz].set(c_ctx)
    mods = _mods(c8, mod_w, mod_b).reshape(depth, 8, 1, 6 * d)
    cos, sin = _rope_tables(n_lat)
    rw_t = router_w.T
    rb = router_b.reshape(N_EXPERTS, 1)

    w_in_b, glu_w_b, w_out_b = w_in.astype(BF16), glu_w.astype(BF16), w_out.astype(BF16)
    assert w_out.shape[1] == 2 * s5w
    w1, q, a16 = jax.vmap(_s5_weights)(s5_a_re, s5_a_im, s5_log_dt, s5_b_re, s5_b_im, s5_c_re, s5_c_im)

    moe = None
    for layer in range(depth):
        xs, h = _prenorm(xs, moe, mods[layer - 1] if layer else None, mods[layer], norm1_g[layer],
                         n_lat=lat_total, rows_per_batch=n_lat, final=False)
        pu, pr = _in_proj(h, w_in_b, layer, s5w)

        ysl, ysc = _s5_scan(pu, w1, q, a16, layer, bsz=bsz, n_lat=n_lat, n_ctx=n_ctx)

        log_gamma = jnp.log1p(-jnp.exp2(-ret_decay_raw[layer].astype(F32)))
        rl, rc = _retention(pr, log_gamma, cos, sin, bsz=bsz, n_lat=n_lat, n_ctx=n_ctx, col0=0)

        xs, fu, eidx, ew, rank, counts = _mix(
            ysl, ysc, pu, rl, rc, xs, mods[layer], s5_d[layer].reshape(1, s5w), glu_w_b,
            glu_b[layer].reshape(1, s5w), w_out_b, norm2_g[layer].reshape(1, d), rw_t, rb,
            layer=layer, n_lat=lat_total, rows_per_batch=n_lat)

        dest, bexp, nused, counts, pstart = _dispatch_plan(eidx, rank, counts)
        xsort, wg_b, wu_b, wd_b = _dispatch(fu, dest, counts, pstart, nused, exp_w_gate, exp_w_up, exp_w_down,
                                            layer, nblk=bexp.shape[0])
        ys = _experts(xsort, bexp, nused, wg_b, wu_b, wd_b)
        moe = (ys, dest, ew.T)

    out = _prenorm(xs, moe, mods[depth - 1], mods[depth - 1], final_norm_g,
                   n_lat=lat_total, rows_per_batch=n_lat, final=True)
    return out.reshape(bsz, n_lat, d)
```

```python
import functools

import jax
import jax.numpy as jnp
import numpy as np
from jax import lax
from jax.experimental import pallas as pl
from jax.experimental.pallas import tpu as pltpu

F32 = jnp.float32
BF16 = jnp.bfloat16
HIGHEST = lax.Precision.HIGHEST

GRID_W = 64
S5_GROUP = 16
S5_STATE = 64
S5_STEP = 16
RET_HEAD_DIM = 256
RET_BLOCK = 256
ROPE_BASE = 10000.0
N_EXPERTS = 16
N_EXPERT_GROUPS = 4
EXPERTS_PER_GROUP = 4
TOP_K = 2
NORM_EPS = 1e-6

ROW_BLOCK = 256
MIX_ROW_BLOCK = 512
MM_ROW_BLOCK = 1024
MM_COL_BLOCK = 1024
MOE_ROWS = 256
S5_TILE_GROUPS = 128 // S5_GROUP
MIB = 1024 * 1024


def _cparams(sem, vmem_mib):
    return pltpu.CompilerParams(dimension_semantics=sem, vmem_limit_bytes=int(vmem_mib * MIB))


def _mods_kernel(c_ref, w_ref, b_ref, o_ref):
    cv = c_ref[...]
    act = cv * jax.nn.sigmoid(cv)
    o_ref[0] = jnp.dot(act.astype(BF16), w_ref[0].astype(BF16), preferred_element_type=F32) + b_ref[0]


def _mods(c8, mod_w, mod_b):
    depth, d, n = mod_w.shape
    tn = 2048
    return pl.pallas_call(
        _mods_kernel,
        out_shape=jax.ShapeDtypeStruct((depth, 8, n), F32),
        grid=(depth, n // tn),
        in_specs=[pl.BlockSpec((8, d), lambda l, j: (0, 0)),
                  pl.BlockSpec((1, d, tn), lambda l, j: (l, 0, j)),
                  pl.BlockSpec((1, 1, tn), lambda l, j: (l, 0, j))],
        out_specs=pl.BlockSpec((1, 8, tn), lambda l, j: (l, 0, j)),
        compiler_params=_cparams(("arbitrary", "arbitrary"), 52),
        name="adaln_mods",
    )(c8, mod_w, mod_b.reshape(depth, 1, n))


def _prenorm_kernel(*refs, combine, final, d, tm, nblk, lat_blocks):
    if combine:
        (dest0_ref, destn_ref, x_ref, ys_hbm, wt_ref, mprev_ref, mcur_ref, g_ref), rest = refs[:8], refs[8:]
        outs, (ybuf, sem) = rest[:-2], rest[-2:]
    else:
        (xl_ref, xc_ref, mcur_ref, g_ref), outs = refs[:4], refs[4:]
    if not combine:
        x = jnp.where(pl.program_id(0) < lat_blocks, xl_ref[...], xc_ref[...])
    else:
        x = x_ref[...]
        i = pl.program_id(0)
        slot = i % 2
        per_slot = TOP_K * tm

        def gather(dref, s):
            for tt in range(tm):
                for k in range(TOP_K):
                    src = ys_hbm.at[pl.ds(dref[k, tt], 1)]
                    dst = ybuf.at[pl.ds(s * per_slot + k * tm + tt, 1)]
                    pltpu.make_async_copy(src, dst, sem.at[s]).start(priority=k)

        @pl.when(i == 0)
        def _():
            gather(dest0_ref, 0)

        for s_next in range(2):
            @pl.when((i + 1 < nblk) & (slot == 1 - s_next))
            def _():
                gather(destn_ref, s_next)

        base = pl.multiple_of(slot * per_slot, per_slot)
        pltpu.make_async_copy(ys_hbm.at[pl.ds(0, per_slot)], ybuf.at[pl.ds(base, per_slot)], sem.at[slot]).wait()

        def rows_of(k):
            return ybuf[pl.ds(pl.multiple_of(base + k * tm, tm), tm), :]

        gate2 = mprev_ref[:, 5 * d:6 * d]
        x = x + gate2 * (wt_ref[:, 0:1] * rows_of(0) + wt_ref[:, 1:2] * rows_of(1))
    ms = jnp.mean(x * x, axis=-1, keepdims=True)
    xn = x * lax.rsqrt(ms + NORM_EPS) * g_ref[...]
    if final:
        outs[0][...] = xn
    else:
        outs[0][...] = x
        shift = mcur_ref[:, 0:d]
        scale = mcur_ref[:, d:2 * d]
        outs[1][...] = (xn * (1.0 + scale) + shift).astype(BF16)


def _prenorm(x, moe, mods_prev, mods_cur, gain, *, n_lat, rows_per_batch, final):
    combine = moe is not None
    assert combine != isinstance(x, tuple)
    t = x.shape[0] if combine else x[0].shape[0] + x[1].shape[0]
    d = x.shape[1] if combine else x[0].shape[1]
    tm = ROW_BLOCK
    nblk = (n_lat if final else t) // tm
    lat_blocks = n_lat // tm
    per_b = rows_per_batch // tm
    ctx_row = n_lat // rows_per_batch

    def mod_idx(i):
        return (jnp.where(i < lat_blocks, i // per_b, ctx_row), 0, 0)

    row_spec = pl.BlockSpec((tm, d), lambda i: (i, 0))
    mod_spec = pl.BlockSpec((None, 1, 6 * d), mod_idx)
    g_spec = pl.BlockSpec((1, d), lambda i: (0, 0))
    ins, specs, scratch = [], [], []
    if combine:
        ys, dest, wt = moe
        ins += [dest, dest]
        specs += [pl.BlockSpec((TOP_K, tm), lambda i: (0, 0), memory_space=pltpu.SMEM),
                  pl.BlockSpec((TOP_K, tm), lambda i: (0, jnp.minimum(i + 1, nblk - 1)), memory_space=pltpu.SMEM)]
    if combine:
        ins.append(x)
        specs.append(row_spec)
    else:
        ins += list(x)
        specs += [pl.BlockSpec((tm, d), lambda i: (jnp.minimum(i, lat_blocks - 1), 0)),
                  pl.BlockSpec((tm, d), lambda i: (jnp.maximum(i - lat_blocks, 0), 0))]
    if combine:
        ins += [ys, wt, mods_prev]
        specs += [pl.BlockSpec(memory_space=pl.ANY), pl.BlockSpec((tm, TOP_K), lambda i: (i, 0)), mod_spec]
        scratch = [pltpu.VMEM((2 * TOP_K * tm, d), F32), pltpu.SemaphoreType.DMA((2,))]
    ins += [mods_cur, gain.reshape(1, d)]
    specs += [mod_spec, g_spec]
    if final:
        out_shape = jax.ShapeDtypeStruct((n_lat, d), F32)
        out_specs = row_spec
    else:
        out_shape = (jax.ShapeDtypeStruct((t, d), F32), jax.ShapeDtypeStruct((t, d), BF16))
        out_specs = (row_spec, row_spec)
    return pl.pallas_call(
        functools.partial(_prenorm_kernel, combine=combine, final=final, d=d, tm=tm, nblk=nblk,
                          lat_blocks=lat_blocks),
        out_shape=out_shape, grid=(nblk,), in_specs=specs, out_specs=out_specs, scratch_shapes=scratch,
        compiler_params=_cparams(("arbitrary",), 40),
        name="prenorm_final" if final else ("prenorm_combine" if combine else "prenorm"),
    )(*ins)


def _inproj_kernel(h_ref, w_ref, pu_ref, pr_ref):
    acc = jnp.dot(h_ref[...], w_ref[...], preferred_element_type=F32)

    @pl.when(pl.program_id(1) == 0)
    def _():
        pu_ref[...] = acc

    @pl.when(pl.program_id(1) > 0)
    def _():
        pr_ref[...] = acc.astype(pr_ref.dtype)


def _in_proj(h, w, layer, s5w):
    t, k = h.shape
    n = w.shape[2]
    tm, tn = MM_ROW_BLOCK, MM_COL_BLOCK
    assert s5w == tn
    return pl.pallas_call(
        _inproj_kernel,
        out_shape=(jax.ShapeDtypeStruct((t, s5w), F32), jax.ShapeDtypeStruct((t, n - s5w), BF16)),
        grid=(t // tm, n // tn),
        in_specs=[pl.BlockSpec((tm, k), lambda i, j: (i, 0)),
                  pl.BlockSpec((None, k, tn), lambda i, j: (layer, 0, j))],
        out_specs=(pl.BlockSpec((tm, tn), lambda i, j: (i, 0)),
                   pl.BlockSpec((tm, tn), lambda i, j: (i, jnp.maximum(j - 1, 0)))),
        compiler_params=_cparams(("arbitrary", "arbitrary"), 40),
        name="in_proj",
    )(h, w)


def _s5_weights(a_re, a_im, log_dt, b_re, b_im, c_re, c_im):
    ns = S5_STEP
    a_re, a_im = a_re.astype(F32), a_im.astype(F32)
    dt = jnp.exp(log_dt.astype(F32))[..., None]
    lam_re, lam_im = dt * a_re, dt * a_im
    tau = jnp.arange(ns + 1, dtype=F32)[:, None, None, None]
    mag = jnp.exp(tau * lam_re[None])
    p_re, p_im = mag * jnp.cos(tau * lam_im[None]), mag * jnp.sin(tau * lam_im[None])
    x, y = p_re[1] - 1.0, p_im[1]
    den = a_re * a_re + a_im * a_im
    k_re, k_im = (x * a_re + y * a_im) / den, (y * a_re - x * a_im) / den
    b_re, b_im = b_re.astype(F32), b_im.astype(F32)
    bb_re = k_re[..., None] * b_re - k_im[..., None] * b_im
    bb_im = k_re[..., None] * b_im + k_im[..., None] * b_re
    c_re, c_im = c_re.astype(F32), c_im.astype(F32)
    cp_re = c_re[None] * p_re[:, :, :, None, :] - c_im[None] * p_im[:, :, :, None, :]
    cp_im = c_re[None] * p_im[:, :, :, None, :] + c_im[None] * p_re[:, :, :, None, :]
    kmat = jnp.einsum('tdgon,dgni->tdgoi', jnp.concatenate([cp_re, -cp_im], axis=-1),
                      jnp.concatenate([bb_re, bb_im], axis=-2))
    lag = np.arange(ns)[None, :] - np.arange(ns)[:, None]
    sel_f = jnp.asarray(lag[None] == np.arange(ns)[:, None, None], F32)
    sel_b = jnp.asarray(-lag[None] == np.arange(ns)[:, None, None], F32)
    g = a_re.shape[1]
    m = (jnp.einsum('lst,lgoi->gsito', sel_f, kmat[:ns, 0]) + jnp.einsum('lst,lgoi->gsito', sel_b, kmat[:ns, 1]))
    m = m.reshape(g, ns * S5_GROUP, ns * S5_GROUP)

    def inject(p_r, p_i, d):
        r = p_r[:, :, :, None] * bb_re[d][None] - p_i[:, :, :, None] * bb_im[d][None]
        i = p_r[:, :, :, None] * bb_im[d][None] + p_i[:, :, :, None] * bb_re[d][None]
        tr = lambda z: jnp.transpose(z, (1, 0, 3, 2)).reshape(g, ns * S5_GROUP, S5_STATE)
        return tr(r), tr(i)

    pf_re, pf_im = inject(p_re[ns - 1::-1, 0], p_im[ns - 1::-1, 0], 0)
    pb_re, pb_im = inject(p_re[:ns, 1], p_im[:ns, 1], 1)
    w1 = jnp.concatenate([z.astype(BF16) for z in (m, pf_re, pb_re, pf_im, pb_im)], axis=-1)

    def carry(cp_t):
        return jnp.transpose(cp_t, (1, 3, 0, 2)).reshape(g, S5_STATE, ns * S5_GROUP)

    up = slice(1, ns + 1)
    pd_re, pd_im = p_re[ns:0:-1, 1], p_im[ns:0:-1, 1]
    cpd_re = c_re[1][None] * pd_re[:, :, None, :] - c_im[1][None] * pd_im[:, :, None, :]
    cpd_im = c_re[1][None] * pd_im[:, :, None, :] + c_im[1][None] * pd_re[:, :, None, :]
    q = jnp.concatenate([carry(cp_re[up, 0]).astype(BF16), carry(cpd_re).astype(BF16),
                         (-carry(cp_im[up, 0])).astype(BF16), (-carry(cpd_im)).astype(BF16)],
                        axis=1)
    a16 = jnp.stack([jnp.concatenate([p_re[ns, 0], p_re[ns, 1]], axis=-1),
                     jnp.concatenate([p_im[ns, 0], p_im[ns, 1]], axis=-1)], axis=1)
    return w1.astype(BF16), q.astype(BF16), a16


def _group_transpose(xs):
    ng = len(xs)
    grp = lax.broadcasted_iota(jnp.int32, xs[0].shape, 1) // S5_GROUP
    rolled = []
    for j in range(ng):
        y = xs[j % ng]
        for k in range(1, ng):
            y = jnp.where(grp == k, xs[(k + j) % ng], y)
        rolled.append(y if j == 0 else pltpu.roll(y, S5_GROUP * j, 1))
    outs = []
    for a in range(ng):
        o = rolled[(-a) % ng]
        for b in range(1, ng):
            o = jnp.where(grp == b, rolled[(b - a) % ng], o)
        outs.append(o)
    return outs


def _s5_kernel(ul_ref, uc_ref, w1_ref, q_ref, a_ref, yl_ref, yc_ref, uflat, st_re, st_im, yfl, *,
               lat_chunks, ctx_chunks):
    ng = S5_TILE_GROUPS
    width = S5_STEP * S5_GROUP
    n_chunks = lat_chunks + ctx_chunks
    sub = 32

    def flatten(src_ref, c0, row0, n):
        for hh in range(2):
            xs = [src_ref[pl.ds(row0 + hh * 8 + t, n, stride=S5_STEP), :] for t in range(8)]
            outs = _group_transpose(xs)
            for k in range(ng):
                uflat[k, pl.ds(c0, n), hh * 128:(hh + 1) * 128] = outs[k].astype(BF16)

    def flat_body(i, _):
        c0 = pl.multiple_of(i * sub, sub)
        flatten(ul_ref, c0, c0 * S5_STEP, sub)
        return 0

    lax.fori_loop(0, lat_chunks // sub, flat_body, 0)
    flatten(uc_ref, lat_chunks, 0, ctx_chunks)

    for k in range(ng):
        z = jnp.dot(uflat[k], w1_ref[k], preferred_element_type=F32)
        yfl[k] = z[:, :width]
        st_re[pl.ds(k, n_chunks, stride=ng), :] = z[:, width:width + 128]
        st_im[pl.ds(k, n_chunks, stride=ng), :] = z[:, width + 128:]

    lane = lax.broadcasted_iota(jnp.int32, (ng, 128), 1)
    fwd_lane = lane < S5_STATE
    a_re = a_ref[0]
    a_im = a_ref[1]

    def make_body(base, n):
        def body(i, carry):
            h_re, h_im = carry
            rf = pl.multiple_of((base + i) * ng, ng)
            rb = pl.multiple_of((base + n - 1 - i) * ng, ng)
            tf_re, tf_im = st_re[pl.ds(rf, ng), :], st_im[pl.ds(rf, ng), :]
            tb_re, tb_im = st_re[pl.ds(rb, ng), :], st_im[pl.ds(rb, ng), :]
            t_re = jnp.where(fwd_lane, tf_re, tb_re)
            t_im = jnp.where(fwd_lane, tf_im, tb_im)
            n_re = a_re * h_re - a_im * h_im + t_re
            n_im = a_re * h_im + a_im * h_re + t_im
            st_re[pl.ds(rf, ng), :] = jnp.where(fwd_lane, h_re, tf_re)
            st_im[pl.ds(rf, ng), :] = jnp.where(fwd_lane, h_im, tf_im)
            st_re[pl.ds(rb, ng), :] = jnp.where(fwd_lane, tb_re, h_re)
            st_im[pl.ds(rb, ng), :] = jnp.where(fwd_lane, tb_im, h_im)
            return n_re, n_im
        return body

    carry = (jnp.zeros((ng, 128), F32), jnp.zeros((ng, 128), F32))
    carry = lax.fori_loop(0, ctx_chunks, make_body(lat_chunks, ctx_chunks), carry)
    carry = lax.fori_loop(0, lat_chunks, make_body(0, lat_chunks), carry)

    for k in range(ng):
        hk = jnp.concatenate([st_re[pl.ds(k, n_chunks, stride=ng), :], st_im[pl.ds(k, n_chunks, stride=ng), :]],
                             axis=1).astype(BF16)
        yfl[k] = yfl[k] + jnp.dot(hk, q_ref[k], preferred_element_type=F32)

    def unflatten(dst_ref, c0, row0, n):
        for hh in range(2):
            vs = [yfl[k, pl.ds(c0, n), hh * 128:(hh + 1) * 128] for k in range(ng)]
            ts = _group_transpose(vs)
            for t in range(8):
                dst_ref[pl.ds(row0 + hh * 8 + t, n, stride=S5_STEP), :] = ts[t]

    def unflat_body(i, _):
        c0 = pl.multiple_of(i * sub, sub)
        unflatten(yl_ref, c0, c0 * S5_STEP, sub)
        return 0

    lax.fori_loop(0, lat_chunks // sub, unflat_body, 0)
    unflatten(yc_ref, lat_chunks, 0, ctx_chunks)


def _s5_scan(pu, w1, q, a16, layer, *, bsz, n_lat, n_ctx):
    g = w1.shape[1]
    ng = S5_TILE_GROUPS
    width = S5_STEP * S5_GROUP
    lat_chunks, ctx_chunks = n_lat // S5_STEP, n_ctx // S5_STEP
    assert lat_chunks % 32 == 0 and ctx_chunks % 2 == 0 and ctx_chunks % 16 == 0
    n_chunks = lat_chunks + ctx_chunks
    ctx_blk0 = bsz * n_lat // n_ctx
    a_t = jnp.transpose(a16.reshape(-1, g // ng, ng, 2, 128), (0, 1, 3, 2, 4))
    kern = functools.partial(_s5_kernel, lat_chunks=lat_chunks, ctx_chunks=ctx_chunks)
    return pl.pallas_call(
        kern,
        out_shape=(jax.ShapeDtypeStruct((bsz * n_lat, g * S5_GROUP), F32),
                   jax.ShapeDtypeStruct((bsz * n_ctx, g * S5_GROUP), F32)),
        grid=(g // ng, bsz),
        in_specs=[pl.BlockSpec((n_lat, 128), lambda i, b: (b, i)),
                  pl.BlockSpec((n_ctx, 128), lambda i, b: (ctx_blk0 + b, i)),
                  pl.BlockSpec((None, ng, width, 2 * width), lambda i, b: (layer, i, 0, 0)),
                  pl.BlockSpec((None, ng, width, width), lambda i, b: (layer, i, 0, 0)),
                  pl.BlockSpec((None, None, 2, ng, 128), lambda i, b: (layer, i, 0, 0, 0))],
        out_specs=(pl.BlockSpec((n_lat, 128), lambda i, b: (b, i)),
                   pl.BlockSpec((n_ctx, 128), lambda i, b: (b, i))),
        scratch_shapes=[pltpu.VMEM((ng, n_chunks, width), BF16),
                        pltpu.VMEM((n_chunks * ng, 128), F32),
                        pltpu.VMEM((n_chunks * ng, 128), F32),
                        pltpu.VMEM((ng, n_chunks, width), F32)],
        compiler_params=_cparams(("arbitrary", "arbitrary"), 40),
        name="s5_scan",
    )(pu, pu, w1, q, a_t)


def _ret_kernel(lg_ref, ql_ref, kl_ref, vl_ref, gl_ref, qc_ref, kc_ref, vc_ref, gc_ref, cos_ref, sin_ref,
                ol_ref, oc_ref, qs, ks, vs, acc, sf, sb, dm, *, n_ctx, n_lat):
    c = RET_BLOCK
    half = RET_HEAD_DIM // 2
    head = pl.program_id(1)
    lgf = lg_ref[0, head]
    lgb = lg_ref[1, head]
    qscale = RET_HEAD_DIM ** -0.5
    ctx_rows = n_ctx * c

    ii = lax.broadcasted_iota(jnp.int32, (c, c), 0)
    jj = lax.broadcasted_iota(jnp.int32, (c, c), 1)
    rel = (ii - jj).astype(F32)
    dsum = (jnp.where(rel >= 0, jnp.exp(lgf * jnp.maximum(rel, 0.0)), 0.0)
            + jnp.where(rel <= 0, jnp.exp(lgb * jnp.maximum(-rel, 0.0)), 0.0))
    pos = lax.broadcasted_iota(jnp.int32, (c, 1), 0).astype(F32)
    qdec_f = jnp.exp(lgf * (pos + 1.0))
    kdec_f = jnp.exp(lgf * (c - 1.0 - pos))
    qdec_b = jnp.exp(lgb * (c - pos))
    kdec_b = jnp.exp(lgb * pos)
    cg_f = jnp.exp(lgf * c)
    cg_b = jnp.exp(lgb * c)

    qs[0:ctx_rows, :] = (qc_ref[...].astype(F32) * qscale).astype(BF16)
    ks[0:ctx_rows, :] = kc_ref[...]
    vs[0:ctx_rows, :] = vc_ref[...]

    def rope_body(n, _):
        r = pl.multiple_of(n * c, c)
        cs = cos_ref[pl.ds(r, c), :]
        sn = sin_ref[pl.ds(r, c), :]

        def rot(t):
            t1, t2 = t[:, :half], t[:, half:]
            return jnp.concatenate([t1 * cs - t2 * sn, t2 * cs + t1 * sn], axis=1)

        dst = pl.multiple_of(ctx_rows + r, c)
        qs[pl.ds(dst, c), :] = (rot(ql_ref[pl.ds(r, c), :].astype(F32)) * qscale).astype(BF16)
        ks[pl.ds(dst, c), :] = rot(kl_ref[pl.ds(r, c), :].astype(F32)).astype(BF16)
        vs[pl.ds(dst, c), :] = vl_ref[pl.ds(r, c), :]
        return 0

    lax.fori_loop(0, n_lat, rope_body, 0)

    dm[...] = dsum
    sf[...] = jnp.zeros_like(sf)
    sb[...] = jnp.zeros_like(sb)

    def load(r):
        return qs[pl.ds(r, c), :], ks[pl.ds(r, c), :], vs[pl.ds(r, c), :]

    def intra_of(q, k, v):
        s = lax.dot_general(q, k, (((1,), (1,)), ((), ())), preferred_element_type=F32)
        return jnp.dot((s * dm[...]).astype(BF16), v, preferred_element_type=F32)

    def step(s_ref, q, k, v, qdec, kdec, cg):
        inter = jnp.dot(q, s_ref[...].astype(BF16), preferred_element_type=F32) * qdec
        kd = (k.astype(F32) * kdec).astype(BF16)
        s_ref[...] = cg * s_ref[...] + lax.dot_general(kd, v, (((0,), (0,)), ((), ())),
                                                       preferred_element_type=F32)
        return inter

    def finish(o, gate, out_ref, ro):
        o = o * lax.rsqrt(jnp.mean(o * o, axis=-1, keepdims=True) + NORM_EPS)
        out_ref[pl.ds(ro, c), :] = (o * (gate * jax.nn.sigmoid(gate))).astype(out_ref.dtype)

    fwd = (sf, qdec_f, kdec_f, cg_f)
    bwd = (sb, qdec_b, kdec_b, cg_b)

    for n in range(n_ctx):
        q, k, v = load(n * c)
        acc[n * c:(n + 1) * c, :] = intra_of(q, k, v) + step(fwd[0], q, k, v, *fwd[1:])
    for n in range(n_ctx - 1, -1, -1):
        q, k, v = load(n * c)
        o = acc[n * c:(n + 1) * c, :] + step(bwd[0], q, k, v, *bwd[1:])
        finish(o, gc_ref[n * c:(n + 1) * c, :].astype(F32), oc_ref, n * c)

    def first_visits(i, _):
        rf = pl.multiple_of(i * c, c)
        rb = pl.multiple_of((n_lat - 1 - i) * c, c)
        q, k, v = load(ctx_rows + rf)
        acc[pl.ds(ctx_rows + rf, c), :] = intra_of(q, k, v) + step(fwd[0], q, k, v, *fwd[1:])
        q, k, v = load(ctx_rows + rb)
        acc[pl.ds(ctx_rows + rb, c), :] = step(bwd[0], q, k, v, *bwd[1:])
        return 0

    def second_visits(i, _):
        rf = pl.multiple_of(i * c, c)
        rb = pl.multiple_of((n_lat - 1 - i) * c, c)
        q, k, v = load(ctx_rows + rf)
        o = acc[pl.ds(ctx_rows + rf, c), :] + intra_of(q, k, v) + step(fwd[0], q, k, v, *fwd[1:])
        finish(o, gl_ref[pl.ds(rf, c), :].astype(F32), ol_ref, rf)
        q, k, v = load(ctx_rows + rb)
        o = acc[pl.ds(ctx_rows + rb, c), :] + step(bwd[0], q, k, v, *bwd[1:])
        finish(o, gl_ref[pl.ds(rb, c), :].astype(F32), ol_ref, rb)
        return 0

    lax.fori_loop(0, n_lat // 2, first_visits, 0)
    lax.fori_loop(n_lat // 2, n_lat, second_visits, 0)


def _retention(p, log_gamma, cos, sin, *, bsz, n_lat, n_ctx, col0):
    t = p.shape[0]
    dh = RET_HEAD_DIM
    heads = (p.shape[1] - col0) // (4 * dh)
    qb, kb, vb, gb = (col0 // dh + i * heads for i in range(4))
    ctx_blk0 = bsz * n_lat // n_ctx

    def lat(cb):
        return pl.BlockSpec((n_lat, dh), lambda b, h: (b, cb + h))

    def ctx(cb):
        return pl.BlockSpec((n_ctx, dh), lambda b, h: (ctx_blk0 + b, cb + h))

    tab = pl.BlockSpec((n_lat, dh // 2), lambda b, h: (0, 0))
    assert n_ctx % RET_BLOCK == 0 and (n_lat // RET_BLOCK) % 2 == 0
    kern = functools.partial(_ret_kernel, n_ctx=n_ctx // RET_BLOCK, n_lat=n_lat // RET_BLOCK)
    seq = n_ctx + n_lat
    return pl.pallas_call(
        kern,
        out_shape=(jax.ShapeDtypeStruct((bsz * n_lat, heads * dh), BF16),
                   jax.ShapeDtypeStruct((bsz * n_ctx, heads * dh), BF16)),
        grid=(bsz, heads),
        in_specs=[pl.BlockSpec(memory_space=pltpu.SMEM),
                  lat(qb), lat(kb), lat(vb), lat(gb), ctx(qb), ctx(kb), ctx(vb), ctx(gb), tab, tab],
        out_specs=(pl.BlockSpec((n_lat, dh), lambda b, h: (b, h)),
                   pl.BlockSpec((n_ctx, dh), lambda b, h: (b, h))),
        scratch_shapes=[pltpu.VMEM((seq, dh), BF16), pltpu.VMEM((seq, dh), BF16), pltpu.VMEM((seq, dh), BF16),
                        pltpu.VMEM((seq, dh), F32), pltpu.VMEM((dh, dh), F32), pltpu.VMEM((dh, dh), F32),
                        pltpu.VMEM((RET_BLOCK, RET_BLOCK), F32)],
        compiler_params=_cparams(("arbitrary", "arbitrary"), 52),
        name="retention",
    )(log_gamma, p, p, p, p, p, p, p, p, cos, sin)


def _top2_sum(a, b, c, d):
    hi1, lo1 = jnp.maximum(a, b), jnp.minimum(a, b)
    hi2, lo2 = jnp.maximum(c, d), jnp.minimum(c, d)
    return jnp.maximum(hi1, hi2) + jnp.maximum(jnp.minimum(hi1, hi2), jnp.maximum(lo1, lo2))


def _route(logits_t, rb):
    sc = jax.nn.sigmoid(logits_t)
    biased = sc + rb
    per = EXPERTS_PER_GROUP
    brow = [biased[e:e + 1, :] for e in range(N_EXPERTS)]
    srow = [sc[e:e + 1, :] for e in range(N_EXPERTS)]
    gscore = [_top2_sum(*brow[per * g:per * g + per]) for g in range(N_EXPERT_GROUPS)]
    best_v = gscore[0]
    best_g = jnp.zeros_like(best_v, dtype=jnp.int32)
    for g in range(1, N_EXPERT_GROUPS):
        upd = gscore[g] > best_v
        best_v = jnp.where(upd, gscore[g], best_v)
        best_g = jnp.where(upd, g, best_g)
    vals, sels = [], []
    for j in range(per):
        v = brow[j]
        s = srow[j]
        for g in range(1, N_EXPERT_GROUPS):
            v = jnp.where(best_g == g, brow[per * g + j], v)
            s = jnp.where(best_g == g, srow[per * g + j], s)
        vals.append(v)
        sels.append(s)
    v1, i1, s1 = vals[0], jnp.zeros_like(best_g), sels[0]
    for j in range(1, per):
        upd = vals[j] > v1
        v1 = jnp.where(upd, vals[j], v1)
        s1 = jnp.where(upd, sels[j], s1)
        i1 = jnp.where(upd, j, i1)
    v2 = jnp.full_like(v1, -jnp.inf)
    i2, s2 = jnp.zeros_like(best_g), jnp.zeros_like(s1)
    for j in range(per):
        upd = (i1 != j) & (vals[j] > v2)
        v2 = jnp.where(upd, vals[j], v2)
        s2 = jnp.where(upd, sels[j], s2)
        i2 = jnp.where(upd, j, i2)
    tot = s1 + s2
    idx = jnp.concatenate([best_g * per + i1, best_g * per + i2], axis=0)
    wts = jnp.concatenate([s1 / tot, s2 / tot], axis=0)
    return idx, wts


def _mix_kernel(ysl_ref, ysc_ref, u_ref, rl_ref, rc_ref, x_ref, m_ref, dsk_ref, gw_ref, gb_ref, ws_ref, wr_ref,
                g2_ref, rw_ref, rb_ref, xo_ref, fu_ref, ei_ref, ew_ref, rank_ref, cnt_ref, run_ref, *, lat_blocks, d):
    is_lat = pl.program_id(0) < lat_blocks
    ys = jnp.where(is_lat, ysl_ref[...], ysc_ref[...])
    y = ys + dsk_ref[...] * u_ref[...]
    y = jax.nn.gelu(y, approximate=True)
    z = jnp.dot(y.astype(BF16), gw_ref[...], preferred_element_type=F32) + gb_ref[...]
    s5o = (y * jax.nn.sigmoid(z)).astype(BF16)
    r = jnp.where(is_lat, rl_ref[...], rc_ref[...])
    mixed = (jnp.dot(s5o, ws_ref[...], preferred_element_type=F32)
             + jnp.dot(r, wr_ref[...], preferred_element_type=F32))
    gate1 = m_ref[:, 2 * d:3 * d]
    x = x_ref[...] + gate1 * mixed
    xo_ref[...] = x
    ms = jnp.mean(x * x, axis=-1, keepdims=True)
    f = x * lax.rsqrt(ms + NORM_EPS) * g2_ref[...]
    f = f * (1.0 + m_ref[:, 4 * d:5 * d]) + m_ref[:, 3 * d:4 * d]
    tm = f.shape[0]
    fu_ref[...] = f
    logits_t = lax.dot_general(rw_ref[...], f, (((1,), (1,)), ((), ())), precision=HIGHEST,
                               preferred_element_type=F32)
    idx, wts = _route(logits_t, rb_ref[...])
    ei_ref[...] = idx
    ew_ref[...] = wts

    @pl.when(pl.program_id(0) == 0)
    def _():
        run_ref[...] = jnp.zeros_like(run_ref)

    e_iota = lax.broadcasted_iota(jnp.int32, (N_EXPERTS, tm), 0)
    tri = jnp.where(lax.broadcasted_iota(jnp.int32, (tm, tm), 0) <= lax.broadcasted_iota(jnp.int32, (tm, tm), 1),
                    1.0, 0.0).astype(BF16)
    hit0 = e_iota == idx[0:1, :]
    hit1 = e_iota == idx[1:2, :]
    p0 = jnp.dot(jnp.where(hit0, 1.0, 0.0).astype(BF16), tri, preferred_element_type=F32)
    p1 = jnp.dot(jnp.where(hit1, 1.0, 0.0).astype(BF16), tri, preferred_element_type=F32)
    tot0 = p0[:, tm - 1:tm]
    tot1 = p1[:, tm - 1:tm]
    before = run_ref[...]
    r0 = jnp.sum(jnp.where(hit0, before + p0, 0.0), axis=0, keepdims=True) - 1.0
    r1 = jnp.sum(jnp.where(hit1, before + tot0 + p1, 0.0), axis=0, keepdims=True) - 1.0
    rank_ref[...] = jnp.concatenate([r0, r1], axis=0).astype(jnp.int32)
    after = before + tot0 + tot1
    run_ref[...] = after
    cnt_ref[...] = after.astype(jnp.int32)


def _mix(ysl, ysc, pu, rl, rc, x, mods_cur, d_skip, glu_w, glu_b, w_out, g2, rw_t, rb, *, layer, n_lat,
         rows_per_batch):
    t, d = x.shape
    sw = pu.shape[1]
    tm = MIX_ROW_BLOCK
    lat_blocks = n_lat // tm
    per_b = rows_per_batch // tm
    ctx_row = n_lat // rows_per_batch
    const = lambda i: (0, 0)
    row = lambda i: (i, 0)
    lat_row = lambda i: (jnp.minimum(i, lat_blocks - 1), 0)
    ctx_rowblk = lambda i: (jnp.maximum(i - lat_blocks, 0), 0)
    once = pl.Buffered(1)
    kern = functools.partial(_mix_kernel, lat_blocks=lat_blocks, d=d)
    return pl.pallas_call(
        kern,
        out_shape=(jax.ShapeDtypeStruct((t, d), F32), jax.ShapeDtypeStruct((t, d), F32),
                   jax.ShapeDtypeStruct((TOP_K, t), jnp.int32), jax.ShapeDtypeStruct((TOP_K, t), F32),
                   jax.ShapeDtypeStruct((TOP_K, t), jnp.int32), jax.ShapeDtypeStruct((N_EXPERTS, 1), jnp.int32)),
        grid=(t // tm,),
        scratch_shapes=[pltpu.VMEM((N_EXPERTS, 1), F32)],
        in_specs=[pl.BlockSpec((tm, sw), lat_row),
                  pl.BlockSpec((tm, sw), ctx_rowblk, pipeline_mode=once),
                  pl.BlockSpec((tm, sw), row),
                  pl.BlockSpec((tm, sw), lat_row),
                  pl.BlockSpec((tm, sw), ctx_rowblk, pipeline_mode=once),
                  pl.BlockSpec((tm, d), row),
                  pl.BlockSpec((None, 1, 6 * d), lambda i: (jnp.where(i < lat_blocks, i // per_b, ctx_row), 0, 0)),
                  pl.BlockSpec((1, sw), const),
                  pl.BlockSpec((None, sw, sw), lambda i: (layer, 0, 0), pipeline_mode=once),
                  pl.BlockSpec((1, sw), const),
                  pl.BlockSpec((None, sw, d), lambda i: (layer, 0, 0), pipeline_mode=once),
                  pl.BlockSpec((None, d - sw, d), lambda i: (layer, 1, 0), pipeline_mode=once),
                  pl.BlockSpec((1, d), const),
                  pl.BlockSpec((N_EXPERTS, d), const),
                  pl.BlockSpec((N_EXPERTS, 1), const)],
        out_specs=(pl.BlockSpec((tm, d), row), pl.BlockSpec((tm, d), row),
                   pl.BlockSpec((TOP_K, tm), lambda i: (0, i)), pl.BlockSpec((TOP_K, tm), lambda i: (0, i)),
                   pl.BlockSpec((TOP_K, tm), lambda i: (0, i)), pl.BlockSpec((N_EXPERTS, 1), const)),
        compiler_params=_cparams(("arbitrary",), 58),
        name="mix",
    )(ysl, ysc, pu, rl, rc, x, mods_cur, d_skip, glu_w, glu_b, w_out, w_out, g2, rw_t, rb)


def _dispatch_plan(eidx, rank, counts):
    bm = MOE_ROWS
    n_assign = eidx.size
    nblk = -(-n_assign // bm) + N_EXPERTS
    counts = counts.reshape(N_EXPERTS)
    padded = (counts + bm - 1) // bm * bm
    pend = jnp.cumsum(padded)
    pstart = pend - padded
    experts = jnp.arange(N_EXPERTS, dtype=jnp.int32)[:, None, None]
    dest = rank + jnp.sum(jnp.where(eidx[None] == experts, pstart[:, None, None], 0), axis=0)
    blk_start = jnp.arange(nblk, dtype=jnp.int32) * bm
    bexp = jnp.minimum(jnp.sum((pend[None, :] <= blk_start[:, None]).astype(jnp.int32), axis=1), N_EXPERTS - 1)
    nused = (pend[-1] // bm).astype(jnp.int32).reshape(1)
    return dest.astype(jnp.int32), bexp.astype(jnp.int32), nused, counts.astype(jnp.int32), pstart.astype(jnp.int32)


def _dispatch_kernel(cnt_ref, pst_ref, nused_ref, dest_ref, fu_ref, wg_ref, wu_ref, wd_ref,
                     xs_hbm, wgb_ref, wub_ref, wdb_ref, zblk, sem, zsem, *, tb, bm, nblk, n_conv):
    i = pl.program_id(0)

    for tt in range(tb):
        for k in range(TOP_K):
            pltpu.make_async_copy(fu_ref.at[pl.ds(tt, 1)], xs_hbm.at[pl.ds(dest_ref[k, tt], 1)],
                                  sem).start(priority=k)

    @pl.when(i < n_conv)
    def _():
        wgb_ref[...] = wg_ref[...].astype(BF16)
        wub_ref[...] = wu_ref[...].astype(BF16)
        wdb_ref[...] = wd_ref[...].astype(BF16)

    @pl.when(i == 0)
    def _():
        zblk[...] = jnp.zeros_like(zblk)
        zrow = zblk.at[pl.ds(0, 1)]

        def pad_rows(e, _):
            first = pst_ref[e] + cnt_ref[e]
            npad = (bm - cnt_ref[e] % bm) % bm

            def start(r, _):
                pltpu.make_async_copy(zrow, xs_hbm.at[pl.ds(first + r, 1)], zsem).start()
                return 0

            def wait(r, _):
                pltpu.make_async_copy(zrow, xs_hbm.at[pl.ds(0, 1)], zsem).wait()
                return 0

            lax.fori_loop(0, npad, start, 0)
            lax.fori_loop(0, npad, wait, 0)
            return 0

        lax.fori_loop(0, N_EXPERTS, pad_rows, 0)

        def pad_block(j, _):
            cp = pltpu.make_async_copy(zblk, xs_hbm.at[pl.ds(pl.multiple_of(j * bm, bm), bm)], zsem)
            cp.start()
            cp.wait()
            return 0

        lax.fori_loop(nused_ref[0], nblk, pad_block, 0)

    for k in range(TOP_K):
        pltpu.make_async_copy(fu_ref, xs_hbm.at[pl.ds(0, tb)], sem).wait()


def _dispatch(fu, dest, counts, pstart, nused, wg, wu, wd, layer, *, nblk):
    n_tok, d = fu.shape
    bm = MOE_ROWS
    tb = ROW_BLOCK
    assert n_tok % tb == 0
    steps = n_tok // tb
    depth, n_exp, _, ff = wg.shape
    n_conv = 1 << (steps.bit_length() - 1)
    rows_up, rows_dn = n_exp * d // n_conv, n_exp * ff // n_conv
    assert rows_up * n_conv == n_exp * d and rows_dn * n_conv == n_exp * ff and rows_dn % 16 == 0
    slab = lambda i, c, p, n: (layer, jnp.minimum(i, n_conv - 1), 0)
    out_slab = lambda i, c, p, n: (jnp.minimum(i, n_conv - 1), 0)
    kern = functools.partial(_dispatch_kernel, tb=tb, bm=bm, nblk=nblk, n_conv=n_conv)
    grid_spec = pltpu.PrefetchScalarGridSpec(
        num_scalar_prefetch=3,
        grid=(steps,),
        in_specs=[pl.BlockSpec((TOP_K, tb), lambda i, c, p, n: (0, i), memory_space=pltpu.SMEM),
                  pl.BlockSpec((tb, d), lambda i, c, p, n: (i, 0)),
                  pl.BlockSpec((None, rows_up, ff), slab),
                  pl.BlockSpec((None, rows_up, ff), slab),
                  pl.BlockSpec((None, rows_dn, d), slab)],
        out_specs=(pl.BlockSpec(memory_space=pl.ANY),
                   pl.BlockSpec((rows_up, ff), out_slab),
                   pl.BlockSpec((rows_up, ff), out_slab),
                   pl.BlockSpec((rows_dn, d), out_slab)),
        scratch_shapes=[pltpu.VMEM((bm, d), F32),
                        pltpu.SemaphoreType.DMA(()), pltpu.SemaphoreType.DMA(())],
    )
    xs, wgb, wub, wdb = pl.pallas_call(
        kern,
        out_shape=(jax.ShapeDtypeStruct((nblk * bm, d), F32),
                   jax.ShapeDtypeStruct((n_exp * d, ff), BF16),
                   jax.ShapeDtypeStruct((n_exp * d, ff), BF16),
                   jax.ShapeDtypeStruct((n_exp * ff, d), BF16)),
        grid_spec=grid_spec,
        compiler_params=_cparams(("arbitrary",), 40),
        name="dispatch",
    )(counts, pstart, nused, dest, fu, wg.reshape(depth, n_exp * d, ff), wu.reshape(depth, n_exp * d, ff),
      wd.reshape(depth, n_exp * ff, d))
    return xs, wgb.reshape(n_exp, d, ff), wub.reshape(n_exp, d, ff), wdb.reshape(n_exp, ff, d)


def _expert_kernel(bexp_ref, nused_ref, xs_ref, wg_ref, wu_ref, wd_ref, ys_ref):
    @pl.when(pl.program_id(0) < nused_ref[0])
    def _():
        xb = xs_ref[...].astype(BF16)
        gt = jnp.dot(xb, wg_ref[...], preferred_element_type=F32)
        up = jnp.dot(xb, wu_ref[...], preferred_element_type=F32)
        hmid = (gt * jax.nn.sigmoid(gt) * up).astype(BF16)
        ys_ref[...] = jnp.dot(hmid, wd_ref[...], preferred_element_type=F32)

    @pl.when(pl.program_id(0) >= nused_ref[0])
    def _():
        ys_ref[...] = jnp.zeros_like(ys_ref)


def _experts(xs, bexp, nused, wg, wu, wd):
    bm = MOE_ROWS
    nblk = bexp.shape[0]
    d, ff = wg.shape[1], wg.shape[2]
    grid_spec = pltpu.PrefetchScalarGridSpec(
        num_scalar_prefetch=2,
        grid=(nblk,),
        in_specs=[pl.BlockSpec((bm, d), lambda j, be, nu: (j, 0)),
                  pl.BlockSpec((None, d, ff), lambda j, be, nu: (be[j], 0, 0)),
                  pl.BlockSpec((None, d, ff), lambda j, be, nu: (be[j], 0, 0)),
                  pl.BlockSpec((None, ff, d), lambda j, be, nu: (be[j], 0, 0))],
        out_specs=pl.BlockSpec((bm, d), lambda j, be, nu: (j, 0)),
    )
    return pl.pallas_call(
        _expert_kernel,
        out_shape=jax.ShapeDtypeStruct((nblk * bm, d), F32),
        grid_spec=grid_spec,
        compiler_params=_cparams(("arbitrary",), 52),
        name="experts",
    )(bexp, nused, xs, wg, wu, wd)


def _rope_tables(n_lat):
    n_rows = n_lat // GRID_W
    pos = np.arange(n_lat)
    quarter = RET_HEAD_DIM // 4
    freqs = ROPE_BASE ** (-np.arange(quarter, dtype=np.float64) / quarter)
    ang = np.concatenate([(pos // GRID_W)[:, None] * freqs, (pos % GRID_W)[:, None] * freqs], axis=-1)
    assert ang.shape == (n_rows * GRID_W, RET_HEAD_DIM // 2)
    return jnp.asarray(np.cos(ang), F32), jnp.asarray(np.sin(ang), F32)


def kernel(x, c, ctx, c_ctx, mod_w, mod_b, norm1_g, norm2_g, w_in, s5_a_re, s5_a_im, s5_log_dt, s5_b_re, s5_b_im,
           s5_c_re, s5_c_im, s5_d, glu_w, glu_b, ret_decay_raw, w_out, router_w, router_b, exp_w_gate, exp_w_up,
           exp_w_down, final_norm_g):
    bsz, n_lat, d = x.shape
    n_ctx = ctx.shape[1]
    depth = mod_w.shape[0]
    s5w = s5_d.shape[1]
    t = bsz * (n_lat + n_ctx)
    lat_total = bsz * n_lat

    xs = (x.reshape(lat_total, d), ctx.reshape(bsz * n_ctx, d))
    c8 = jnp.zeros((8, d), F32).at[:bsz].set(c).at[bsz].set(c_ctx)
    mods = _mods(c8, mod_w, mod_b).reshape(depth, 8, 1, 6 * d)
    cos, sin = _rope_tables(n_lat)
    rw_t = router_w.T
    rb = router_b.reshape(N_EXPERTS, 1)

    w_in_b, glu_w_b, w_out_b = w_in.astype(BF16), glu_w.astype(BF16), w_out.astype(BF16)
    assert w_out.shape[1] == 2 * s5w
    w1, q, a16 = jax.vmap(_s5_weights)(s5_a_re, s5_a_im, s5_log_dt, s5_b_re, s5_b_im, s5_c_re, s5_c_im)

    moe = None
    for layer in range(depth):
        xs, h = _prenorm(xs, moe, mods[layer - 1] if layer else None, mods[layer], norm1_g[layer],
                         n_lat=lat_total, rows_per_batch=n_lat, final=False)
        pu, pr = _in_proj(h, w_in_b, layer, s5w)

        ysl, ysc = _s5_scan(pu, w1, q, a16, layer, bsz=bsz, n_lat=n_lat, n_ctx=n_ctx)

        log_gamma = jnp.log1p(-jnp.exp2(-ret_decay_raw[layer].astype(F32)))
        rl, rc = _retention(pr, log_gamma, cos, sin, bsz=bsz, n_lat=n_lat, n_ctx=n_ctx, col0=0)

        xs, fu, eidx, ew, rank, counts = _mix(
            ysl, ysc, pu, rl, rc, xs, mods[layer], s5_d[layer].reshape(1, s5w), glu_w_b,
            glu_b[layer].reshape(1, s5w), w_out_b, norm2_g[layer].reshape(1, d), rw_t, rb,
            layer=layer, n_lat=lat_total, rows_per_batch=n_lat)

        dest, bexp, nused, counts, pstart = _dispatch_plan(eidx, rank, counts)
        xsort, wg_b, wu_b, wd_b = _dispatch(fu, dest, counts, pstart, nused, exp_w_gate, exp_w_up, exp_w_down,
                                            layer, nblk=bexp.shape[0])
        ys = _experts(xsort, bexp, nused, wg_b, wu_b, wd_b)
        moe = (ys, dest, ew.T)

    out = _prenorm(xs, moe, mods[depth - 1], mods[depth - 1], final_norm_g,
                   n_lat=lat_total, rows_per_batch=n_lat, final=True)
    return out.reshape(bsz, n_lat, d)
```
